```python
import jax, jax.numpy as jnp
from jax import lax
import numpy as np

D_MODEL = 2048
BATCH = 2
SEQ = 16384
DEPTH = 2

RET_WIDTH = D_MODEL // 2
RET_HEADS = 4
RET_HEAD_DIM = RET_WIDTH // RET_HEADS
RET_CHUNK = 128
ROPE_BASE = 10000.0
SG_WIDTH = D_MODEL - RET_WIDTH
SG_GROUPS = 8
SG_GROUP_DIM = SG_WIDTH // SG_GROUPS
SG_CHUNK = 128
MIX_WIDTH = RET_WIDTH + SG_WIDTH
IN_COLS = 4 * RET_WIDTH + 2 * SG_WIDTH
N_EXPERTS = 32
N_EXPERT_GROUPS = 8
EXPERTS_PER_GROUP = N_EXPERTS // N_EXPERT_GROUPS
TOP_K = 2
D_FF_EXPERT = 512
DISPATCH_BLOCK = 256
N_ADA = 6
EPS = 1e-6

kernel_name = "hybrid_retention_spatialgate_grouped_moe_adaln"


def _rms(x):
    xf = x.astype(jnp.float32)
    return xf * lax.rsqrt(jnp.mean(xf * xf, axis=-1, keepdims=True) + EPS)


def rms_norm(x, w):
    return (_rms(x) * w.astype(jnp.float32)).astype(x.dtype)


def rotary(x, positions):
    half = x.shape[-1] // 2
    inv_freq = ROPE_BASE ** (-jnp.arange(half, dtype=jnp.float32) / half)
    ang = positions.astype(jnp.float32)[..., None] * inv_freq
    cos = jnp.cos(ang)[:, :, None, :].astype(x.dtype)
    sin = jnp.sin(ang)[:, :, None, :].astype(x.dtype)
    x1, x2 = x[..., :half], x[..., half:]
    return jnp.concatenate([x1 * cos - x2 * sin, x2 * cos + x1 * sin], axis=-1)


def retention_chunkwise(q, k, v):
    bsz, seq, heads, dk = q.shape
    dv = v.shape[-1]
    n_chunks = seq // RET_CHUNK
    log_gamma = jnp.log1p(-jnp.exp2(-5.0 - jnp.arange(heads, dtype=jnp.float32)))
    pos = jnp.arange(RET_CHUNK, dtype=jnp.float32)
    diff = pos[:, None] - pos[None, :]
    decay_mask = jnp.where(diff >= 0, jnp.exp(log_gamma[:, None, None] * jnp.maximum(diff, 0.0)), 0.0).astype(q.dtype)
    q_decay = jnp.exp(log_gamma[:, None] * (pos + 1.0)).astype(q.dtype)[None, :, :, None]
    k_decay = jnp.exp(log_gamma[:, None] * (RET_CHUNK - 1.0 - pos)).astype(q.dtype)[None, :, :, None]
    chunk_decay = jnp.exp(log_gamma * RET_CHUNK).astype(q.dtype)[None, :, None, None]

    def to_chunks(t):
        return t.reshape(bsz, n_chunks, RET_CHUNK, heads, t.shape[-1]).transpose(1, 0, 3, 2, 4)

    def step(state, qkv):
        qc, kc, vc = qkv
        scores = jnp.einsum('bhnd,bhmd->bhnm', qc, kc) * decay_mask
        intra = jnp.einsum('bhnm,bhme->bhne', scores, vc)
        cross = jnp.einsum('bhnd,bhde->bhne', qc, state) * q_decay
        new_state = state * chunk_decay + jnp.einsum('bhmd,bhme->bhde', kc * k_decay, vc)
        return new_state, intra + cross

    state0 = jnp.zeros((bsz, heads, dk, dv), q.dtype)
    _, out = lax.scan(step, state0, (to_chunks(q), to_chunks(k), to_chunks(v)))
    return out.transpose(1, 0, 3, 2, 4).reshape(bsz, seq, heads, dv)


def spatial_gating(u, v, w_s, b_s):
    bsz, seq, groups, dg = v.shape
    n_chunks = seq // SG_CHUNK
    v = _rms(v).astype(u.dtype)
    causal = jnp.tril(jnp.ones((SG_CHUNK, SG_CHUNK), dtype=w_s.dtype))
    vc = v.reshape(bsz, n_chunks, SG_CHUNK, groups, dg)
    z = jnp.einsum('gnm,bkmgd->bkngd', w_s * causal, vc) + b_s.T[None, None, :, :, None]
    return u * z.reshape(bsz, seq, groups, dg)


def shared_router_moe(h, w_router, router_bias, w1, w3, w2):
    bsz, seq, d = h.shape
    xt = h.reshape(-1, d)
    n_tok = xt.shape[0]
    scores = jax.nn.sigmoid(xt.astype(jnp.float32) @ w_router.astype(jnp.float32))
    biased = (scores + router_bias.astype(jnp.float32)).reshape(n_tok, N_EXPERT_GROUPS, EXPERTS_PER_GROUP)
    group_score = lax.top_k(biased, 2)[0].sum(-1)
    g_sel = jnp.argmax(group_score, axis=-1).astype(jnp.int32)
    in_group = jnp.take_along_axis(biased, g_sel[:, None, None], axis=1)[:, 0]
    _, local = lax.top_k(in_group, TOP_K)
    expert_id = g_sel[:, None] * EXPERTS_PER_GROUP + local.astype(jnp.int32)
    gate = jnp.take_along_axis(scores, expert_id, axis=1)
    gate = gate / jnp.sum(gate, axis=-1, keepdims=True)

    n_assign = n_tok * TOP_K
    flat_e = expert_id.reshape(-1)
    flat_tok = jnp.repeat(jnp.arange(n_tok, dtype=jnp.int32), TOP_K)
    flat_gate = gate.reshape(-1)
    order = jnp.argsort(flat_e)
    sorted_e = flat_e[order]
    counts = jnp.bincount(flat_e, length=N_EXPERTS)
    starts = jnp.cumsum(counts) - counts
    padded = (counts + DISPATCH_BLOCK - 1) // DISPATCH_BLOCK * DISPATCH_BLOCK
    pad_ends = jnp.cumsum(padded)
    pad_starts = pad_ends - padded
    dest = pad_starts[sorted_e] + (jnp.arange(n_assign, dtype=jnp.int32) - starts[sorted_e])
    n_blocks = -(-n_assign // DISPATCH_BLOCK) + N_EXPERTS
    n_slots = n_blocks * DISPATCH_BLOCK
    slot_tok = jnp.zeros((n_slots,), jnp.int32).at[dest].set(flat_tok[order])
    slot_gate = jnp.zeros((n_slots,), jnp.float32).at[dest].set(flat_gate[order])
    block_start = jnp.arange(n_blocks, dtype=pad_ends.dtype) * DISPATCH_BLOCK
    block_expert = jnp.minimum(jnp.searchsorted(pad_ends, block_start, side='right'), N_EXPERTS - 1)
    xg = xt[slot_tok].reshape(n_blocks, DISPATCH_BLOCK, d)

    def expert_block(args):
        xb, e = args
        return (jax.nn.silu(xb @ w1[e]) * (xb @ w3[e])) @ w2[e]

    yg = lax.map(expert_block, (xg, block_expert)).reshape(n_slots, d)
    out = jnp.zeros_like(xt).at[slot_tok].add(yg * slot_gate[:, None].astype(yg.dtype))
    return out.reshape(bsz, seq, d)


def setup_inputs(seed: int = 0) -> dict:
    key = jax.random.key(seed)
    ks = jax.random.split(key, 18)
    nrm = jax.random.normal
    f32 = jnp.float32
    x = nrm(ks[0], (BATCH, SEQ, D_MODEL), f32)
    c = nrm(ks[1], (BATCH, D_MODEL), f32)
    positions = jnp.broadcast_to(jnp.arange(SEQ, dtype=jnp.int32)[None, :], (BATCH, SEQ))
    w_ada = nrm(ks[2], (DEPTH, D_MODEL, N_ADA * D_MODEL), f32) * (0.5 * D_MODEL ** -0.5)
    b_ada = 0.02 * nrm(ks[3], (DEPTH, N_ADA * D_MODEL), f32)
    norm1_w = 1.0 + 0.05 * nrm(ks[4], (DEPTH, D_MODEL), f32)
    norm2_w = 1.0 + 0.05 * nrm(ks[5], (DEPTH, D_MODEL), f32)
    w_in = nrm(ks[6], (DEPTH, D_MODEL, IN_COLS), f32) * D_MODEL ** -0.5
    w_out = nrm(ks[7], (DEPTH, MIX_WIDTH, D_MODEL), f32) * MIX_WIDTH ** -0.5
    ret_norm_w = 1.0 + 0.05 * nrm(ks[8], (DEPTH, RET_WIDTH), f32)
    sg_w_s = nrm(ks[9], (DEPTH, SG_GROUPS, SG_CHUNK, SG_CHUNK), f32) * SG_CHUNK ** -0.5
    sg_b_s = 1.0 + 0.1 * nrm(ks[10], (DEPTH, SG_GROUPS, SG_CHUNK), f32)
    w_router = nrm(ks[11], (D_MODEL, N_EXPERTS), f32) * D_MODEL ** -0.5
    router_bias = 0.01 * nrm(ks[12], (N_EXPERTS,), f32)
    w1 = nrm(ks[13], (DEPTH, N_EXPERTS, D_MODEL, D_FF_EXPERT), f32) * D_MODEL ** -0.5
    w3 = nrm(ks[14], (DEPTH, N_EXPERTS, D_MODEL, D_FF_EXPERT), f32) * D_MODEL ** -0.5
    w2 = nrm(ks[15], (DEPTH, N_EXPERTS, D_FF_EXPERT, D_MODEL), f32) * D_FF_EXPERT ** -0.5
    final_norm_w = 1.0 + 0.05 * nrm(ks[16], (D_MODEL,), f32)
    return {"x": x, "c": c, "positions": positions, "w_ada": w_ada, "b_ada": b_ada,
            "norm1_w": norm1_w, "norm2_w": norm2_w, "w_in": w_in, "w_out": w_out,
            "ret_norm_w": ret_norm_w, "sg_w_s": sg_w_s, "sg_b_s": sg_b_s,
            "w_router": w_router, "router_bias": router_bias, "w1": w1, "w3": w3, "w2": w2,
            "final_norm_w": final_norm_w}


def reference(x, c, positions, w_ada, b_ada, norm1_w, norm2_w, w_in, w_out, ret_norm_w,
              sg_w_s, sg_b_s, w_router, router_bias, w1, w3, w2, final_norm_w):
    bsz, seq, _ = x.shape
    c_act = jax.nn.silu(c)
    split_cols = [RET_WIDTH, 2 * RET_WIDTH, 3 * RET_WIDTH, 4 * RET_WIDTH, 4 * RET_WIDTH + SG_WIDTH]
    for layer in range(DEPTH):
        mod = c_act @ w_ada[layer] + b_ada[layer]
        shift1, scale1, gate1, shift2, scale2, gate2 = jnp.split(mod, N_ADA, axis=-1)

        h = rms_norm(x, norm1_w[layer]) * (1 + scale1[:, None]) + shift1[:, None]
        proj = h @ w_in[layer]
        q, k, v, g, u, vs = jnp.split(proj, split_cols, axis=-1)
        q = rotary(q.reshape(bsz, seq, RET_HEADS, RET_HEAD_DIM), positions)
        k = rotary(k.reshape(bsz, seq, RET_HEADS, RET_HEAD_DIM), positions) * RET_HEAD_DIM ** -0.5
        v = v.reshape(bsz, seq, RET_HEADS, RET_HEAD_DIM)
        ret = retention_chunkwise(q, k, v)
        ret = rms_norm(ret, ret_norm_w[layer].reshape(RET_HEADS, RET_HEAD_DIM)).reshape(bsz, seq, RET_WIDTH) * jax.nn.silu(g)
        sg = spatial_gating(jax.nn.gelu(u).reshape(bsz, seq, SG_GROUPS, SG_GROUP_DIM),
                            jax.nn.gelu(vs).reshape(bsz, seq, SG_GROUPS, SG_GROUP_DIM),
                            sg_w_s[layer], sg_b_s[layer]).reshape(bsz, seq, SG_WIDTH)
        mix = jnp.concatenate([ret, sg], axis=-1) @ w_out[layer]
        x = x + gate1[:, None] * mix

        h2 = rms_norm(x, norm2_w[layer]) * (1 + scale2[:, None]) + shift2[:, None]
        x = x + gate2[:, None] * shared_router_moe(h2, w_router, router_bias, w1[layer], w3[layer], w2[layer])
    return rms_norm(x, final_norm_w)
```

```python
import functools

import jax
import jax.numpy as jnp
from jax import lax
from jax.experimental import pallas as pl
from jax.experimental.pallas import tpu as pltpu

F32 = jnp.float32
BF16 = jnp.bfloat16
HIGHEST = lax.Precision.HIGHEST

D = 2048
N_LAYERS = 2
RW = D // 2
NH = 4
HD = RW // NH
CH = 128
ROPE_THETA = 10000.0
SGW = D - RW
NG = 8
GD = SGW // NG
PROJ = 4 * RW + 2 * SGW
NE = 32
NEG = 8
EPG = NE // NEG
FF = 512
N_MOD = 6
NORM_EPS = 1e-6

VMEM_LIMIT = 48 * 1024 * 1024

ADA_TN = 1024
ROPE_TM = 2048
INP_TM = 1024
INP_TN = 512
MIX_ROWS = 512
OUT_TM = 512
EXP_BM = 256
CMB_TM = 256


def _sigmoid(v):
    return 1.0 / (1.0 + jnp.exp(-v))


def _gelu_tanh(v):
    return 0.5 * v * (1.0 + jnp.tanh(0.7978845608028654 * (v + 0.044715 * (v * v * v))))


def _ada_kernel(c_ref, w_ref, b_ref, o_ref):
    c = c_ref[...]
    ca = c * _sigmoid(c)
    o_ref[0] = jnp.dot(ca, w_ref[0], precision=HIGHEST, preferred_element_type=F32) + b_ref[0]


def _ada_call(c_pad, w_ada, b_ada):
    depth, _, ncol = w_ada.shape
    return pl.pallas_call(
        _ada_kernel,
        grid=(depth, ncol // ADA_TN),
        in_specs=[
            pl.BlockSpec((8, D), lambda l, j: (0, 0)),
            pl.BlockSpec((1, D, ADA_TN), lambda l, j: (l, 0, j)),
            pl.BlockSpec((1, 1, ADA_TN), lambda l, j: (l, 0, j)),
        ],
        out_specs=pl.BlockSpec((1, 8, ADA_TN), lambda l, j: (l, 0, j)),
        out_shape=jax.ShapeDtypeStruct((depth, 8, ncol), F32),
        compiler_params=pltpu.CompilerParams(vmem_limit_bytes=VMEM_LIMIT),
        name="ada_mod",
    )(c_pad, w_ada, b_ada.reshape(depth, 1, ncol))


def _rope_kernel(pos_ref, freq_ref, cos_ref, sin_ref):
    ang = pos_ref[...].astype(F32) * freq_ref[...]
    cos_ref[...] = jnp.cos(ang)
    sin_ref[...] = jnp.sin(ang)


def _rope_call(pos_col, inv_freq):
    n = pos_col.shape[0]
    half = inv_freq.shape[1]
    return pl.pallas_call(
        _rope_kernel,
        grid=(n // ROPE_TM,),
        in_specs=[
            pl.BlockSpec((ROPE_TM, 1), lambda i: (i, 0)),
            pl.BlockSpec((1, half), lambda i: (0, 0)),
        ],
        out_specs=[
            pl.BlockSpec((ROPE_TM, half), lambda i: (i, 0)),
            pl.BlockSpec((ROPE_TM, half), lambda i: (i, 0)),
        ],
        out_shape=[jax.ShapeDtypeStruct((n, half), F32)] * 2,
        name="rope_tables",
    )(pos_col, inv_freq)


def _inproj_kernel(x_ref, shift_ref, scale_ref, nw_ref, w_ref, cos_ref, sin_ref, o_ref, h_ref):
    j = pl.program_id(1)

    @pl.when(j == 0)
    def _():
        x = x_ref[...]
        ms = jnp.mean(x * x, axis=-1, keepdims=True)
        h = x * lax.rsqrt(ms + NORM_EPS) * nw_ref[...]
        h = h * (1.0 + scale_ref[0]) + shift_ref[0]
        h_ref[...] = h.astype(BF16)

    acc = jnp.dot(h_ref[...], w_ref[...], preferred_element_type=F32)
    sec = j // (RW // INP_TN)
    half = HD // 2

    def rotary(scale):
        cos = cos_ref[...]
        sin = sin_ref[...]
        for hh in range(INP_TN // HD):
            a = acc[:, hh * HD:hh * HD + half]
            b = acc[:, hh * HD + half:(hh + 1) * HD]
            o_ref[:, hh * HD:hh * HD + half] = ((a * cos - b * sin) * scale).astype(BF16)
            o_ref[:, hh * HD + half:(hh + 1) * HD] = ((b * cos + a * sin) * scale).astype(BF16)

    @pl.when(sec == 0)
    def _():
        rotary(1.0)

    @pl.when(sec == 1)
    def _():
        rotary(HD ** -0.5)

    @pl.when(sec == 2)
    def _():
        o_ref[...] = acc.astype(BF16)

    @pl.when(sec == 3)
    def _():
        o_ref[...] = (acc * _sigmoid(acc)).astype(BF16)

    @pl.when(sec == 4)
    def _():
        o_ref[...] = _gelu_tanh(acc).astype(BF16)

    @pl.when(sec == 5)
    def _():
        for gg in range(INP_TN // GD):
            t = _gelu_tanh(acc[:, gg * GD:(gg + 1) * GD])
            ms = jnp.mean(t * t, axis=-1, keepdims=True)
            o_ref[:, gg * GD:(gg + 1) * GD] = (t * lax.rsqrt(ms + NORM_EPS)).astype(BF16)


def _inproj_call(x2, shift, scale, nw, w_bf, cos, sin, seq):
    n = x2.shape[0]
    tiles_per_batch = seq // INP_TM
    return pl.pallas_call(
        _inproj_kernel,
        grid=(n // INP_TM, PROJ // INP_TN),
        in_specs=[
            pl.BlockSpec((INP_TM, D), lambda i, j: (i, 0)),
            pl.BlockSpec((1, 1, D), lambda i, j: (i // tiles_per_batch, 0, 0)),
            pl.BlockSpec((1, 1, D), lambda i, j: (i // tiles_per_batch, 0, 0)),
            pl.BlockSpec((1, D), lambda i, j: (0, 0)),
            pl.BlockSpec((D, INP_TN), lambda i, j: (0, j)),
            pl.BlockSpec((INP_TM, HD // 2), lambda i, j: (i, 0)),
            pl.BlockSpec((INP_TM, HD // 2), lambda i, j: (i, 0)),
        ],
        out_specs=pl.BlockSpec((INP_TM, INP_TN), lambda i, j: (i, j)),
        out_shape=jax.ShapeDtypeStruct((n, PROJ), BF16),
        scratch_shapes=[pltpu.VMEM((INP_TM, D), BF16)],
        compiler_params=pltpu.CompilerParams(
            dimension_semantics=("arbitrary", "arbitrary"), vmem_limit_bytes=VMEM_LIMIT),
        name="in_proj",
    )(x2, shift, scale, nw, w_bf, cos, sin)


def _mix_kernel(cd_ref, q_ref, k_ref, v_ref, g_ref, u_ref, vs_ref, dm_ref, qd_ref, kd_ref,
                rnw_ref, ws_ref, bs_ref, o_ref, state_ref):
    @pl.when(pl.program_id(1) == 0)
    def _():
        state_ref[...] = jnp.zeros_like(state_ref)

    row = lax.broadcasted_iota(jnp.int32, (CH, CH), 0)
    col = lax.broadcasted_iota(jnp.int32, (CH, CH), 1)
    causal = row >= col

    def chunk(c, carry):
        r0 = pl.multiple_of(c * CH, CH)
        rows = pl.ds(r0, CH)
        for h in range(NH):
            cols = slice(h * HD, (h + 1) * HD)
            q = q_ref[rows, cols]
            k = k_ref[rows, cols]
            v = v_ref[rows, cols]
            st = state_ref[h]
            s = lax.dot_general(q, k, (((1,), (1,)), ((), ())), preferred_element_type=F32)
            s = s * dm_ref[h]
            intra = jnp.dot(s.astype(BF16), v, preferred_element_type=F32)
            cross = jnp.dot(q, st.astype(BF16), preferred_element_type=F32) * qd_ref[:, h:h + 1]
            kdec = (k.astype(F32) * kd_ref[:, h:h + 1]).astype(BF16)
            upd = lax.dot_general(kdec, v, (((0,), (0,)), ((), ())), preferred_element_type=F32)
            state_ref[h] = st * cd_ref[h] + upd
            o = intra + cross
            ms = jnp.mean(o * o, axis=-1, keepdims=True)
            o = o * lax.rsqrt(ms + NORM_EPS) * rnw_ref[:, cols]
            o_ref[rows, cols] = (o * g_ref[rows, cols].astype(F32)).astype(BF16)
        for g in range(NG):
            cols = slice(g * GD, (g + 1) * GD)
            wm = jnp.where(causal, ws_ref[g], 0.0).astype(BF16)
            z = jnp.dot(wm, vs_ref[rows, cols], preferred_element_type=F32) + bs_ref[:, g:g + 1]
            o_ref[rows, RW + g * GD:RW + (g + 1) * GD] = (u_ref[rows, cols].astype(F32) * z).astype(BF16)
        return carry

    lax.fori_loop(0, MIX_ROWS // CH, chunk, 0)


def _mix_call(proj, chunk_decay, decay_mask, q_decay_t, k_decay_t, rnw, ws, bs_t, bsz, seq):
    n = proj.shape[0]
    steps = seq // MIX_ROWS

    def colblk(jb):
        return pl.BlockSpec((MIX_ROWS, RW), lambda b, s, cd: (b * steps + s, jb))

    const2 = lambda b, s, cd: (0, 0)
    const3 = lambda b, s, cd: (0, 0, 0)
    grid_spec = pltpu.PrefetchScalarGridSpec(
        num_scalar_prefetch=1,
        grid=(bsz, steps),
        in_specs=[colblk(0), colblk(1), colblk(2), colblk(3), colblk(4), colblk(5),
                  pl.BlockSpec((NH, CH, CH), const3),
                  pl.BlockSpec((CH, NH), const2),
                  pl.BlockSpec((CH, NH), const2),
                  pl.BlockSpec((1, RW), const2),
                  pl.BlockSpec((NG, CH, CH), const3),
                  pl.BlockSpec((CH, NG), const2)],
        out_specs=pl.BlockSpec((MIX_ROWS, D), lambda b, s, cd: (b * steps + s, 0)),
        scratch_shapes=[pltpu.VMEM((NH, HD, HD), F32)],
    )
    return pl.pallas_call(
        _mix_kernel,
        grid_spec=grid_spec,
        out_shape=jax.ShapeDtypeStruct((n, D), BF16),
        compiler_params=pltpu.CompilerParams(
            dimension_semantics=("arbitrary", "arbitrary"), vmem_limit_bytes=VMEM_LIMIT),
        name="retention_spatial_mix",
    )(chunk_decay, proj, proj, proj, proj, proj, proj, decay_mask, q_decay_t, k_decay_t, rnw, ws, bs_t)


def _outproj_kernel(mix_ref, x_ref, w_ref, gate_ref, shift_ref, scale_ref, nw_ref, wr_ref, rb_ref,
                    x1_ref, h2_ref, eid_ref, gw_ref, rank_ref, cnt_ref, carry_ref):
    i = pl.program_id(0)

    @pl.when(i == 0)
    def _():
        carry_ref[...] = jnp.zeros_like(carry_ref)

    y = jnp.dot(mix_ref[...], w_ref[...], preferred_element_type=F32)
    x1 = x_ref[...] + gate_ref[0] * y
    x1_ref[...] = x1
    ms = jnp.mean(x1 * x1, axis=-1, keepdims=True)
    h2 = x1 * lax.rsqrt(ms + NORM_EPS) * nw_ref[...]
    h2 = h2 * (1.0 + scale_ref[0]) + shift_ref[0]
    h2_ref[...] = h2

    logits = lax.dot_general(wr_ref[...], h2, (((1,), (1,)), ((), ())),
                             precision=HIGHEST, preferred_element_type=F32)
    scores = _sigmoid(logits)
    biased = scores + rb_ref[...]
    a = [biased[m * NEG:(m + 1) * NEG] for m in range(EPG)]
    sc = [scores[m * NEG:(m + 1) * NEG] for m in range(EPG)]
    p, q = jnp.maximum(a[0], a[1]), jnp.minimum(a[0], a[1])
    r, s = jnp.maximum(a[2], a[3]), jnp.minimum(a[2], a[3])
    group_score = jnp.maximum(p, r) + jnp.maximum(jnp.minimum(p, r), jnp.maximum(q, s))
    gidx = lax.broadcasted_iota(jnp.int32, group_score.shape, 0)
    gmax = jnp.max(group_score, axis=0, keepdims=True)
    g_sel = jnp.min(jnp.where(group_score == gmax, gidx, NEG), axis=0, keepdims=True)
    pick = gidx == g_sel
    v = [jnp.sum(jnp.where(pick, a[m], 0.0), axis=0, keepdims=True) for m in range(EPG)]
    u = [jnp.sum(jnp.where(pick, sc[m], 0.0), axis=0, keepdims=True) for m in range(EPG)]

    def first_argmax(vals):
        m = functools.reduce(jnp.maximum, vals)
        idx = jnp.full(m.shape, EPG - 1, jnp.int32)
        for t in range(EPG - 2, -1, -1):
            idx = jnp.where(vals[t] == m, t, idx)
        return idx

    def select(vals, idx):
        out = vals[EPG - 1]
        for t in range(EPG - 2, -1, -1):
            out = jnp.where(idx == t, vals[t], out)
        return out

    i1 = first_argmax(v)
    v2 = [jnp.where(i1 == t, -jnp.inf, v[t]) for t in range(EPG)]
    i2 = first_argmax(v2)
    s1 = select(u, i1)
    s2 = select(u, i2)
    denom = s1 + s2
    e1 = g_sel * EPG + i1
    e2 = g_sel * EPG + i2
    eid_ref[0:1, :] = e1
    eid_ref[1:2, :] = e2
    gw_ref[0:1, :] = s1 / denom
    gw_ref[1:2, :] = s2 / denom

    tm = e1.shape[1]
    eidx = lax.broadcasted_iota(jnp.int32, (NE, tm), 0)
    oh1 = eidx == e1
    oh2 = eidx == e2
    oh = jnp.where(oh1 | oh2, 1.0, 0.0)
    tr = lax.broadcasted_iota(jnp.int32, (tm, tm), 0)
    tc = lax.broadcasted_iota(jnp.int32, (tm, tm), 1)
    before = jnp.where(tr < tc, 1.0, 0.0).astype(BF16)
    prefix = jnp.dot(oh.astype(BF16), before, preferred_element_type=F32) + carry_ref[...]
    rank_ref[0:1, :] = jnp.sum(jnp.where(oh1, prefix, 0.0), axis=0, keepdims=True).astype(jnp.int32)
    rank_ref[1:2, :] = jnp.sum(jnp.where(oh2, prefix, 0.0), axis=0, keepdims=True).astype(jnp.int32)
    total = carry_ref[...] + jnp.sum(oh, axis=1, keepdims=True)
    carry_ref[...] = total
    cnt_ref[...] = total.astype(jnp.int32)


def _outproj_call(mix, x2, w_bf, gate, shift, scale, nw, wr_t, rb_col, seq):
    n = x2.shape[0]
    tiles_per_batch = seq // OUT_TM
    bmap = lambda i: (i // tiles_per_batch, 0, 0)
    row = lambda i: (i, 0)
    lane = lambda i: (0, i)
    const = lambda i: (0, 0)
    return pl.pallas_call(
        _outproj_kernel,
        grid=(n // OUT_TM,),
        in_specs=[
            pl.BlockSpec((OUT_TM, D), row),
            pl.BlockSpec((OUT_TM, D), row),
            pl.BlockSpec((D, D), const),
            pl.BlockSpec((1, 1, D), bmap),
            pl.BlockSpec((1, 1, D), bmap),
            pl.BlockSpec((1, 1, D), bmap),
            pl.BlockSpec((1, D), const),
            pl.BlockSpec((NE, D), const),
            pl.BlockSpec((NE, 1), const),
        ],
        out_specs=[
            pl.BlockSpec((OUT_TM, D), row),
            pl.BlockSpec((OUT_TM, D), row),
            pl.BlockSpec((2, OUT_TM), lane),
            pl.BlockSpec((2, OUT_TM), lane),
            pl.BlockSpec((2, OUT_TM), lane),
            pl.BlockSpec((NE, 1), const),
        ],
        out_shape=[
            jax.ShapeDtypeStruct((n, D), F32),
            jax.ShapeDtypeStruct((n, D), F32),
            jax.ShapeDtypeStruct((2, n), jnp.int32),
            jax.ShapeDtypeStruct((2, n), F32),
            jax.ShapeDtypeStruct((2, n), jnp.int32),
            jax.ShapeDtypeStruct((NE, 1), jnp.int32),
        ],
        scratch_shapes=[pltpu.VMEM((NE, 1), F32)],
        compiler_params=pltpu.CompilerParams(
            dimension_semantics=("arbitrary",), vmem_limit_bytes=VMEM_LIMIT),
        name="out_proj_router",
    )(mix, x2, w_bf, gate, shift, scale, nw, wr_t, rb_col)


def _row_copy(src_hbm, src_row, dst_vmem, dst_row, sem):
    return pltpu.make_async_copy(src_hbm.at[pl.ds(src_row, 1), :], dst_vmem.at[pl.ds(dst_row, 1), :], sem)


def _expert_kernel(be_ref, na_ref, tok_ref, h_hbm, w1_ref, w3_ref, w2_ref, o_ref, xbuf, sem):
    i = pl.program_id(0)
    active = i < na_ref[0]

    @pl.when(active)
    def _():
        def issue(r, carry):
            _row_copy(h_hbm, tok_ref[0, 0, r], xbuf, r, sem).start()
            return carry

        lax.fori_loop(0, EXP_BM, issue, 0)

        def drain(r, carry):
            _row_copy(h_hbm, 0, xbuf, r, sem).wait()
            return carry

        lax.fori_loop(0, EXP_BM, drain, 0)
        xb = xbuf[...].astype(BF16)
        h1 = jnp.dot(xb, w1_ref[0], preferred_element_type=F32)
        h3 = jnp.dot(xb, w3_ref[0], preferred_element_type=F32)
        act = (h1 * _sigmoid(h1) * h3).astype(BF16)
        o_ref[...] = jnp.dot(act, w2_ref[0], preferred_element_type=F32)

    @pl.when(jnp.logical_not(active))
    def _():
        o_ref[...] = jnp.zeros_like(o_ref)


def _expert_call(block_expert, n_active, slot_tok, h2, w1_bf, w3_bf, w2_bf):
    nb = block_expert.shape[0]
    grid_spec = pltpu.PrefetchScalarGridSpec(
        num_scalar_prefetch=2,
        grid=(nb,),
        in_specs=[
            pl.BlockSpec((1, 1, EXP_BM), lambda i, be, na: (i, 0, 0), memory_space=pltpu.SMEM),
            pl.BlockSpec(memory_space=pl.ANY),
            pl.BlockSpec((1, D, FF), lambda i, be, na: (be[i], 0, 0)),
            pl.BlockSpec((1, D, FF), lambda i, be, na: (be[i], 0, 0)),
            pl.BlockSpec((1, FF, D), lambda i, be, na: (be[i], 0, 0)),
        ],
        out_specs=pl.BlockSpec((EXP_BM, D), lambda i, be, na: (i, 0)),
        scratch_shapes=[pltpu.VMEM((EXP_BM, D), F32), pltpu.SemaphoreType.DMA(())],
    )
    return pl.pallas_call(
        _expert_kernel,
        grid_spec=grid_spec,
        out_shape=jax.ShapeDtypeStruct((nb * EXP_BM, D), F32),
        compiler_params=pltpu.CompilerParams(
            dimension_semantics=("arbitrary",), vmem_limit_bytes=VMEM_LIMIT),
        name="expert_blocks",
    )(block_expert, n_active, slot_tok.reshape(nb, 1, EXP_BM), h2, w1_bf, w3_bf, w2_bf)


def _combine_kernel(dest_ref, x_ref, gate_ref, gw_ref, fw_ref, y_hbm, o_ref, ybuf, sem, *, final_norm):
    def issue(r, carry):
        _row_copy(y_hbm, dest_ref[0, 0, r], ybuf.at[0], r, sem).start()
        _row_copy(y_hbm, dest_ref[0, 1, r], ybuf.at[1], r, sem).start()
        return carry

    lax.fori_loop(0, CMB_TM, issue, 0)

    def drain(r, carry):
        _row_copy(y_hbm, 0, ybuf.at[0], r, sem).wait()
        _row_copy(y_hbm, 0, ybuf.at[1], r, sem).wait()
        return carry

    lax.fori_loop(0, CMB_TM, drain, 0)
    gw = gw_ref[...]
    moe = gw[:, 0:1] * ybuf[0] + gw[:, 1:2] * ybuf[1]
    x2 = x_ref[...] + gate_ref[0] * moe
    if final_norm:
        ms = jnp.mean(x2 * x2, axis=-1, keepdims=True)
        x2 = x2 * lax.rsqrt(ms + NORM_EPS) * fw_ref[...]
    o_ref[...] = x2


def _combine_call(dest_blk, x1, gate, gw_col, fw, yg, seq, final_norm):
    n = x1.shape[0]
    tiles_per_batch = seq // CMB_TM
    return pl.pallas_call(
        functools.partial(_combine_kernel, final_norm=final_norm),
        grid=(n // CMB_TM,),
        in_specs=[
            pl.BlockSpec((1, 2, CMB_TM), lambda i: (i, 0, 0), memory_space=pltpu.SMEM),
            pl.BlockSpec((CMB_TM, D), lambda i: (i, 0)),
            pl.BlockSpec((1, 1, D), lambda i: (i // tiles_per_batch, 0, 0)),
            pl.BlockSpec((CMB_TM, 2), lambda i: (i, 0)),
            pl.BlockSpec((1, D), lambda i: (0, 0)),
            pl.BlockSpec(memory_space=pl.ANY),
        ],
        out_specs=pl.BlockSpec((CMB_TM, D), lambda i: (i, 0)),
        out_shape=jax.ShapeDtypeStruct((n, D), F32),
        scratch_shapes=[pltpu.VMEM((2, CMB_TM, D), F32), pltpu.SemaphoreType.DMA(())],
        compiler_params=pltpu.CompilerParams(
            dimension_semantics=("arbitrary",), vmem_limit_bytes=VMEM_LIMIT),
        name="moe_combine",
    )(dest_blk, x1, gate, gw_col, fw, yg)


def _decay_tables():
    log_gamma = jnp.log1p(-jnp.exp2(-5.0 - jnp.arange(NH, dtype=F32)))
    pos = jnp.arange(CH, dtype=F32)
    diff = pos[:, None] - pos[None, :]
    mask = jnp.where(diff >= 0, jnp.exp(log_gamma[:, None, None] * jnp.maximum(diff, 0.0)), 0.0)
    q_decay = jnp.exp(log_gamma[:, None] * (pos + 1.0))
    k_decay = jnp.exp(log_gamma[:, None] * (CH - 1.0 - pos))
    chunk_decay = jnp.exp(log_gamma * CH)
    return mask.astype(F32), q_decay.T, k_decay.T, chunk_decay


def kernel(x, c, positions, w_ada, b_ada, norm1_w, norm2_w, w_in, w_out, ret_norm_w, sg_w_s, sg_b_s,
           w_router, router_bias, w1, w3, w2, final_norm_w):
    bsz, seq, d = x.shape
    n = bsz * seq
    assert d == D and seq % INP_TM == 0 and seq % MIX_ROWS == 0 and n % CMB_TM == 0

    c_pad = jnp.zeros((8, D), F32).at[:bsz].set(c)
    mod = _ada_call(c_pad, w_ada, b_ada)[:, :bsz]
    mod = mod.reshape(N_LAYERS, bsz, N_MOD, 1, D)

    half = HD // 2
    inv_freq = (ROPE_THETA ** (-jnp.arange(half, dtype=F32) / half)).reshape(1, half)
    cos, sin = _rope_call(positions.reshape(n, 1), inv_freq)
    decay_mask, q_decay_t, k_decay_t, chunk_decay = _decay_tables()

    perm = (jnp.arange(NE) % NEG) * EPG + jnp.arange(NE) // NEG
    wr_t = w_router.T[perm]
    rb_col = router_bias[perm].reshape(NE, 1)

    nb = n * 2 // EXP_BM + NE
    n_slots = nb * EXP_BM
    xs = x.reshape(n, D)
    for l in range(N_LAYERS):
        shift1, scale1, gate1, shift2, scale2, gate2 = [mod[l, :, t] for t in range(N_MOD)]
        proj = _inproj_call(xs, shift1, scale1, norm1_w[l].reshape(1, D), w_in[l].astype(BF16), cos, sin, seq)
        mix = _mix_call(proj, chunk_decay, decay_mask, q_decay_t, k_decay_t, ret_norm_w[l].reshape(1, RW),
                        sg_w_s[l], sg_b_s[l].T, bsz, seq)
        x1, h2, eid, gw, rank, counts = _outproj_call(
            mix, xs, w_out[l].astype(BF16), gate1, shift2, scale2, norm2_w[l].reshape(1, D), wr_t, rb_col, seq)

        counts = counts.reshape(NE)
        padded = (counts + EXP_BM - 1) // EXP_BM * EXP_BM
        pad_ends = jnp.cumsum(padded)
        pad_starts = pad_ends - padded
        starts = jnp.cumsum(counts) - counts
        dest = pad_starts[eid] + rank
        flat_e = eid.T.reshape(-1)
        order = jnp.argsort(flat_e).astype(jnp.int32)
        slot = jnp.arange(n_slots, dtype=jnp.int32)
        slot_e = jnp.minimum(jnp.searchsorted(pad_ends, slot, side='right'), NE - 1).astype(jnp.int32)
        off = slot - pad_starts[slot_e]
        valid = off < counts[slot_e]
        src = jnp.clip(starts[slot_e] + off, 0, 2 * n - 1)
        slot_tok = jnp.where(valid, order[src] // 2, 0).astype(jnp.int32)
        block_expert = slot_e[::EXP_BM]
        n_active = (pad_ends[-1] // EXP_BM).astype(jnp.int32).reshape(1)

        yg = _expert_call(block_expert, n_active, slot_tok, h2,
                          w1[l].astype(BF16), w3[l].astype(BF16), w2[l].astype(BF16))
        dest_blk = dest.reshape(2, n // CMB_TM, CMB_TM).transpose(1, 0, 2)
        xs = _combine_call(dest_blk, x1, gate2, gw.T, final_norm_w.reshape(1, D), yg, seq,
                           final_norm=(l == N_LAYERS - 1))
    return xs.reshape(bsz, seq, D)
```

```python
import functools

import jax
import jax.numpy as jnp
from jax import lax
from jax.experimental import pallas as pl
from jax.experimental.pallas import tpu as pltpu

F32 = jnp.float32
BF16 = jnp.bfloat16
HIGHEST = lax.Precision.HIGHEST

D = 2048
N_LAYERS = 2
RW = D // 2
NH = 4
HD = RW // NH
CH = 128
ROPE_THETA = 10000.0
SGW = D - RW
NG = 8
GD = SGW // NG
PROJ = 4 * RW + 2 * SGW
NE = 32
NEG = 8
EPG = NE // NEG
FF = 512
N_MOD = 6
NORM_EPS = 1e-6

VMEM_LIMIT = 48 * 1024 * 1024

ADA_TN = 1024
ROPE_TM = 2048
INP_TM = 1024
INP_TN = 512
MIX_ROWS = 512
OUT_TM = 512
EXP_BM = 256
DSP_TM = OUT_TM
CMB_TM = 256

LANES = 128
PK_SUB = D // 2 // LANES


def _sigmoid(v):
    return 1.0 / (1.0 + jnp.exp(-v))


def _gelu_tanh(v):
    return 0.5 * v * (1.0 + jnp.tanh(0.7978845608028654 * (v + 0.044715 * (v * v * v))))


def _ada_kernel(c_ref, w_ref, b_ref, o_ref):
    c = c_ref[...]
    ca = c * _sigmoid(c)
    o_ref[0] = jnp.dot(ca, w_ref[0], precision=HIGHEST, preferred_element_type=F32) + b_ref[0]


def _ada_call(c_pad, w_ada, b_ada):
    depth, _, ncol = w_ada.shape
    return pl.pallas_call(
        _ada_kernel,
        grid=(depth, ncol // ADA_TN),
        in_specs=[
            pl.BlockSpec((8, D), lambda l, j: (0, 0)),
            pl.BlockSpec((1, D, ADA_TN), lambda l, j: (l, 0, j)),
            pl.BlockSpec((1, 1, ADA_TN), lambda l, j: (l, 0, j)),
        ],
        out_specs=pl.BlockSpec((1, 8, ADA_TN), lambda l, j: (l, 0, j)),
        out_shape=jax.ShapeDtypeStruct((depth, 8, ncol), F32),
        compiler_params=pltpu.CompilerParams(vmem_limit_bytes=VMEM_LIMIT),
        name="ada_mod",
    )(c_pad, w_ada, b_ada.reshape(depth, 1, ncol))


def _rope_kernel(pos_ref, freq_ref, cos_ref, sin_ref):
    ang = pos_ref[...].astype(F32) * freq_ref[...]
    cos_ref[...] = jnp.cos(ang)
    sin_ref[...] = jnp.sin(ang)


def _rope_call(pos_col, inv_freq):
    n = pos_col.shape[0]
    half = inv_freq.shape[1]
    return pl.pallas_call(
        _rope_kernel,
        grid=(n // ROPE_TM,),
        in_specs=[
            pl.BlockSpec((ROPE_TM, 1), lambda i: (i, 0)),
            pl.BlockSpec((1, half), lambda i: (0, 0)),
        ],
        out_specs=[
            pl.BlockSpec((ROPE_TM, half), lambda i: (i, 0)),
            pl.BlockSpec((ROPE_TM, half), lambda i: (i, 0)),
        ],
        out_shape=[jax.ShapeDtypeStruct((n, half), F32)] * 2,
        name="rope_tables",
    )(pos_col, inv_freq)


def _inproj_kernel(x_ref, shift_ref, scale_ref, nw_ref, w_ref, cos_ref, sin_ref, o_ref, h_ref):
    j = pl.program_id(1)

    @pl.when(j == 0)
    def _():
        x = x_ref[...]
        ms = jnp.mean(x * x, axis=-1, keepdims=True)
        h = x * lax.rsqrt(ms + NORM_EPS) * nw_ref[...]
        h = h * (1.0 + scale_ref[0]) + shift_ref[0]
        h_ref[...] = h.astype(BF16)

    acc = jnp.dot(h_ref[...], w_ref[...], preferred_element_type=F32)
    sec = j // (RW // INP_TN)
    half = HD // 2

    def rotary(scale):
        cos = cos_ref[...]
        sin = sin_ref[...]
        for hh in range(INP_TN // HD):
            a = acc[:, hh * HD:hh * HD + half]
            b = acc[:, hh * HD + half:(hh + 1) * HD]
            o_ref[:, hh * HD:hh * HD + half] = ((a * cos - b * sin) * scale).astype(BF16)
            o_ref[:, hh * HD + half:(hh + 1) * HD] = ((b * cos + a * sin) * scale).astype(BF16)

    @pl.when(sec == 0)
    def _():
        rotary(1.0)

    @pl.when(sec == 1)
    def _():
        rotary(HD ** -0.5)

    @pl.when(sec == 2)
    def _():
        o_ref[...] = acc.astype(BF16)

    @pl.when(sec == 3)
    def _():
        o_ref[...] = (acc * _sigmoid(acc)).astype(BF16)

    @pl.when(sec == 4)
    def _():
        o_ref[...] = _gelu_tanh(acc).astype(BF16)

    @pl.when(sec == 5)
    def _():
        for gg in range(INP_TN // GD):
            t = _gelu_tanh(acc[:, gg * GD:(gg + 1) * GD])
            ms = jnp.mean(t * t, axis=-1, keepdims=True)
            o_ref[:, gg * GD:(gg + 1) * GD] = (t * lax.rsqrt(ms + NORM_EPS)).astype(BF16)


def _inproj_call(x2, shift, scale, nw, w_bf, cos, sin, seq):
    n = x2.shape[0]
    tiles_per_batch = seq // INP_TM
    return pl.pallas_call(
        _inproj_kernel,
        grid=(n // INP_TM, PROJ // INP_TN),
        in_specs=[
            pl.BlockSpec((INP_TM, D), lambda i, j: (i, 0)),
            pl.BlockSpec((1, 1, D), lambda i, j: (i // tiles_per_batch, 0, 0)),
            pl.BlockSpec((1, 1, D), lambda i, j: (i // tiles_per_batch, 0, 0)),
            pl.BlockSpec((1, D), lambda i, j: (0, 0)),
            pl.BlockSpec((D, INP_TN), lambda i, j: (0, j)),
            pl.BlockSpec((INP_TM, HD // 2), lambda i, j: (i, 0)),
            pl.BlockSpec((INP_TM, HD // 2), lambda i, j: (i, 0)),
        ],
        out_specs=pl.BlockSpec((INP_TM, INP_TN), lambda i, j: (i, j)),
        out_shape=jax.ShapeDtypeStruct((n, PROJ), BF16),
        scratch_shapes=[pltpu.VMEM((INP_TM, D), BF16)],
        compiler_params=pltpu.CompilerParams(
            dimension_semantics=("arbitrary", "arbitrary"), vmem_limit_bytes=VMEM_LIMIT),
        name="in_proj",
    )(x2, shift, scale, nw, w_bf, cos, sin)


def _mix_kernel(cd_ref, q_ref, k_ref, v_ref, g_ref, u_ref, vs_ref, dm_ref, qd_ref, kd_ref,
                rnw_ref, ws_ref, bs_ref, o_ref, state_ref):
    @pl.when(pl.program_id(1) == 0)
    def _():
        state_ref[...] = jnp.zeros_like(state_ref)

    row = lax.broadcasted_iota(jnp.int32, (CH, CH), 0)
    col = lax.broadcasted_iota(jnp.int32, (CH, CH), 1)
    causal = row >= col

    def chunk(c, carry):
        r0 = pl.multiple_of(c * CH, CH)
        rows = pl.ds(r0, CH)
        for h in range(NH):
            cols = slice(h * HD, (h + 1) * HD)
            q = q_ref[rows, cols]
            k = k_ref[rows, cols]
            v = v_ref[rows, cols]
            st = state_ref[h]
            s = lax.dot_general(q, k, (((1,), (1,)), ((), ())), preferred_element_type=F32)
            s = s * dm_ref[h]
            intra = jnp.dot(s.astype(BF16), v, preferred_element_type=F32)
            cross = jnp.dot(q, st.astype(BF16), preferred_element_type=F32) * qd_ref[:, h:h + 1]
            kdec = (k.astype(F32) * kd_ref[:, h:h + 1]).astype(BF16)
            upd = lax.dot_general(kdec, v, (((0,), (0,)), ((), ())), preferred_element_type=F32)
            state_ref[h] = st * cd_ref[h] + upd
            o = intra + cross
            ms = jnp.mean(o * o, axis=-1, keepdims=True)
            o = o * lax.rsqrt(ms + NORM_EPS) * rnw_ref[:, cols]
            o_ref[rows, cols] = (o * g_ref[rows, cols].astype(F32)).astype(BF16)
        for g in range(NG):
            cols = slice(g * GD, (g + 1) * GD)
            wm = jnp.where(causal, ws_ref[g], 0.0).astype(BF16)
            z = jnp.dot(wm, vs_ref[rows, cols], preferred_element_type=F32) + bs_ref[:, g:g + 1]
            o_ref[rows, RW + g * GD:RW + (g + 1) * GD] = (u_ref[rows, cols].astype(F32) * z).astype(BF16)
        return carry

    lax.fori_loop(0, MIX_ROWS // CH, chunk, 0)


def _mix_call(proj, chunk_decay, decay_mask, q_decay_t, k_decay_t, rnw, ws, bs_t, bsz, seq):
    n = proj.shape[0]
    steps = seq // MIX_ROWS

    def colblk(jb):
        return pl.BlockSpec((MIX_ROWS, RW), lambda b, s, cd: (b * steps + s, jb))

    const2 = lambda b, s, cd: (0, 0)
    const3 = lambda b, s, cd: (0, 0, 0)
    grid_spec = pltpu.PrefetchScalarGridSpec(
        num_scalar_prefetch=1,
        grid=(bsz, steps),
        in_specs=[colblk(0), colblk(1), colblk(2), colblk(3), colblk(4), colblk(5),
                  pl.BlockSpec((NH, CH, CH), const3),
                  pl.BlockSpec((CH, NH), const2),
                  pl.BlockSpec((CH, NH), const2),
                  pl.BlockSpec((1, RW), const2),
                  pl.BlockSpec((NG, CH, CH), const3),
                  pl.BlockSpec((CH, NG), const2)],
        out_specs=pl.BlockSpec((MIX_ROWS, D), lambda b, s, cd: (b * steps + s, 0)),
        scratch_shapes=[pltpu.VMEM((NH, HD, HD), F32)],
    )
    return pl.pallas_call(
        _mix_kernel,
        grid_spec=grid_spec,
        out_shape=jax.ShapeDtypeStruct((n, D), BF16),
        compiler_params=pltpu.CompilerParams(
            dimension_semantics=("arbitrary", "arbitrary"), vmem_limit_bytes=VMEM_LIMIT),
        name="retention_spatial_mix",
    )(chunk_decay, proj, proj, proj, proj, proj, proj, decay_mask, q_decay_t, k_decay_t, rnw, ws, bs_t)


def _outproj_kernel(mix_ref, x_ref, w_ref, gate_ref, shift_ref, scale_ref, nw_ref, wr_ref, rb_ref,
                    x1_ref, h2_ref, eid_ref, gw_ref, rank_ref, cnt_ref, carry_ref):
    i = pl.program_id(0)

    @pl.when(i == 0)
    def _():
        carry_ref[...] = jnp.zeros_like(carry_ref)

    y = jnp.dot(mix_ref[...], w_ref[...], preferred_element_type=F32)
    x1 = x_ref[...] + gate_ref[0] * y
    x1_ref[...] = x1
    ms = jnp.mean(x1 * x1, axis=-1, keepdims=True)
    h2 = x1 * lax.rsqrt(ms + NORM_EPS) * nw_ref[...]
    h2 = h2 * (1.0 + scale_ref[0]) + shift_ref[0]
    hi = lax.bitcast_convert_type(h2[:, :D // 2].astype(BF16).astype(F32), jnp.uint32)
    lo = lax.bitcast_convert_type(h2[:, D // 2:].astype(BF16).astype(F32), jnp.uint32)
    packed = hi | (lo >> 16)
    for j in range(PK_SUB):
        h2_ref[pl.ds(j, OUT_TM, stride=PK_SUB), :] = packed[:, j * LANES:(j + 1) * LANES]

    logits = lax.dot_general(wr_ref[...], h2, (((1,), (1,)), ((), ())),
                             precision=HIGHEST, preferred_element_type=F32)
    scores = _sigmoid(logits)
    biased = scores + rb_ref[...]
    a = [biased[m * NEG:(m + 1) * NEG] for m in range(EPG)]
    sc = [scores[m * NEG:(m + 1) * NEG] for m in range(EPG)]
    p, q = jnp.maximum(a[0], a[1]), jnp.minimum(a[0], a[1])
    r, s = jnp.maximum(a[2], a[3]), jnp.minimum(a[2], a[3])
    group_score = jnp.maximum(p, r) + jnp.maximum(jnp.minimum(p, r), jnp.maximum(q, s))
    gidx = lax.broadcasted_iota(jnp.int32, group_score.shape, 0)
    gmax = jnp.max(group_score, axis=0, keepdims=True)
    g_sel = jnp.min(jnp.where(group_score == gmax, gidx, NEG), axis=0, keepdims=True)
    pick = gidx == g_sel
    v = [jnp.sum(jnp.where(pick, a[m], 0.0), axis=0, keepdims=True) for m in range(EPG)]
    u = [jnp.sum(jnp.where(pick, sc[m], 0.0), axis=0, keepdims=True) for m in range(EPG)]

    def first_argmax(vals):
        m = functools.reduce(jnp.maximum, vals)
        idx = jnp.full(m.shape, EPG - 1, jnp.int32)
        for t in range(EPG - 2, -1, -1):
            idx = jnp.where(vals[t] == m, t, idx)
        return idx

    def select(vals, idx):
        out = vals[EPG - 1]
        for t in range(EPG - 2, -1, -1):
            out = jnp.where(idx == t, vals[t], out)
        return out

    i1 = first_argmax(v)
    v2 = [jnp.where(i1 == t, -jnp.inf, v[t]) for t in range(EPG)]
    i2 = first_argmax(v2)
    s1 = select(u, i1)
    s2 = select(u, i2)
    denom = s1 + s2
    e1 = g_sel * EPG + i1
    e2 = g_sel * EPG + i2
    eid_ref[0, 0:1, :] = e1
    eid_ref[0, 1:2, :] = e2
    gw_ref[0:1, :] = s1 / denom
    gw_ref[1:2, :] = s2 / denom

    tm = e1.shape[1]
    eidx = lax.broadcasted_iota(jnp.int32, (NE, tm), 0)
    oh1 = eidx == e1
    oh2 = eidx == e2
    oh = jnp.where(oh1 | oh2, 1.0, 0.0)
    tr = lax.broadcasted_iota(jnp.int32, (tm, tm), 0)
    tc = lax.broadcasted_iota(jnp.int32, (tm, tm), 1)
    before = jnp.where(tr < tc, 1.0, 0.0).astype(BF16)
    prefix = jnp.dot(oh.astype(BF16), before, preferred_element_type=F32) + carry_ref[...]
    rank_ref[0, 0:1, :] = jnp.sum(jnp.where(oh1, prefix, 0.0), axis=0, keepdims=True).astype(jnp.int32)
    rank_ref[0, 1:2, :] = jnp.sum(jnp.where(oh2, prefix, 0.0), axis=0, keepdims=True).astype(jnp.int32)
    total = carry_ref[...] + jnp.sum(oh, axis=1, keepdims=True)
    carry_ref[...] = total
    cnt_ref[...] = total.astype(jnp.int32)


def _outproj_call(mix, x2, w_bf, gate, shift, scale, nw, wr_t, rb_col, seq):
    n = x2.shape[0]
    tiles_per_batch = seq // OUT_TM
    bmap = lambda i: (i // tiles_per_batch, 0, 0)
    row = lambda i: (i, 0)
    lane = lambda i: (0, i)
    const = lambda i: (0, 0)
    table = lambda i: (i, 0, 0)
    return pl.pallas_call(
        _outproj_kernel,
        grid=(n // OUT_TM,),
        in_specs=[
            pl.BlockSpec((OUT_TM, D), row),
            pl.BlockSpec((OUT_TM, D), row),
            pl.BlockSpec((D, D), const),
            pl.BlockSpec((1, 1, D), bmap),
            pl.BlockSpec((1, 1, D), bmap),
            pl.BlockSpec((1, 1, D), bmap),
            pl.BlockSpec((1, D), const),
            pl.BlockSpec((NE, D), const),
            pl.BlockSpec((NE, 1), const),
        ],
        out_specs=[
            pl.BlockSpec((OUT_TM, D), row),
            pl.BlockSpec((OUT_TM * PK_SUB, LANES), row),
            pl.BlockSpec((1, 2, OUT_TM), table),
            pl.BlockSpec((2, OUT_TM), lane),
            pl.BlockSpec((1, 2, OUT_TM), table),
            pl.BlockSpec((NE, 1), const),
        ],
        out_shape=[
            jax.ShapeDtypeStruct((n, D), F32),
            jax.ShapeDtypeStruct((n * PK_SUB, LANES), jnp.uint32),
            jax.ShapeDtypeStruct((n // OUT_TM, 2, OUT_TM), jnp.int32),
            jax.ShapeDtypeStruct((2, n), F32),
            jax.ShapeDtypeStruct((n // OUT_TM, 2, OUT_TM), jnp.int32),
            jax.ShapeDtypeStruct((NE, 1), jnp.int32),
        ],
        scratch_shapes=[pltpu.VMEM((NE, 1), F32)],
        compiler_params=pltpu.CompilerParams(
            dimension_semantics=("arbitrary",), vmem_limit_bytes=VMEM_LIMIT),
        name="out_proj_router",
    )(mix, x2, w_bf, gate, shift, scale, nw, wr_t, rb_col)


def _row_copy(src_hbm, src_row, dst_vmem, dst_row, sem):
    return pltpu.make_async_copy(src_hbm.at[pl.ds(src_row, 1), :], dst_vmem.at[pl.ds(dst_row, 1), :], sem)


def _tile_rows(row, nrows):
    start = row * PK_SUB
    if not isinstance(row, int):
        start = pl.multiple_of(start, PK_SUB)
    return pl.ds(start, nrows * PK_SUB)


def _tile_copy(src, src_row, dst, dst_row, nrows, sem):
    return pltpu.make_async_copy(src.at[_tile_rows(src_row, nrows), :], dst.at[_tile_rows(dst_row, nrows), :], sem)


def _pad_chunks(fn):
    size = EXP_BM // 2
    while size >= 1:
        fn(size)
        size //= 2


def _dispatch_kernel(ps_ref, pe_ref, cnt_ref, na_ref, eid_ref, rank_ref, h_ref, xg_hbm, zbuf, sem, zsem):
    i = pl.program_id(0)
    nb = xg_hbm.shape[0] // (EXP_BM * PK_SUB)

    @pl.when(i == 0)
    def _():
        zbuf[...] = jnp.zeros_like(zbuf)

        def pads(wait):
            def per_expert(e, carry):
                base = ps_ref[e] + cnt_ref[e]
                npad = pe_ref[e] - base

                def chunk(size):
                    off = npad & ~(2 * size - 1)

                    @pl.when((npad & size) != 0)
                    def _():
                        cp = _tile_copy(zbuf, 0, xg_hbm, base + off, size, zsem)
                        cp.wait() if wait else cp.start()

                _pad_chunks(chunk)
                return carry

            lax.fori_loop(0, NE, per_expert, 0)

            def per_block(b, carry):
                cp = _tile_copy(zbuf, 0, xg_hbm, b * EXP_BM, EXP_BM, zsem)
                cp.wait() if wait else cp.start()
                return carry

            lax.fori_loop(na_ref[0], nb, per_block, 0)

        pads(False)
        pads(True)

    def issue(r, carry):
        for k in range(2):
            slot = ps_ref[eid_ref[0, k, r]] + rank_ref[0, k, r]
            _tile_copy(h_ref, r, xg_hbm, slot, 1, sem).start()
        return carry

    lax.fori_loop(0, DSP_TM, issue, 0)

    def drain(r, carry):
        for k in range(2):
            _tile_copy(h_ref, r, xg_hbm, 0, 1, sem).wait()
        return carry

    lax.fori_loop(0, DSP_TM, drain, 0)


def _dispatch_call(pad_starts, pad_ends, counts, n_active, eid, rank, h2p, nb):
    n = h2p.shape[0] // PK_SUB
    grid_spec = pltpu.PrefetchScalarGridSpec(
        num_scalar_prefetch=4,
        grid=(n // DSP_TM,),
        in_specs=[
            pl.BlockSpec((1, 2, DSP_TM), lambda i, *_: (i, 0, 0), memory_space=pltpu.SMEM),
            pl.BlockSpec((1, 2, DSP_TM), lambda i, *_: (i, 0, 0), memory_space=pltpu.SMEM),
            pl.BlockSpec((DSP_TM * PK_SUB, LANES), lambda i, *_: (i, 0)),
        ],
        out_specs=pl.BlockSpec(memory_space=pl.ANY),
        scratch_shapes=[pltpu.VMEM((EXP_BM * PK_SUB, LANES), jnp.uint32),
                        pltpu.SemaphoreType.DMA(()), pltpu.SemaphoreType.DMA(())],
    )
    return pl.pallas_call(
        _dispatch_kernel,
        grid_spec=grid_spec,
        out_shape=jax.ShapeDtypeStruct((nb * EXP_BM * PK_SUB, LANES), jnp.uint32),
        compiler_params=pltpu.CompilerParams(
            dimension_semantics=("arbitrary",), vmem_limit_bytes=VMEM_LIMIT),
        name="moe_dispatch",
    )(pad_starts, pad_ends, counts, n_active, eid, rank, h2p)


def _unpack_rows(x_ref, rows):
    his, los = [], []
    for j in range(PK_SUB):
        p = x_ref[pl.ds(j, rows, stride=PK_SUB), :]
        his.append(lax.bitcast_convert_type(p & jnp.uint32(0xFFFF0000), F32).astype(BF16))
        los.append(lax.bitcast_convert_type(p << 16, F32).astype(BF16))
    return jnp.concatenate(his + los, axis=1)


def _expert_kernel(be_ref, na_ref, x_ref, w1_ref, w3_ref, w2_ref, o_ref, w1b, w3b, w2b):
    i = pl.program_id(0)
    active = i < na_ref[0]
    changed = jnp.logical_or(i == 0, be_ref[i] != be_ref[jnp.maximum(i - 1, 0)])

    @pl.when(jnp.logical_and(active, changed))
    def _():
        w1b[...] = w1_ref[0, 0].astype(BF16)
        w3b[...] = w3_ref[0, 0].astype(BF16)
        w2b[...] = w2_ref[0, 0].astype(BF16)

    @pl.when(active)
    def _():
        xb = _unpack_rows(x_ref, EXP_BM)
        h1 = jnp.dot(xb, w1b[...], preferred_element_type=F32)
        h3 = jnp.dot(xb, w3b[...], preferred_element_type=F32)
        act = (h1 * _sigmoid(h1) * h3).astype(BF16)
        o_ref[...] = jnp.dot(act, w2b[...], preferred_element_type=F32)

    @pl.when(jnp.logical_not(active))
    def _():
        o_ref[...] = jnp.zeros_like(o_ref)


def _expert_call(block_expert, n_active, xg, w1, w3, w2, layer):
    nb = block_expert.shape[0]
    live = lambda i, be, na: (jnp.minimum(i, na[0] - 1), 0)
    wmap = lambda i, be, na: (layer, be[i], 0, 0)
    grid_spec = pltpu.PrefetchScalarGridSpec(
        num_scalar_prefetch=2,
        grid=(nb,),
        in_specs=[
            pl.BlockSpec((EXP_BM * PK_SUB, LANES), live),
            pl.BlockSpec((1, 1, D, FF), wmap),
            pl.BlockSpec((1, 1, D, FF), wmap),
            pl.BlockSpec((1, 1, FF, D), wmap),
        ],
        out_specs=pl.BlockSpec((EXP_BM, D), lambda i, be, na: (i, 0)),
        scratch_shapes=[pltpu.VMEM((D, FF), BF16), pltpu.VMEM((D, FF), BF16), pltpu.VMEM((FF, D), BF16)],
    )
    return pl.pallas_call(
        _expert_kernel,
        grid_spec=grid_spec,
        out_shape=jax.ShapeDtypeStruct((nb * EXP_BM, D), F32),
        compiler_params=pltpu.CompilerParams(
            dimension_semantics=("arbitrary",), vmem_limit_bytes=VMEM_LIMIT),
        name="expert_blocks",
    )(block_expert, n_active, xg, w1, w3, w2)


def _combine_kernel(ps_ref, eid_ref, rank_ref, x_ref, gate_ref, gw_ref, fw_ref, y_hbm, o_ref, ybuf, sem, *,
                    final_norm):
    base = (pl.program_id(0) % (OUT_TM // CMB_TM)) * CMB_TM

    def issue(r, carry):
        for k in range(2):
            slot = ps_ref[eid_ref[0, k, base + r]] + rank_ref[0, k, base + r]
            _row_copy(y_hbm, slot, ybuf.at[k], r, sem).start()
        return carry

    lax.fori_loop(0, CMB_TM, issue, 0)

    def drain(r, carry):
        _row_copy(y_hbm, 0, ybuf.at[0], r, sem).wait()
        _row_copy(y_hbm, 0, ybuf.at[1], r, sem).wait()
        return carry

    lax.fori_loop(0, CMB_TM, drain, 0)
    gw = gw_ref[...]
    moe = gw[:, 0:1] * ybuf[0] + gw[:, 1:2] * ybuf[1]
    x2 = x_ref[...] + gate_ref[0] * moe
    if final_norm:
        ms = jnp.mean(x2 * x2, axis=-1, keepdims=True)
        x2 = x2 * lax.rsqrt(ms + NORM_EPS) * fw_ref[...]
    o_ref[...] = x2


def _combine_call(pad_starts, eid, rank, x1, gate, gw_col, fw, yg, seq, final_norm):
    n = x1.shape[0]
    tiles_per_batch = seq // CMB_TM
    per_table = OUT_TM // CMB_TM
    table = lambda i, ps: (i // per_table, 0, 0)
    grid_spec = pltpu.PrefetchScalarGridSpec(
        num_scalar_prefetch=1,
        grid=(n // CMB_TM,),
        in_specs=[
            pl.BlockSpec((1, 2, OUT_TM), table, memory_space=pltpu.SMEM),
            pl.BlockSpec((1, 2, OUT_TM), table, memory_space=pltpu.SMEM),
            pl.BlockSpec((CMB_TM, D), lambda i, ps: (i, 0)),
            pl.BlockSpec((1, 1, D), lambda i, ps: (i // tiles_per_batch, 0, 0)),
            pl.BlockSpec((CMB_TM, 2), lambda i, ps: (i, 0)),
            pl.BlockSpec((1, D), lambda i, ps: (0, 0)),
            pl.BlockSpec(memory_space=pl.ANY),
        ],
        out_specs=pl.BlockSpec((CMB_TM, D), lambda i, ps: (i, 0)),
        scratch_shapes=[pltpu.VMEM((2, CMB_TM, D), F32), pltpu.SemaphoreType.DMA(())],
    )
    return pl.pallas_call(
        functools.partial(_combine_kernel, final_norm=final_norm),
        grid_spec=grid_spec,
        out_shape=jax.ShapeDtypeStruct((n, D), F32),
        compiler_params=pltpu.CompilerParams(
            dimension_semantics=("arbitrary",), vmem_limit_bytes=VMEM_LIMIT),
        name="moe_combine",
    )(pad_starts, eid, rank, x1, gate, gw_col, fw, yg)


def _decay_tables():
    log_gamma = jnp.log1p(-jnp.exp2(-5.0 - jnp.arange(NH, dtype=F32)))
    pos = jnp.arange(CH, dtype=F32)
    diff = pos[:, None] - pos[None, :]
    mask = jnp.where(diff >= 0, jnp.exp(log_gamma[:, None, None] * jnp.maximum(diff, 0.0)), 0.0)
    q_decay = jnp.exp(log_gamma[:, None] * (pos + 1.0))
    k_decay = jnp.exp(log_gamma[:, None] * (CH - 1.0 - pos))
    chunk_decay = jnp.exp(log_gamma * CH)
    return mask.astype(F32), q_decay.T, k_decay.T, chunk_decay


def kernel(x, c, positions, w_ada, b_ada, norm1_w, norm2_w, w_in, w_out, ret_norm_w, sg_w_s, sg_b_s,
           w_router, router_bias, w1, w3, w2, final_norm_w):
    bsz, seq, d = x.shape
    n = bsz * seq
    assert d == D and seq % INP_TM == 0 and seq % MIX_ROWS == 0 and n % CMB_TM == 0

    c_pad = jnp.zeros((8, D), F32).at[:bsz].set(c)
    mod = _ada_call(c_pad, w_ada, b_ada)[:, :bsz]
    mod = mod.reshape(N_LAYERS, bsz, N_MOD, 1, D)

    half = HD // 2
    inv_freq = (ROPE_THETA ** (-jnp.arange(half, dtype=F32) / half)).reshape(1, half)
    cos, sin = _rope_call(positions.reshape(n, 1), inv_freq)
    decay_mask, q_decay_t, k_decay_t, chunk_decay = _decay_tables()

    perm = (jnp.arange(NE) % NEG) * EPG + jnp.arange(NE) // NEG
    wr_t = w_router.T[perm]
    rb_col = router_bias[perm].reshape(NE, 1)

    nb = n * 2 // EXP_BM + NE
    xs = x.reshape(n, D)
    for l in range(N_LAYERS):
        shift1, scale1, gate1, shift2, scale2, gate2 = [mod[l, :, t] for t in range(N_MOD)]
        proj = _inproj_call(xs, shift1, scale1, norm1_w[l].reshape(1, D), w_in[l].astype(BF16), cos, sin, seq)
        mix = _mix_call(proj, chunk_decay, decay_mask, q_decay_t, k_decay_t, ret_norm_w[l].reshape(1, RW),
                        sg_w_s[l], sg_b_s[l].T, bsz, seq)
        x1, h2p, eid, gw, rank, counts = _outproj_call(
            mix, xs, w_out[l].astype(BF16), gate1, shift2, scale2, norm2_w[l].reshape(1, D), wr_t, rb_col, seq)

        counts = counts.reshape(NE)
        padded = (counts + EXP_BM - 1) // EXP_BM * EXP_BM
        pad_ends = jnp.cumsum(padded)
        pad_starts = pad_ends - padded
        block_start = jnp.arange(nb, dtype=jnp.int32) * EXP_BM
        block_expert = jnp.minimum(jnp.searchsorted(pad_ends, block_start, side='right'), NE - 1).astype(jnp.int32)
        n_active = (pad_ends[-1] // EXP_BM).astype(jnp.int32).reshape(1)

        xg = _dispatch_call(pad_starts, pad_ends, counts, n_active, eid, rank, h2p, nb)
        yg = _expert_call(block_expert, n_active, xg, w1, w3, w2, l)
        xs = _combine_call(pad_starts, eid, rank, x1, gate2, gw.T, final_norm_w.reshape(1, D), yg, seq,
                           final_norm=(l == N_LAYERS - 1))
    return xs.reshape(bsz, seq, D)
```

```python
import functools

import jax
import jax.numpy as jnp
from jax import lax
from jax.experimental import pallas as pl
from jax.experimental.pallas import tpu as pltpu

F32 = jnp.float32
BF16 = jnp.bfloat16
HIGHEST = lax.Precision.HIGHEST

D = 2048
N_LAYERS = 2
RW = D // 2
NH = 4
HD = RW // NH
CH = 128
ROPE_THETA = 10000.0
SGW = D - RW
NG = 8
GD = SGW // NG
PROJ = 4 * RW + 2 * SGW
NE = 32
NEG = 8
EPG = NE // NEG
FF = 512
N_MOD = 6
NORM_EPS = 1e-6

VMEM_LIMIT = 48 * 1024 * 1024
INP_VMEM_LIMIT = 56 * 1024 * 1024

ADA_TN = 1024
ROPE_TM = 2048
INP_TM = 256
INP_TN = 512
MIX_ROWS = 512
OUT_TM = 512
EXP_BM = 512
EXP_SUB = 2
DSP_TM = OUT_TM
CMB_TM = 512

LANES = 128
PK_SUB = D // 2 // LANES


def _sigmoid(v):
    return 1.0 / (1.0 + jnp.exp(-v))


def _gelu_tanh(v):
    return 0.5 * v * (1.0 + jnp.tanh(0.7978845608028654 * (v + 0.044715 * (v * v * v))))


def _ada_kernel(c_ref, w_ref, b_ref, o_ref):
    c = c_ref[...]
    ca = c * _sigmoid(c)
    o_ref[0] = jnp.dot(ca, w_ref[0], precision=HIGHEST, preferred_element_type=F32) + b_ref[0]


def _ada_call(c_pad, w_ada, b_ada):
    depth, _, ncol = w_ada.shape
    return pl.pallas_call(
        _ada_kernel,
        grid=(depth, ncol // ADA_TN),
        in_specs=[
            pl.BlockSpec((8, D), lambda l, j: (0, 0)),
            pl.BlockSpec((1, D, ADA_TN), lambda l, j: (l, 0, j)),
            pl.BlockSpec((1, 1, ADA_TN), lambda l, j: (l, 0, j)),
        ],
        out_specs=pl.BlockSpec((1, 8, ADA_TN), lambda l, j: (l, 0, j)),
        out_shape=jax.ShapeDtypeStruct((depth, 8, ncol), F32),
        compiler_params=pltpu.CompilerParams(vmem_limit_bytes=VMEM_LIMIT),
        name="ada_mod",
    )(c_pad, w_ada, b_ada.reshape(depth, 1, ncol))


def _rope_kernel(pos_ref, freq_ref, cos_ref, sin_ref):
    ang = pos_ref[...].astype(F32) * freq_ref[...]
    cos_ref[...] = jnp.cos(ang)
    sin_ref[...] = jnp.sin(ang)


def _rope_call(pos_col, inv_freq):
    n = pos_col.shape[0]
    half = inv_freq.shape[1]
    return pl.pallas_call(
        _rope_kernel,
        grid=(n // ROPE_TM,),
        in_specs=[
            pl.BlockSpec((ROPE_TM, 1), lambda i: (i, 0)),
            pl.BlockSpec((1, half), lambda i: (0, 0)),
        ],
        out_specs=[
            pl.BlockSpec((ROPE_TM, half), lambda i: (i, 0)),
            pl.BlockSpec((ROPE_TM, half), lambda i: (i, 0)),
        ],
        out_shape=[jax.ShapeDtypeStruct((n, half), F32)] * 2,
        name="rope_tables",
    )(pos_col, inv_freq)


def _inproj_kernel(x_ref, shift_ref, scale_ref, nw_ref, cos_ref, sin_ref, w_hbm, o_ref, w_vmem, sem):
    @pl.when(pl.program_id(0) == 0)
    def _():
        cp = pltpu.make_async_copy(w_hbm, w_vmem, sem)
        cp.start()
        cp.wait()

    x = x_ref[...]
    ms = jnp.mean(x * x, axis=-1, keepdims=True)
    h = x * lax.rsqrt(ms + NORM_EPS) * nw_ref[...]
    h = (h * (1.0 + scale_ref[0]) + shift_ref[0]).astype(BF16)
    cos = cos_ref[...]
    sin = sin_ref[...]
    half = HD // 2

    for j in range(PROJ // INP_TN):
        c0 = j * INP_TN
        acc = jnp.dot(h, w_vmem[:, c0:c0 + INP_TN], preferred_element_type=F32)
        sec = c0 // RW
        if sec in (0, 1):
            scale = 1.0 if sec == 0 else HD ** -0.5
            for hh in range(INP_TN // HD):
                a = acc[:, hh * HD:hh * HD + half]
                b = acc[:, hh * HD + half:(hh + 1) * HD]
                o_ref[:, c0 + hh * HD:c0 + hh * HD + half] = ((a * cos - b * sin) * scale).astype(BF16)
                o_ref[:, c0 + hh * HD + half:c0 + (hh + 1) * HD] = ((b * cos + a * sin) * scale).astype(BF16)
        elif sec == 2:
            o_ref[:, c0:c0 + INP_TN] = acc.astype(BF16)
        elif sec == 3:
            o_ref[:, c0:c0 + INP_TN] = (acc * _sigmoid(acc)).astype(BF16)
        elif sec == 4:
            o_ref[:, c0:c0 + INP_TN] = _gelu_tanh(acc).astype(BF16)
        else:
            for gg in range(INP_TN // GD):
                t = _gelu_tanh(acc[:, gg * GD:(gg + 1) * GD])
                ms_g = jnp.mean(t * t, axis=-1, keepdims=True)
                o_ref[:, c0 + gg * GD:c0 + (gg + 1) * GD] = (t * lax.rsqrt(ms_g + NORM_EPS)).astype(BF16)


def _inproj_call(x2, shift, scale, nw, w_bf, cos, sin, seq):
    n = x2.shape[0]
    tiles_per_batch = seq // INP_TM
    bmap = lambda i: (i // tiles_per_batch, 0, 0)
    return pl.pallas_call(
        _inproj_kernel,
        grid=(n // INP_TM,),
        in_specs=[
            pl.BlockSpec((INP_TM, D), lambda i: (i, 0)),
            pl.BlockSpec((1, 1, D), bmap),
            pl.BlockSpec((1, 1, D), bmap),
            pl.BlockSpec((1, D), lambda i: (0, 0)),
            pl.BlockSpec((INP_TM, HD // 2), lambda i: (i, 0)),
            pl.BlockSpec((INP_TM, HD // 2), lambda i: (i, 0)),
            pl.BlockSpec(memory_space=pl.ANY),
        ],
        out_specs=pl.BlockSpec((INP_TM, PROJ), lambda i: (i, 0)),
        out_shape=jax.ShapeDtypeStruct((n, PROJ), BF16),
        scratch_shapes=[pltpu.VMEM((D, PROJ), BF16), pltpu.SemaphoreType.DMA(())],
        compiler_params=pltpu.CompilerParams(
            dimension_semantics=("arbitrary",), vmem_limit_bytes=INP_VMEM_LIMIT),
        name="in_proj",
    )(x2, shift, scale, nw, cos, sin, w_bf)


def _mix_kernel(cd_ref, q_ref, k_ref, v_ref, g_ref, u_ref, vs_ref, dm_ref, qd_ref, kd_ref,
                rnw_ref, ws_ref, bs_ref, o_ref, state_ref):
    @pl.when(pl.program_id(1) == 0)
    def _():
        state_ref[...] = jnp.zeros_like(state_ref)

    row = lax.broadcasted_iota(jnp.int32, (CH, CH), 0)
    col = lax.broadcasted_iota(jnp.int32, (CH, CH), 1)
    causal = row >= col

    def chunk(c, carry):
        r0 = pl.multiple_of(c * CH, CH)
        rows = pl.ds(r0, CH)
        for h in range(NH):
            cols = slice(h * HD, (h + 1) * HD)
            q = q_ref[rows, cols]
            k = k_ref[rows, cols]
            v = v_ref[rows, cols]
            st = state_ref[h]
            s = lax.dot_general(q, k, (((1,), (1,)), ((), ())), preferred_element_type=F32)
            s = s * dm_ref[h]
            intra = jnp.dot(s.astype(BF16), v, preferred_element_type=F32)
            cross = jnp.dot(q, st.astype(BF16), preferred_element_type=F32) * qd_ref[:, h:h + 1]
            kdec = (k.astype(F32) * kd_ref[:, h:h + 1]).astype(BF16)
            upd = lax.dot_general(kdec, v, (((0,), (0,)), ((), ())), preferred_element_type=F32)
            state_ref[h] = st * cd_ref[h] + upd
            o = intra + cross
            ms = jnp.mean(o * o, axis=-1, keepdims=True)
            o = o * lax.rsqrt(ms + NORM_EPS) * rnw_ref[:, cols]
            o_ref[rows, cols] = (o * g_ref[rows, cols].astype(F32)).astype(BF16)
        for g in range(NG):
            cols = slice(g * GD, (g + 1) * GD)
            wm = jnp.where(causal, ws_ref[g], 0.0).astype(BF16)
            z = jnp.dot(wm, vs_ref[rows, cols], preferred_element_type=F32) + bs_ref[:, g:g + 1]
            o_ref[rows, RW + g * GD:RW + (g + 1) * GD] = (u_ref[rows, cols].astype(F32) * z).astype(BF16)
        return carry

    lax.fori_loop(0, MIX_ROWS // CH, chunk, 0)


def _mix_call(proj, chunk_decay, decay_mask, q_decay_t, k_decay_t, rnw, ws, bs_t, bsz, seq):
    n = proj.shape[0]
    steps = seq // MIX_ROWS

    def colblk(jb):
        return pl.BlockSpec((MIX_ROWS, RW), lambda b, s, cd: (b * steps + s, jb))

    const2 = lambda b, s, cd: (0, 0)
    const3 = lambda b, s, cd: (0, 0, 0)
    grid_spec = pltpu.PrefetchScalarGridSpec(
        num_scalar_prefetch=1,
        grid=(bsz, steps),
        in_specs=[colblk(0), colblk(1), colblk(2), colblk(3), colblk(4), colblk(5),
                  pl.BlockSpec((NH, CH, CH), const3),
                  pl.BlockSpec((CH, NH), const2),
                  pl.BlockSpec((CH, NH), const2),
                  pl.BlockSpec((1, RW), const2),
                  pl.BlockSpec((NG, CH, CH), const3),
                  pl.BlockSpec((CH, NG), const2)],
        out_specs=pl.BlockSpec((MIX_ROWS, D), lambda b, s, cd: (b * steps + s, 0)),
        scratch_shapes=[pltpu.VMEM((NH, HD, HD), F32)],
    )
    return pl.pallas_call(
        _mix_kernel,
        grid_spec=grid_spec,
        out_shape=jax.ShapeDtypeStruct((n, D), BF16),
        compiler_params=pltpu.CompilerParams(
            dimension_semantics=("arbitrary", "arbitrary"), vmem_limit_bytes=VMEM_LIMIT),
        name="retention_spatial_mix",
    )(chunk_decay, proj, proj, proj, proj, proj, proj, decay_mask, q_decay_t, k_decay_t, rnw, ws, bs_t)


def _outproj_kernel(mix_ref, x_ref, w_ref, gate_ref, shift_ref, scale_ref, nw_ref, wr_ref, rb_ref,
                    x1_ref, h2_ref, eid_ref, gw_ref, rank_ref, cnt_ref, carry_ref):
    i = pl.program_id(0)

    @pl.when(i == 0)
    def _():
        carry_ref[...] = jnp.zeros_like(carry_ref)

    y = jnp.dot(mix_ref[...], w_ref[...], preferred_element_type=F32)
    x1 = x_ref[...] + gate_ref[0] * y
    x1_ref[...] = x1
    ms = jnp.mean(x1 * x1, axis=-1, keepdims=True)
    h2 = x1 * lax.rsqrt(ms + NORM_EPS) * nw_ref[...]
    h2 = h2 * (1.0 + scale_ref[0]) + shift_ref[0]
    _pack_rows(h2, h2_ref, OUT_TM)

    nt = (((1,), (1,)), ((), ()))
    wr = wr_ref[...]
    w_hi = wr.astype(BF16)
    w_lo = (wr - w_hi.astype(F32)).astype(BF16)
    h_hi = h2.astype(BF16)
    h_lo = (h2 - h_hi.astype(F32)).astype(BF16)
    p_hi = lax.dot_general(jnp.concatenate([w_hi, w_lo], axis=0), h_hi, nt, preferred_element_type=F32)
    p_lo = lax.dot_general(w_hi, h_lo, nt, preferred_element_type=F32)
    logits = p_hi[:NE] + p_hi[NE:] + p_lo
    scores = _sigmoid(logits)
    biased = scores + rb_ref[...]
    a = [biased[m * NEG:(m + 1) * NEG] for m in range(EPG)]
    sc = [scores[m * NEG:(m + 1) * NEG] for m in range(EPG)]
    p, q = jnp.maximum(a[0], a[1]), jnp.minimum(a[0], a[1])
    r, s = jnp.maximum(a[2], a[3]), jnp.minimum(a[2], a[3])
    group_score = jnp.maximum(p, r) + jnp.maximum(jnp.minimum(p, r), jnp.maximum(q, s))
    gidx = lax.broadcasted_iota(jnp.int32, group_score.shape, 0)
    gmax = jnp.max(group_score, axis=0, keepdims=True)
    g_sel = jnp.min(jnp.where(group_score == gmax, gidx, NEG), axis=0, keepdims=True)
    pick = gidx == g_sel
    v = [jnp.sum(jnp.where(pick, a[m], 0.0), axis=0, keepdims=True) for m in range(EPG)]
    u = [jnp.sum(jnp.where(pick, sc[m], 0.0), axis=0, keepdims=True) for m in range(EPG)]

    def first_argmax(vals):
        m = functools.reduce(jnp.maximum, vals)
        idx = jnp.full(m.shape, EPG - 1, jnp.int32)
        for t in range(EPG - 2, -1, -1):
            idx = jnp.where(vals[t] == m, t, idx)
        return idx

    def select(vals, idx):
        out = vals[EPG - 1]
        for t in range(EPG - 2, -1, -1):
            out = jnp.where(idx == t, vals[t], out)
        return out

    i1 = first_argmax(v)
    v2 = [jnp.where(i1 == t, -jnp.inf, v[t]) for t in range(EPG)]
    i2 = first_argmax(v2)
    s1 = select(u, i1)
    s2 = select(u, i2)
    denom = s1 + s2
    e1 = g_sel * EPG + i1
    e2 = g_sel * EPG + i2
    eid_ref[0, 0:1, :] = e1
    eid_ref[0, 1:2, :] = e2
    gw_ref[0:1, :] = s1 / denom
    gw_ref[1:2, :] = s2 / denom

    tm = e1.shape[1]
    eidx = lax.broadcasted_iota(jnp.int32, (NE, tm), 0)
    oh1 = eidx == e1
    oh2 = eidx == e2
    oh = jnp.where(oh1 | oh2, 1.0, 0.0)
    tr = lax.broadcasted_iota(jnp.int32, (tm, tm), 0)
    tc = lax.broadcasted_iota(jnp.int32, (tm, tm), 1)
    before = jnp.where(tr < tc, 1.0, 0.0).astype(BF16)
    prefix = jnp.dot(oh.astype(BF16), before, preferred_element_type=F32) + carry_ref[...]
    rank_ref[0, 0:1, :] = jnp.sum(jnp.where(oh1, prefix, 0.0), axis=0, keepdims=True).astype(jnp.int32)
    rank_ref[0, 1:2, :] = jnp.sum(jnp.where(oh2, prefix, 0.0), axis=0, keepdims=True).astype(jnp.int32)
    total = carry_ref[...] + jnp.sum(oh, axis=1, keepdims=True)
    carry_ref[...] = total
    cnt_ref[...] = total.astype(jnp.int32)


def _outproj_call(mix, x2, w_bf, gate, shift, scale, nw, wr_t, rb_col, seq):
    n = x2.shape[0]
    tiles_per_batch = seq // OUT_TM
    bmap = lambda i: (i // tiles_per_batch, 0, 0)
    row = lambda i: (i, 0)
    lane = lambda i: (0, i)
    const = lambda i: (0, 0)
    table = lambda i: (i, 0, 0)
    return pl.pallas_call(
        _outproj_kernel,
        grid=(n // OUT_TM,),
        in_specs=[
            pl.BlockSpec((OUT_TM, D), row),
            pl.BlockSpec((OUT_TM, D), row),
            pl.BlockSpec((D, D), const),
            pl.BlockSpec((1, 1, D), bmap),
            pl.BlockSpec((1, 1, D), bmap),
            pl.BlockSpec((1, 1, D), bmap),
            pl.BlockSpec((1, D), const),
            pl.BlockSpec((NE, D), const),
            pl.BlockSpec((NE, 1), const),
        ],
        out_specs=[
            pl.BlockSpec((OUT_TM, D), row),
            pl.BlockSpec((OUT_TM * PK_SUB, LANES), row),
            pl.BlockSpec((1, 2, OUT_TM), table),
            pl.BlockSpec((2, OUT_TM), lane),
            pl.BlockSpec((1, 2, OUT_TM), table),
            pl.BlockSpec((NE, 1), const),
        ],
        out_shape=[
            jax.ShapeDtypeStruct((n, D), F32),
            jax.ShapeDtypeStruct((n * PK_SUB, LANES), jnp.uint32),
            jax.ShapeDtypeStruct((n // OUT_TM, 2, OUT_TM), jnp.int32),
            jax.ShapeDtypeStruct((2, n), F32),
            jax.ShapeDtypeStruct((n // OUT_TM, 2, OUT_TM), jnp.int32),
            jax.ShapeDtypeStruct((NE, 1), jnp.int32),
        ],
        scratch_shapes=[pltpu.VMEM((NE, 1), F32)],
        compiler_params=pltpu.CompilerParams(
            dimension_semantics=("arbitrary",), vmem_limit_bytes=VMEM_LIMIT),
        name="out_proj_router",
    )(mix, x2, w_bf, gate, shift, scale, nw, wr_t, rb_col)


def _pack_rows(v, o_ref, rows):
    hi = lax.bitcast_convert_type(v[:, :D // 2].astype(BF16).astype(F32), jnp.uint32)
    lo = lax.bitcast_convert_type(v[:, D // 2:].astype(BF16).astype(F32), jnp.uint32)
    packed = hi | (lo >> 16)
    for j in range(PK_SUB):
        o_ref[pl.ds(j, rows, stride=PK_SUB), :] = packed[:, j * LANES:(j + 1) * LANES]


def _unpack_halves(x_ref, rows):
    his, los = [], []
    for j in range(PK_SUB):
        p = x_ref[pl.ds(j, rows, stride=PK_SUB), :]
        his.append(lax.bitcast_convert_type(p & jnp.uint32(0xFFFF0000), F32))
        los.append(lax.bitcast_convert_type(p << 16, F32))
    return his, los


def _tile_rows(row, nrows):
    start = row * PK_SUB
    if not isinstance(row, int):
        start = pl.multiple_of(start, PK_SUB)
    return pl.ds(start, nrows * PK_SUB)


def _tile_copy(src, src_row, dst, dst_row, nrows, sem):
    return pltpu.make_async_copy(src.at[_tile_rows(src_row, nrows), :], dst.at[_tile_rows(dst_row, nrows), :], sem)


def _pad_chunks(fn):
    size = EXP_BM // 2
    while size >= 1:
        fn(size)
        size //= 2


def _dispatch_kernel(ps_ref, pe_ref, cnt_ref, na_ref, eid_ref, rank_ref, h_ref, xg_hbm, zbuf, sem, zsem):
    i = pl.program_id(0)
    nb = xg_hbm.shape[0] // (EXP_BM * PK_SUB)

    @pl.when(i == 0)
    def _():
        zbuf[...] = jnp.zeros_like(zbuf)

        def pads(wait):
            def per_expert(e, carry):
                base = ps_ref[e] + cnt_ref[e]
                npad = pe_ref[e] - base

                def chunk(size):
                    off = npad & ~(2 * size - 1)

                    @pl.when((npad & size) != 0)
                    def _():
                        cp = _tile_copy(zbuf, 0, xg_hbm, base + off, size, zsem)
                        cp.wait() if wait else cp.start()

                _pad_chunks(chunk)
                return carry

            lax.fori_loop(0, NE, per_expert, 0)

            def per_block(b, carry):
                cp = _tile_copy(zbuf, 0, xg_hbm, b * EXP_BM, EXP_BM, zsem)
                cp.wait() if wait else cp.start()
                return carry

            lax.fori_loop(na_ref[0], nb, per_block, 0)

        pads(False)
        pads(True)

    def issue(r, carry):
        for k in range(2):
            slot = ps_ref[eid_ref[0, k, r]] + rank_ref[0, k, r]
            _tile_copy(h_ref, r, xg_hbm, slot, 1, sem).start(priority=k)
        return carry

    lax.fori_loop(0, DSP_TM, issue, 0)

    def drain(r, carry):
        for k in range(2):
            _tile_copy(h_ref, r, xg_hbm, 0, 1, sem).wait()
        return carry

    lax.fori_loop(0, DSP_TM, drain, 0)


def _dispatch_call(pad_starts, pad_ends, counts, n_active, eid, rank, h2p, nb):
    n = h2p.shape[0] // PK_SUB
    grid_spec = pltpu.PrefetchScalarGridSpec(
        num_scalar_prefetch=4,
        grid=(n // DSP_TM,),
        in_specs=[
            pl.BlockSpec((1, 2, DSP_TM), lambda i, *_: (i, 0, 0), memory_space=pltpu.SMEM),
            pl.BlockSpec((1, 2, DSP_TM), lambda i, *_: (i, 0, 0), memory_space=pltpu.SMEM),
            pl.BlockSpec((DSP_TM * PK_SUB, LANES), lambda i, *_: (i, 0)),
        ],
        out_specs=pl.BlockSpec(memory_space=pl.ANY),
        scratch_shapes=[pltpu.VMEM((EXP_BM * PK_SUB, LANES), jnp.uint32),
                        pltpu.SemaphoreType.DMA(()), pltpu.SemaphoreType.DMA(())],
    )
    return pl.pallas_call(
        _dispatch_kernel,
        grid_spec=grid_spec,
        out_shape=jax.ShapeDtypeStruct((nb * EXP_BM * PK_SUB, LANES), jnp.uint32),
        compiler_params=pltpu.CompilerParams(
            dimension_semantics=("arbitrary",), vmem_limit_bytes=VMEM_LIMIT),
        name="moe_dispatch",
    )(pad_starts, pad_ends, counts, n_active, eid, rank, h2p)


def _expert_kernel(be_ref, na_ref, x_ref, w1_ref, w3_ref, w2_ref, o_ref, w1b, w3b, w2b):
    i = pl.program_id(0)
    active = i < na_ref[0]
    changed = jnp.logical_or(i == 0, be_ref[i] != be_ref[jnp.maximum(i - 1, 0)])

    @pl.when(jnp.logical_and(active, changed))
    def _():
        w1b[...] = w1_ref[0, 0].astype(BF16)
        w3b[...] = w3_ref[0, 0].astype(BF16)
        w2b[...] = w2_ref[0, 0].astype(BF16)

    @pl.when(active)
    def _():
        rows = EXP_BM // EXP_SUB
        for s in range(EXP_SUB):
            part = pl.ds(s * rows * PK_SUB, rows * PK_SUB)
            his, los = _unpack_halves(x_ref.at[part], rows)
            xb = jnp.concatenate([c.astype(BF16) for c in his + los], axis=1)
            h1 = jnp.dot(xb, w1b[...], preferred_element_type=F32)
            h3 = jnp.dot(xb, w3b[...], preferred_element_type=F32)
            act = (h1 * _sigmoid(h1) * h3).astype(BF16)
            _pack_rows(jnp.dot(act, w2b[...], preferred_element_type=F32), o_ref.at[part], rows)

    @pl.when(jnp.logical_not(active))
    def _():
        o_ref[...] = jnp.zeros_like(o_ref)


def _expert_call(block_expert, n_active, xg, w1, w3, w2, layer):
    nb = block_expert.shape[0]
    live = lambda i, be, na: (jnp.minimum(i, na[0] - 1), 0)
    wmap = lambda i, be, na: (layer, be[i], 0, 0)
    grid_spec = pltpu.PrefetchScalarGridSpec(
        num_scalar_prefetch=2,
        grid=(nb,),
        in_specs=[
            pl.BlockSpec((EXP_BM * PK_SUB, LANES), live),
            pl.BlockSpec((1, 1, D, FF), wmap),
            pl.BlockSpec((1, 1, D, FF), wmap),
            pl.BlockSpec((1, 1, FF, D), wmap),
        ],
        out_specs=pl.BlockSpec((EXP_BM * PK_SUB, LANES), lambda i, be, na: (i, 0)),
        scratch_shapes=[pltpu.VMEM((D, FF), BF16), pltpu.VMEM((D, FF), BF16), pltpu.VMEM((FF, D), BF16)],
    )
    return pl.pallas_call(
        _expert_kernel,
        grid_spec=grid_spec,
        out_shape=jax.ShapeDtypeStruct((nb * EXP_BM * PK_SUB, LANES), jnp.uint32),
        compiler_params=pltpu.CompilerParams(
            dimension_semantics=("arbitrary",), vmem_limit_bytes=VMEM_LIMIT),
        name="expert_blocks",
    )(block_expert, n_active, xg, w1, w3, w2)


def _combine_kernel(ps_ref, eid_ref, rank_ref, x_ref, gate_ref, gw_ref, fw_ref, y_hbm, o_ref, ybuf, sem, *,
                    final_norm):
    base = (pl.program_id(0) % (OUT_TM // CMB_TM)) * CMB_TM

    def issue(r, carry):
        for k in range(2):
            slot = ps_ref[eid_ref[0, k, base + r]] + rank_ref[0, k, base + r]
            _tile_copy(y_hbm, slot, ybuf.at[k], r, 1, sem).start(priority=k)
        return carry

    lax.fori_loop(0, CMB_TM, issue, 0)

    def drain(r, carry):
        for k in range(2):
            _tile_copy(y_hbm, 0, ybuf.at[k], r, 1, sem).wait()
        return carry

    lax.fori_loop(0, CMB_TM, drain, 0)
    gw = gw_ref[...]
    h0, l0 = _unpack_halves(ybuf.at[0], CMB_TM)
    h1, l1 = _unpack_halves(ybuf.at[1], CMB_TM)
    g0, g1 = gw[:, 0:1], gw[:, 1:2]
    moe = jnp.concatenate([g0 * a + g1 * b for a, b in zip(h0 + l0, h1 + l1)], axis=1)
    x2 = x_ref[...] + gate_ref[0] * moe
    if final_norm:
        ms = jnp.mean(x2 * x2, axis=-1, keepdims=True)
        x2 = x2 * lax.rsqrt(ms + NORM_EPS) * fw_ref[...]
    o_ref[...] = x2


def _combine_call(pad_starts, eid, rank, x1, gate, gw_col, fw, yg, seq, final_norm):
    n = x1.shape[0]
    tiles_per_batch = seq // CMB_TM
    per_table = OUT_TM // CMB_TM
    table = lambda i, ps: (i // per_table, 0, 0)
    grid_spec = pltpu.PrefetchScalarGridSpec(
        num_scalar_prefetch=1,
        grid=(n // CMB_TM,),
        in_specs=[
            pl.BlockSpec((1, 2, OUT_TM), table, memory_space=pltpu.SMEM),
            pl.BlockSpec((1, 2, OUT_TM), table, memory_space=pltpu.SMEM),
            pl.BlockSpec((CMB_TM, D), lambda i, ps: (i, 0)),
            pl.BlockSpec((1, 1, D), lambda i, ps: (i // tiles_per_batch, 0, 0)),
            pl.BlockSpec((CMB_TM, 2), lambda i, ps: (i, 0)),
            pl.BlockSpec((1, D), lambda i, ps: (0, 0)),
            pl.BlockSpec(memory_space=pl.ANY),
        ],
        out_specs=pl.BlockSpec((CMB_TM, D), lambda i, ps: (i, 0)),
        scratch_shapes=[pltpu.VMEM((2, CMB_TM * PK_SUB, LANES), jnp.uint32), pltpu.SemaphoreType.DMA(())],
    )
    return pl.pallas_call(
        functools.partial(_combine_kernel, final_norm=final_norm),
        grid_spec=grid_spec,
        out_shape=jax.ShapeDtypeStruct((n, D), F32),
        compiler_params=pltpu.CompilerParams(
            dimension_semantics=("arbitrary",), vmem_limit_bytes=VMEM_LIMIT),
        name="moe_combine",
    )(pad_starts, eid, rank, x1, gate, gw_col, fw, yg)


def _decay_tables():
    log_gamma = jnp.log1p(-jnp.exp2(-5.0 - jnp.arange(NH, dtype=F32)))
    pos = jnp.arange(CH, dtype=F32)
    diff = pos[:, None] - pos[None, :]
    mask = jnp.where(diff >= 0, jnp.exp(log_gamma[:, None, None] * jnp.maximum(diff, 0.0)), 0.0)
    q_decay = jnp.exp(log_gamma[:, None] * (pos + 1.0))
    k_decay = jnp.exp(log_gamma[:, None] * (CH - 1.0 - pos))
    chunk_decay = jnp.exp(log_gamma * CH)
    return mask.astype(F32), q_decay.T, k_decay.T, chunk_decay


def kernel(x, c, positions, w_ada, b_ada, norm1_w, norm2_w, w_in, w_out, ret_norm_w, sg_w_s, sg_b_s,
           w_router, router_bias, w1, w3, w2, final_norm_w):
    bsz, seq, d = x.shape
    n = bsz * seq
    assert d == D and seq % INP_TM == 0 and seq % MIX_ROWS == 0 and n % CMB_TM == 0

    c_pad = jnp.zeros((8, D), F32).at[:bsz].set(c)
    mod = _ada_call(c_pad, w_ada, b_ada)[:, :bsz]
    mod = mod.reshape(N_LAYERS, bsz, N_MOD, 1, D)

    half = HD // 2
    inv_freq = (ROPE_THETA ** (-jnp.arange(half, dtype=F32) / half)).reshape(1, half)
    cos, sin = _rope_call(positions.reshape(n, 1), inv_freq)
    decay_mask, q_decay_t, k_decay_t, chunk_decay = _decay_tables()

    perm = (jnp.arange(NE) % NEG) * EPG + jnp.arange(NE) // NEG
    wr_t = w_router.T[perm]
    rb_col = router_bias[perm].reshape(NE, 1)

    nb = n * 2 // EXP_BM + NE
    xs = x.reshape(n, D)
    for l in range(N_LAYERS):
        shift1, scale1, gate1, shift2, scale2, gate2 = [mod[l, :, t] for t in range(N_MOD)]
        proj = _inproj_call(xs, shift1, scale1, norm1_w[l].reshape(1, D), w_in[l].astype(BF16), cos, sin, seq)
        mix = _mix_call(proj, chunk_decay, decay_mask, q_decay_t, k_decay_t, ret_norm_w[l].reshape(1, RW),
                        sg_w_s[l], sg_b_s[l].T, bsz, seq)
        x1, h2p, eid, gw, rank, counts = _outproj_call(
            mix, xs, w_out[l].astype(BF16), gate1, shift2, scale2, norm2_w[l].reshape(1, D), wr_t, rb_col, seq)

        counts = counts.reshape(NE)
        padded = (counts + EXP_BM - 1) // EXP_BM * EXP_BM
        pad_ends = jnp.cumsum(padded)
        pad_starts = pad_ends - padded
        block_start = jnp.arange(nb, dtype=jnp.int32) * EXP_BM
        block_expert = jnp.minimum(jnp.sum(block_start[:, None] >= pad_ends[None, :], axis=1), NE - 1).astype(jnp.int32)
        n_active = (pad_ends[-1] // EXP_BM).astype(jnp.int32).reshape(1)

        xg = _dispatch_call(pad_starts, pad_ends, counts, n_active, eid, rank, h2p, nb)
        yg = _expert_call(block_expert, n_active, xg, w1, w3, w2, l)
        xs = _combine_call(pad_starts, eid, rank, x1, gate2, gw.T, final_norm_w.reshape(1, D), yg, seq,
                           final_norm=(l == N_LAYERS - 1))
    return xs.reshape(bsz, seq, D)
```

```python
import functools

import jax
import jax.numpy as jnp
from jax import lax
from jax.experimental import pallas as pl
from jax.experimental.pallas import tpu as pltpu

F32 = jnp.float32
BF16 = jnp.bfloat16
HIGHEST = lax.Precision.HIGHEST

D = 2048
N_LAYERS = 2
RW = D // 2
NH = 4
HD = RW // NH
CH = 128
ROPE_THETA = 10000.0
SGW = D - RW
NG = 8
GD = SGW // NG
PROJ = 4 * RW + 2 * SGW
NE = 32
NEG = 8
EPG = NE // NEG
FF = 512
N_MOD = 6
NORM_EPS = 1e-6

VMEM_LIMIT = 48 * 1024 * 1024
BIG_VMEM_LIMIT = 56 * 1024 * 1024

ADA_TN = 1024
ROPE_TM = 2048
INP_TM = 256
INP_TN = 512
MIX_ROWS = 512
OUT_TM = 512
EXP_BM = 512
EXP_SUB = 2
CMB_TM = 512
BLK_LANES = 256

LANES = 128
PK_SUB = D // 2 // LANES


def _sigmoid(v):
    return 1.0 / (1.0 + jnp.exp(-v))


def _gelu_tanh(v):
    return 0.5 * v * (1.0 + jnp.tanh(0.7978845608028654 * (v + 0.044715 * (v * v * v))))


def _ada_kernel(c_ref, w_ref, b_ref, o_ref):
    c = c_ref[...]
    ca = c * _sigmoid(c)
    o_ref[0] = jnp.dot(ca, w_ref[0], precision=HIGHEST, preferred_element_type=F32) + b_ref[0]


def _ada_call(c_pad, w_ada, b_ada):
    depth, _, ncol = w_ada.shape
    return pl.pallas_call(
        _ada_kernel,
        grid=(depth, ncol // ADA_TN),
        in_specs=[
            pl.BlockSpec((8, D), lambda l, j: (0, 0)),
            pl.BlockSpec((1, D, ADA_TN), lambda l, j: (l, 0, j)),
            pl.BlockSpec((1, 1, ADA_TN), lambda l, j: (l, 0, j)),
        ],
        out_specs=pl.BlockSpec((1, 8, ADA_TN), lambda l, j: (l, 0, j)),
        out_shape=jax.ShapeDtypeStruct((depth, 8, ncol), F32),
        compiler_params=pltpu.CompilerParams(vmem_limit_bytes=VMEM_LIMIT),
        name="ada_mod",
    )(c_pad, w_ada, b_ada.reshape(depth, 1, ncol))


def _rope_kernel(pos_ref, freq_ref, cos_ref, sin_ref):
    ang = pos_ref[...].astype(F32) * freq_ref[...]
    cos_ref[...] = jnp.cos(ang)
    sin_ref[...] = jnp.sin(ang)


def _rope_call(pos_col, inv_freq):
    n = pos_col.shape[0]
    half = inv_freq.shape[1]
    return pl.pallas_call(
        _rope_kernel,
        grid=(n // ROPE_TM,),
        in_specs=[
            pl.BlockSpec((ROPE_TM, 1), lambda i: (i, 0)),
            pl.BlockSpec((1, half), lambda i: (0, 0)),
        ],
        out_specs=[
            pl.BlockSpec((ROPE_TM, half), lambda i: (i, 0)),
            pl.BlockSpec((ROPE_TM, half), lambda i: (i, 0)),
        ],
        out_shape=[jax.ShapeDtypeStruct((n, half), F32)] * 2,
        name="rope_tables",
    )(pos_col, inv_freq)


def _inproj_kernel(x_ref, shift_ref, scale_ref, nw_ref, cos_ref, sin_ref, w_hbm, o_ref, w_vmem, sem):
    @pl.when(pl.program_id(0) == 0)
    def _():
        cp = pltpu.make_async_copy(w_hbm, w_vmem, sem)
        cp.start()
        cp.wait()

    x = x_ref[...]
    ms = jnp.mean(x * x, axis=-1, keepdims=True)
    h = x * lax.rsqrt(ms + NORM_EPS) * nw_ref[...]
    h = (h * (1.0 + scale_ref[0]) + shift_ref[0]).astype(BF16)
    cos = cos_ref[...]
    sin = sin_ref[...]
    half = HD // 2

    for j in range(PROJ // INP_TN):
        c0 = j * INP_TN
        acc = jnp.dot(h, w_vmem[:, c0:c0 + INP_TN], preferred_element_type=F32)
        sec = c0 // RW
        if sec in (0, 1):
            scale = 1.0 if sec == 0 else HD ** -0.5
            for hh in range(INP_TN // HD):
                a = acc[:, hh * HD:hh * HD + half]
                b = acc[:, hh * HD + half:(hh + 1) * HD]
                o_ref[:, c0 + hh * HD:c0 + hh * HD + half] = ((a * cos - b * sin) * scale).astype(BF16)
                o_ref[:, c0 + hh * HD + half:c0 + (hh + 1) * HD] = ((b * cos + a * sin) * scale).astype(BF16)
        elif sec == 2:
            o_ref[:, c0:c0 + INP_TN] = acc.astype(BF16)
        elif sec == 3:
            o_ref[:, c0:c0 + INP_TN] = (acc * _sigmoid(acc)).astype(BF16)
        elif sec == 4:
            o_ref[:, c0:c0 + INP_TN] = _gelu_tanh(acc).astype(BF16)
        else:
            for gg in range(INP_TN // GD):
                t = _gelu_tanh(acc[:, gg * GD:(gg + 1) * GD])
                ms_g = jnp.mean(t * t, axis=-1, keepdims=True)
                o_ref[:, c0 + gg * GD:c0 + (gg + 1) * GD] = (t * lax.rsqrt(ms_g + NORM_EPS)).astype(BF16)


def _inproj_call(x2, shift, scale, nw, w_bf, cos, sin, seq):
    n = x2.shape[0]
    tiles_per_batch = seq // INP_TM
    bmap = lambda i: (i // tiles_per_batch, 0, 0)
    return pl.pallas_call(
        _inproj_kernel,
        grid=(n // INP_TM,),
        in_specs=[
            pl.BlockSpec((INP_TM, D), lambda i: (i, 0)),
            pl.BlockSpec((1, 1, D), bmap),
            pl.BlockSpec((1, 1, D), bmap),
            pl.BlockSpec((1, D), lambda i: (0, 0)),
            pl.BlockSpec((INP_TM, HD // 2), lambda i: (i, 0)),
            pl.BlockSpec((INP_TM, HD // 2), lambda i: (i, 0)),
            pl.BlockSpec(memory_space=pl.ANY),
        ],
        out_specs=pl.BlockSpec((INP_TM, PROJ), lambda i: (i, 0)),
        out_shape=jax.ShapeDtypeStruct((n, PROJ), BF16),
        scratch_shapes=[pltpu.VMEM((D, PROJ), BF16), pltpu.SemaphoreType.DMA(())],
        compiler_params=pltpu.CompilerParams(
            dimension_semantics=("arbitrary",), vmem_limit_bytes=BIG_VMEM_LIMIT),
        name="in_proj",
    )(x2, shift, scale, nw, cos, sin, w_bf)


def _mix_kernel(cd_ref, q_ref, k_ref, v_ref, g_ref, u_ref, vs_ref, dm_ref, qd_ref, kd_ref,
                rnw_ref, ws_ref, bs_ref, o_ref, state_ref):
    @pl.when(pl.program_id(1) == 0)
    def _():
        state_ref[...] = jnp.zeros_like(state_ref)

    row = lax.broadcasted_iota(jnp.int32, (CH, CH), 0)
    col = lax.broadcasted_iota(jnp.int32, (CH, CH), 1)
    causal = row >= col

    def chunk(c, carry):
        r0 = pl.multiple_of(c * CH, CH)
        rows = pl.ds(r0, CH)
        for h in range(NH):
            cols = slice(h * HD, (h + 1) * HD)
            q = q_ref[rows, cols]
            k = k_ref[rows, cols]
            v = v_ref[rows, cols]
            st = state_ref[h]
            s = lax.dot_general(q, k, (((1,), (1,)), ((), ())), preferred_element_type=F32)
            s = s * dm_ref[h]
            intra = jnp.dot(s.astype(BF16), v, preferred_element_type=F32)
            cross = jnp.dot(q, st.astype(BF16), preferred_element_type=F32) * qd_ref[:, h:h + 1]
            kdec = (k.astype(F32) * kd_ref[:, h:h + 1]).astype(BF16)
            upd = lax.dot_general(kdec, v, (((0,), (0,)), ((), ())), preferred_element_type=F32)
            state_ref[h] = st * cd_ref[h] + upd
            o = intra + cross
            ms = jnp.mean(o * o, axis=-1, keepdims=True)
            o = o * lax.rsqrt(ms + NORM_EPS) * rnw_ref[:, cols]
            o_ref[rows, cols] = (o * g_ref[rows, cols].astype(F32)).astype(BF16)
        for g in range(NG):
            cols = slice(g * GD, (g + 1) * GD)
            wm = jnp.where(causal, ws_ref[g], 0.0).astype(BF16)
            z = jnp.dot(wm, vs_ref[rows, cols], preferred_element_type=F32) + bs_ref[:, g:g + 1]
            o_ref[rows, RW + g * GD:RW + (g + 1) * GD] = (u_ref[rows, cols].astype(F32) * z).astype(BF16)
        return carry

    lax.fori_loop(0, MIX_ROWS // CH, chunk, 0)


def _mix_call(proj, chunk_decay, decay_mask, q_decay_t, k_decay_t, rnw, ws, bs_t, bsz, seq):
    n = proj.shape[0]
    steps = seq // MIX_ROWS

    def colblk(jb):
        return pl.BlockSpec((MIX_ROWS, RW), lambda b, s, cd: (b * steps + s, jb))

    const2 = lambda b, s, cd: (0, 0)
    const3 = lambda b, s, cd: (0, 0, 0)
    grid_spec = pltpu.PrefetchScalarGridSpec(
        num_scalar_prefetch=1,
        grid=(bsz, steps),
        in_specs=[colblk(0), colblk(1), colblk(2), colblk(3), colblk(4), colblk(5),
                  pl.BlockSpec((NH, CH, CH), const3),
                  pl.BlockSpec((CH, NH), const2),
                  pl.BlockSpec((CH, NH), const2),
                  pl.BlockSpec((1, RW), const2),
                  pl.BlockSpec((NG, CH, CH), const3),
                  pl.BlockSpec((CH, NG), const2)],
        out_specs=pl.BlockSpec((MIX_ROWS, D), lambda b, s, cd: (b * steps + s, 0)),
        scratch_shapes=[pltpu.VMEM((NH, HD, HD), F32)],
    )
    return pl.pallas_call(
        _mix_kernel,
        grid_spec=grid_spec,
        out_shape=jax.ShapeDtypeStruct((n, D), BF16),
        compiler_params=pltpu.CompilerParams(
            dimension_semantics=("arbitrary", "arbitrary"), vmem_limit_bytes=VMEM_LIMIT),
        name="retention_spatial_mix",
    )(chunk_decay, proj, proj, proj, proj, proj, proj, decay_mask, q_decay_t, k_decay_t, rnw, ws, bs_t)


def _outproj_kernel(mix_ref, x_ref, w_ref, gate_ref, shift_ref, scale_ref, nw_ref, wr_ref, rb_ref,
                    x1_ref, slot_ref, gw_ref, be_ref, na_ref, xg_hbm,
                    carry_ref, cur_ref, bev_ref, nfree_ref, hbuf, slot_v, slot_s, tail_v, tail_s, zbuf,
                    sem, ssem, zsem):
    i = pl.program_id(0)
    last = pl.num_programs(0) - 1
    par = i % 2
    nblk = xg_hbm.shape[0] // (EXP_BM * PK_SUB)

    @pl.when(i == 0)
    def _():
        carry_ref[...] = jnp.zeros_like(carry_ref)
        cur_ref[...] = jnp.zeros_like(cur_ref)
        nfree_ref[...] = jnp.zeros_like(nfree_ref)
        bev_ref[...] = jnp.full(bev_ref.shape, NE, F32)

    y = jnp.dot(mix_ref[...], w_ref[...], preferred_element_type=F32)
    x1 = x_ref[...] + gate_ref[0] * y
    x1_ref[...] = x1
    ms = jnp.mean(x1 * x1, axis=-1, keepdims=True)
    h2 = x1 * lax.rsqrt(ms + NORM_EPS) * nw_ref[...]
    h2 = h2 * (1.0 + scale_ref[0]) + shift_ref[0]

    def scatter_waits(parity):
        def body(r, carry):
            for k in range(2):
                _tile_copy(hbuf.at[parity], r, xg_hbm, 0, 1, sem.at[parity]).wait()
            return carry

        lax.fori_loop(0, OUT_TM, body, 0)

    @pl.when(i >= 2)
    def _():
        scatter_waits(par)

    _pack_rows(h2, hbuf.at[par], OUT_TM)

    nt = (((1,), (1,)), ((), ()))
    wr = wr_ref[...]
    w_hi = wr.astype(BF16)
    w_lo = (wr - w_hi.astype(F32)).astype(BF16)
    h_hi = h2.astype(BF16)
    h_lo = (h2 - h_hi.astype(F32)).astype(BF16)
    p_hi = lax.dot_general(jnp.concatenate([w_hi, w_lo], axis=0), h_hi, nt, preferred_element_type=F32)
    p_lo = lax.dot_general(w_hi, h_lo, nt, preferred_element_type=F32)
    logits = p_hi[:NE] + p_hi[NE:] + p_lo
    scores = _sigmoid(logits)
    biased = scores + rb_ref[...]
    a = [biased[m * NEG:(m + 1) * NEG] for m in range(EPG)]
    sc = [scores[m * NEG:(m + 1) * NEG] for m in range(EPG)]
    p, q = jnp.maximum(a[0], a[1]), jnp.minimum(a[0], a[1])
    r, s = jnp.maximum(a[2], a[3]), jnp.minimum(a[2], a[3])
    group_score = jnp.maximum(p, r) + jnp.maximum(jnp.minimum(p, r), jnp.maximum(q, s))
    gidx = lax.broadcasted_iota(jnp.int32, group_score.shape, 0)
    gmax = jnp.max(group_score, axis=0, keepdims=True)
    g_sel = jnp.min(jnp.where(group_score == gmax, gidx, NEG), axis=0, keepdims=True)
    pick = gidx == g_sel
    v = [jnp.sum(jnp.where(pick, a[m], 0.0), axis=0, keepdims=True) for m in range(EPG)]
    u = [jnp.sum(jnp.where(pick, sc[m], 0.0), axis=0, keepdims=True) for m in range(EPG)]

    def first_argmax(vals):
        m = functools.reduce(jnp.maximum, vals)
        idx = jnp.full(m.shape, EPG - 1, jnp.int32)
        for t in range(EPG - 2, -1, -1):
            idx = jnp.where(vals[t] == m, t, idx)
        return idx

    def select(vals, idx):
        out = vals[EPG - 1]
        for t in range(EPG - 2, -1, -1):
            out = jnp.where(idx == t, vals[t], out)
        return out

    i1 = first_argmax(v)
    v2 = [jnp.where(i1 == t, -jnp.inf, v[t]) for t in range(EPG)]
    i2 = first_argmax(v2)
    s1 = select(u, i1)
    s2 = select(u, i2)
    denom = s1 + s2
    e1 = g_sel * EPG + i1
    e2 = g_sel * EPG + i2
    gw_ref[0:1, :] = s1 / denom
    gw_ref[1:2, :] = s2 / denom

    tm = e1.shape[1]
    eidx = lax.broadcasted_iota(jnp.int32, (NE, tm), 0)
    oh1 = eidx == e1
    oh2 = eidx == e2
    oh = jnp.where(oh1 | oh2, 1.0, 0.0)
    tr = lax.broadcasted_iota(jnp.int32, (tm, tm), 0)
    tc = lax.broadcasted_iota(jnp.int32, (tm, tm), 1)
    earlier = jnp.where(tr < tc, 1.0, 0.0).astype(BF16)
    carry = carry_ref[...]
    rank = jnp.dot(oh.astype(BF16), earlier, preferred_element_type=F32) + carry
    total = carry + jnp.sum(oh, axis=1, keepdims=True)
    carry_ref[...] = total

    inv_bm = 1.0 / EXP_BM
    blocks_before = jnp.floor((carry + (EXP_BM - 1)) * inv_bm)
    need = jnp.floor((total + (EXP_BM - 1)) * inv_bm) - blocks_before
    er = lax.broadcasted_iota(jnp.int32, (NE, NE), 0)
    ec = lax.broadcasted_iota(jnp.int32, (NE, NE), 1)
    lower = jnp.where(ec < er, 1.0, 0.0).astype(BF16)
    need_b = jnp.broadcast_to(need, (NE, LANES)).astype(BF16)
    nfree = nfree_ref[...]
    base = nfree + jnp.dot(lower, need_b, preferred_element_type=F32)[:, 0:1]
    cur = cur_ref[...]
    cur_ref[...] = jnp.where(need > 0, base + need - 1.0, cur)
    nfree_new = nfree + jnp.sum(need, axis=0, keepdims=True)
    nfree_ref[...] = nfree_new
    blk = lax.broadcasted_iota(jnp.int32, (NE, bev_ref.shape[1]), 1).astype(F32)
    hit = jnp.logical_and(blk >= base, blk < base + need)
    owner = jnp.sum(jnp.where(hit, eidx[:, 0:1].astype(F32), 0.0), axis=0, keepdims=True)
    taken = jnp.sum(jnp.where(hit, 1.0, 0.0), axis=0, keepdims=True)
    bev = jnp.where(taken > 0, owner, bev_ref[...])
    bev_ref[...] = bev
    be_ref[...] = bev.astype(jnp.int32)
    na_ref[...] = nfree_new.astype(jnp.int32)

    jblk = jnp.floor(rank * inv_bm)
    block_id = jnp.where(jblk < blocks_before, cur, base + (jblk - blocks_before))
    slot_all = block_id * EXP_BM + (rank - jblk * EXP_BM)
    slot1 = jnp.sum(jnp.where(oh1, slot_all, 0.0), axis=0, keepdims=True).astype(jnp.int32)
    slot2 = jnp.sum(jnp.where(oh2, slot_all, 0.0), axis=0, keepdims=True).astype(jnp.int32)
    slot_ref[0, 0:1, :] = slot1
    slot_ref[0, 1:2, :] = slot2
    slot_v[0:1, :] = slot1
    slot_v[1:2, :] = slot2
    to_smem = pltpu.make_async_copy(slot_v, slot_s, ssem)
    to_smem.start()
    to_smem.wait()

    def scatter(r, carry_):
        for k in range(2):
            _tile_copy(hbuf.at[par], r, xg_hbm, slot_s[k, r], 1, sem.at[par]).start(priority=k)
        return carry_

    lax.fori_loop(0, OUT_TM, scatter, 0)

    @pl.when(i == last)
    def _():
        scatter_waits(par)

        @pl.when(last >= 1)
        def _():
            scatter_waits(1 - par)

        lane_e = lax.broadcasted_iota(jnp.int32, (NE, LANES), 1)
        as_row = lambda col: jnp.sum(jnp.where(lane_e == eidx[:, 0:1], col, 0.0), axis=0, keepdims=True)
        lane1 = lax.broadcasted_iota(jnp.int32, (1, LANES), 1)
        cur_row = jnp.where(lane1 == NE, nfree_new, as_row(cur_ref[...]))
        tail_v[0:1, :] = cur_row.astype(jnp.int32)
        tail_v[1:2, :] = as_row(total).astype(jnp.int32)
        cp = pltpu.make_async_copy(tail_v, tail_s, ssem)
        cp.start()
        cp.wait()
        zbuf[...] = jnp.zeros_like(zbuf)

        def fills(wait):
            def per_expert(e, carry_):
                used = tail_s[1, e] & (EXP_BM - 1)
                first = tail_s[0, e] * EXP_BM + used
                npad = jnp.where(used > 0, EXP_BM - used, 0)

                def chunk(size):
                    off = npad & ~(2 * size - 1)

                    @pl.when((npad & size) != 0)
                    def _():
                        cpz = _tile_copy(zbuf, 0, xg_hbm, first + off, size, zsem)
                        cpz.wait() if wait else cpz.start()

                _pad_chunks(chunk)
                return carry_

            lax.fori_loop(0, NE, per_expert, 0)

            def per_block(b, carry_):
                cpz = _tile_copy(zbuf, 0, xg_hbm, b * EXP_BM, EXP_BM, zsem)
                cpz.wait() if wait else cpz.start()
                return carry_

            lax.fori_loop(tail_s[0, NE], nblk, per_block, 0)

        fills(False)
        fills(True)


def _outproj_call(mix, x2, w_bf, gate, shift, scale, nw, wr_t, rb_col, seq, nb):
    n = x2.shape[0]
    tiles_per_batch = seq // OUT_TM
    bmap = lambda i: (i // tiles_per_batch, 0, 0)
    row = lambda i: (i, 0)
    lane = lambda i: (0, i)
    const = lambda i: (0, 0)
    table = lambda i: (i, 0, 0)
    return pl.pallas_call(
        _outproj_kernel,
        grid=(n // OUT_TM,),
        in_specs=[
            pl.BlockSpec((OUT_TM, D), row),
            pl.BlockSpec((OUT_TM, D), row),
            pl.BlockSpec((D, D), const),
            pl.BlockSpec((1, 1, D), bmap),
            pl.BlockSpec((1, 1, D), bmap),
            pl.BlockSpec((1, 1, D), bmap),
            pl.BlockSpec((1, D), const),
            pl.BlockSpec((NE, D), const),
            pl.BlockSpec((NE, 1), const),
        ],
        out_specs=[
            pl.BlockSpec((OUT_TM, D), row),
            pl.BlockSpec((1, 2, OUT_TM), table),
            pl.BlockSpec((2, OUT_TM), lane),
            pl.BlockSpec((1, BLK_LANES), const),
            pl.BlockSpec((1, 1), const),
            pl.BlockSpec(memory_space=pl.ANY),
        ],
        out_shape=[
            jax.ShapeDtypeStruct((n, D), F32),
            jax.ShapeDtypeStruct((n // OUT_TM, 2, OUT_TM), jnp.int32),
            jax.ShapeDtypeStruct((2, n), F32),
            jax.ShapeDtypeStruct((1, BLK_LANES), jnp.int32),
            jax.ShapeDtypeStruct((1, 1), jnp.int32),
            jax.ShapeDtypeStruct((nb * EXP_BM * PK_SUB, LANES), jnp.uint32),
        ],
        scratch_shapes=[
            pltpu.VMEM((NE, 1), F32),
            pltpu.VMEM((NE, 1), F32),
            pltpu.VMEM((1, BLK_LANES), F32),
            pltpu.VMEM((1, 1), F32),
            pltpu.VMEM((2, OUT_TM * PK_SUB, LANES), jnp.uint32),
            pltpu.VMEM((2, OUT_TM), jnp.int32),
            pltpu.SMEM((2, OUT_TM), jnp.int32),
            pltpu.VMEM((2, LANES), jnp.int32),
            pltpu.SMEM((2, LANES), jnp.int32),
            pltpu.VMEM((EXP_BM * PK_SUB, LANES), jnp.uint32),
            pltpu.SemaphoreType.DMA((2,)),
            pltpu.SemaphoreType.DMA(()),
            pltpu.SemaphoreType.DMA(()),
        ],
        compiler_params=pltpu.CompilerParams(
            dimension_semantics=("arbitrary",), vmem_limit_bytes=BIG_VMEM_LIMIT),
        name="out_proj_router",
    )(mix, x2, w_bf, gate, shift, scale, nw, wr_t, rb_col)


def _pack_rows(v, o_ref, rows):
    hi = lax.bitcast_convert_type(v[:, :D // 2].astype(BF16).astype(F32), jnp.uint32)
    lo = lax.bitcast_convert_type(v[:, D // 2:].astype(BF16).astype(F32), jnp.uint32)
    packed = hi | (lo >> 16)
    for j in range(PK_SUB):
        o_ref[pl.ds(j, rows, stride=PK_SUB), :] = packed[:, j * LANES:(j + 1) * LANES]


def _unpack_halves(x_ref, rows):
    his, los = [], []
    for j in range(PK_SUB):
        p = x_ref[pl.ds(j, rows, stride=PK_SUB), :]
        his.append(lax.bitcast_convert_type(p & jnp.uint32(0xFFFF0000), F32))
        los.append(lax.bitcast_convert_type(p << 16, F32))
    return his, los


def _tile_rows(row, nrows):
    start = row * PK_SUB
    if not isinstance(row, int):
        start = pl.multiple_of(start, PK_SUB)
    return pl.ds(start, nrows * PK_SUB)


def _tile_copy(src, src_row, dst, dst_row, nrows, sem):
    return pltpu.make_async_copy(src.at[_tile_rows(src_row, nrows), :], dst.at[_tile_rows(dst_row, nrows), :], sem)


def _pad_chunks(fn):
    size = EXP_BM // 2
    while size >= 1:
        fn(size)
        size //= 2


def _expert_kernel(order_ref, be_ref, na_ref, x_ref, w1_ref, w3_ref, w2_ref, o_ref, w1b, w3b, w2b):
    i = pl.program_id(0)
    active = i < na_ref[0]
    changed = jnp.logical_or(i == 0, be_ref[i] != be_ref[jnp.maximum(i - 1, 0)])

    @pl.when(jnp.logical_and(active, changed))
    def _():
        w1b[...] = w1_ref[0, 0].astype(BF16)
        w3b[...] = w3_ref[0, 0].astype(BF16)
        w2b[...] = w2_ref[0, 0].astype(BF16)

    @pl.when(active)
    def _():
        rows = EXP_BM // EXP_SUB
        for s in range(EXP_SUB):
            part = pl.ds(s * rows * PK_SUB, rows * PK_SUB)
            his, los = _unpack_halves(x_ref.at[part], rows)
            xb = jnp.concatenate([c.astype(BF16) for c in his + los], axis=1)
            h1 = jnp.dot(xb, w1b[...], preferred_element_type=F32)
            h3 = jnp.dot(xb, w3b[...], preferred_element_type=F32)
            act = (h1 * _sigmoid(h1) * h3).astype(BF16)
            _pack_rows(jnp.dot(act, w2b[...], preferred_element_type=F32), o_ref.at[part], rows)

    @pl.when(jnp.logical_not(active))
    def _():
        o_ref[...] = jnp.zeros_like(o_ref)


def _expert_call(order, block_expert, n_active, xg, w1, w3, w2, layer):
    nb = order.shape[0]
    live = lambda i, od, be, na: (od[jnp.minimum(i, na[0] - 1)], 0)
    wmap = lambda i, od, be, na: (layer, be[jnp.minimum(i, na[0] - 1)], 0, 0)
    grid_spec = pltpu.PrefetchScalarGridSpec(
        num_scalar_prefetch=3,
        grid=(nb,),
        in_specs=[
            pl.BlockSpec((EXP_BM * PK_SUB, LANES), live),
            pl.BlockSpec((1, 1, D, FF), wmap),
            pl.BlockSpec((1, 1, D, FF), wmap),
            pl.BlockSpec((1, 1, FF, D), wmap),
        ],
        out_specs=pl.BlockSpec((EXP_BM * PK_SUB, LANES), lambda i, od, be, na: (od[i], 0)),
        scratch_shapes=[pltpu.VMEM((D, FF), BF16), pltpu.VMEM((D, FF), BF16), pltpu.VMEM((FF, D), BF16)],
    )
    return pl.pallas_call(
        _expert_kernel,
        grid_spec=grid_spec,
        out_shape=jax.ShapeDtypeStruct((nb * EXP_BM * PK_SUB, LANES), jnp.uint32),
        compiler_params=pltpu.CompilerParams(
            dimension_semantics=("arbitrary",), vmem_limit_bytes=VMEM_LIMIT),
        name="expert_blocks",
    )(order, block_expert, n_active, xg, w1, w3, w2)


def _combine_kernel(slot_ref, x_ref, gate_ref, gw_ref, fw_ref, y_hbm, o_ref, ybuf, sem, *, final_norm):
    base = (pl.program_id(0) % (OUT_TM // CMB_TM)) * CMB_TM

    def issue(r, carry):
        for k in range(2):
            _tile_copy(y_hbm, slot_ref[0, k, base + r], ybuf.at[k], r, 1, sem).start(priority=k)
        return carry

    lax.fori_loop(0, CMB_TM, issue, 0)

    def drain(r, carry):
        for k in range(2):
            _tile_copy(y_hbm, 0, ybuf.at[k], r, 1, sem).wait()
        return carry

    lax.fori_loop(0, CMB_TM, drain, 0)
    gw = gw_ref[...]
    h0, l0 = _unpack_halves(ybuf.at[0], CMB_TM)
    h1, l1 = _unpack_halves(ybuf.at[1], CMB_TM)
    g0, g1 = gw[:, 0:1], gw[:, 1:2]
    moe = jnp.concatenate([g0 * a + g1 * b for a, b in zip(h0 + l0, h1 + l1)], axis=1)
    x2 = x_ref[...] + gate_ref[0] * moe
    if final_norm:
        ms = jnp.mean(x2 * x2, axis=-1, keepdims=True)
        x2 = x2 * lax.rsqrt(ms + NORM_EPS) * fw_ref[...]
    o_ref[...] = x2


def _combine_call(slots, x1, gate, gw_col, fw, yg, seq, final_norm):
    n = x1.shape[0]
    tiles_per_batch = seq // CMB_TM
    per_table = OUT_TM // CMB_TM
    return pl.pallas_call(
        functools.partial(_combine_kernel, final_norm=final_norm),
        grid=(n // CMB_TM,),
        in_specs=[
            pl.BlockSpec((1, 2, OUT_TM), lambda i: (i // per_table, 0, 0), memory_space=pltpu.SMEM),
            pl.BlockSpec((CMB_TM, D), lambda i: (i, 0)),
            pl.BlockSpec((1, 1, D), lambda i: (i // tiles_per_batch, 0, 0)),
            pl.BlockSpec((CMB_TM, 2), lambda i: (i, 0)),
            pl.BlockSpec((1, D), lambda i: (0, 0)),
            pl.BlockSpec(memory_space=pl.ANY),
        ],
        out_specs=pl.BlockSpec((CMB_TM, D), lambda i: (i, 0)),
        out_shape=jax.ShapeDtypeStruct((n, D), F32),
        scratch_shapes=[pltpu.VMEM((2, CMB_TM * PK_SUB, LANES), jnp.uint32), pltpu.SemaphoreType.DMA(())],
        compiler_params=pltpu.CompilerParams(
            dimension_semantics=("arbitrary",), vmem_limit_bytes=VMEM_LIMIT),
        name="moe_combine",
    )(slots, x1, gate, gw_col, fw, yg)


def _decay_tables():
    log_gamma = jnp.log1p(-jnp.exp2(-5.0 - jnp.arange(NH, dtype=F32)))
    pos = jnp.arange(CH, dtype=F32)
    diff = pos[:, None] - pos[None, :]
    mask = jnp.where(diff >= 0, jnp.exp(log_gamma[:, None, None] * jnp.maximum(diff, 0.0)), 0.0)
    q_decay = jnp.exp(log_gamma[:, None] * (pos + 1.0))
    k_decay = jnp.exp(log_gamma[:, None] * (CH - 1.0 - pos))
    chunk_decay = jnp.exp(log_gamma * CH)
    return mask.astype(F32), q_decay.T, k_decay.T, chunk_decay


def kernel(x, c, positions, w_ada, b_ada, norm1_w, norm2_w, w_in, w_out, ret_norm_w, sg_w_s, sg_b_s,
           w_router, router_bias, w1, w3, w2, final_norm_w):
    bsz, seq, d = x.shape
    n = bsz * seq
    assert d == D and seq % INP_TM == 0 and seq % MIX_ROWS == 0 and n % CMB_TM == 0

    c_pad = jnp.zeros((8, D), F32).at[:bsz].set(c)
    mod = _ada_call(c_pad, w_ada, b_ada)[:, :bsz]
    mod = mod.reshape(N_LAYERS, bsz, N_MOD, 1, D)

    half = HD // 2
    inv_freq = (ROPE_THETA ** (-jnp.arange(half, dtype=F32) / half)).reshape(1, half)
    cos, sin = _rope_call(positions.reshape(n, 1), inv_freq)
    decay_mask, q_decay_t, k_decay_t, chunk_decay = _decay_tables()

    perm = (jnp.arange(NE) % NEG) * EPG + jnp.arange(NE) // NEG
    wr_t = w_router.T[perm]
    rb_col = router_bias[perm].reshape(NE, 1)

    nb = n * 2 // EXP_BM + NE
    assert nb <= BLK_LANES and n % OUT_TM == 0
    xs = x.reshape(n, D)
    for l in range(N_LAYERS):
        shift1, scale1, gate1, shift2, scale2, gate2 = [mod[l, :, t] for t in range(N_MOD)]
        proj = _inproj_call(xs, shift1, scale1, norm1_w[l].reshape(1, D), w_in[l].astype(BF16), cos, sin, seq)
        mix = _mix_call(proj, chunk_decay, decay_mask, q_decay_t, k_decay_t, ret_norm_w[l].reshape(1, RW),
                        sg_w_s[l], sg_b_s[l].T, bsz, seq)
        x1, slots, gw, owner, n_active, xg = _outproj_call(
            mix, xs, w_out[l].astype(BF16), gate1, shift2, scale2, norm2_w[l].reshape(1, D), wr_t, rb_col, seq, nb)

        owner = owner.reshape(BLK_LANES)
        ids = jnp.arange(BLK_LANES, dtype=jnp.int32)
        ahead = (owner[None, :] < owner[:, None]) | ((owner[None, :] == owner[:, None]) & (ids[None, :] < ids[:, None]))
        pos = jnp.sum(ahead, axis=1)
        at = pos[None, :] == ids[:, None]
        order = jnp.sum(jnp.where(at, ids[None, :], 0), axis=1)[:nb].astype(jnp.int32)
        block_expert = jnp.minimum(jnp.sum(jnp.where(at, owner[None, :], 0), axis=1)[:nb], NE - 1).astype(jnp.int32)

        yg = _expert_call(order, block_expert, n_active.reshape(1), xg, w1, w3, w2, l)
        xs = _combine_call(slots, x1, gate2, gw.T, final_norm_w.reshape(1, D), yg, seq,
                           final_norm=(l == N_LAYERS - 1))
    return xs.reshape(bsz, seq, D)
```

```python
import functools

import jax
import jax.numpy as jnp
from jax import lax
from jax.experimental import pallas as pl
from jax.experimental.pallas import tpu as pltpu

F32 = jnp.float32
BF16 = jnp.bfloat16
HIGHEST = lax.Precision.HIGHEST

D = 2048
N_LAYERS = 2
RW = D // 2
NH = 4
HD = RW // NH
CH = 128
ROPE_THETA = 10000.0
SGW = D - RW
NG = 8
GD = SGW // NG
PROJ = 4 * RW + 2 * SGW
NE = 32
NEG = 8
EPG = NE // NEG
FF = 512
N_MOD = 6
NORM_EPS = 1e-6

VMEM_LIMIT = 48 * 1024 * 1024
BIG_VMEM_LIMIT = 56 * 1024 * 1024

ADA_TN = 1024
ROPE_TM = 2048
INP_TM = 256
INP_TN = 512
MIX_ROWS = 512
OUT_TM = 512
RUN_CHUNK = 16
EXP_BM = 512
EXP_SUB = 2
CMB_TM = 512
BLK_LANES = 256

LANES = 128
PK_SUB = D // 2 // LANES


def _sigmoid(v):
    return 1.0 / (1.0 + jnp.exp(-v))


def _gelu_tanh(v):
    return 0.5 * v * (1.0 + jnp.tanh(0.7978845608028654 * (v + 0.044715 * (v * v * v))))


def _ada_kernel(c_ref, w_ref, b_ref, o_ref):
    c = c_ref[...]
    ca = c * _sigmoid(c)
    o_ref[0] = jnp.dot(ca, w_ref[0], precision=HIGHEST, preferred_element_type=F32) + b_ref[0]


def _ada_call(c_pad, w_ada, b_ada):
    depth, _, ncol = w_ada.shape
    return pl.pallas_call(
        _ada_kernel,
        grid=(depth, ncol // ADA_TN),
        in_specs=[
            pl.BlockSpec((8, D), lambda l, j: (0, 0)),
            pl.BlockSpec((1, D, ADA_TN), lambda l, j: (l, 0, j)),
            pl.BlockSpec((1, 1, ADA_TN), lambda l, j: (l, 0, j)),
        ],
        out_specs=pl.BlockSpec((1, 8, ADA_TN), lambda l, j: (l, 0, j)),
        out_shape=jax.ShapeDtypeStruct((depth, 8, ncol), F32),
        compiler_params=pltpu.CompilerParams(vmem_limit_bytes=VMEM_LIMIT),
        name="ada_mod",
    )(c_pad, w_ada, b_ada.reshape(depth, 1, ncol))


def _rope_kernel(pos_ref, freq_ref, cos_ref, sin_ref):
    ang = pos_ref[...].astype(F32) * freq_ref[...]
    cos_ref[...] = jnp.cos(ang)
    sin_ref[...] = jnp.sin(ang)


def _rope_call(pos_col, inv_freq):
    n = pos_col.shape[0]
    half = inv_freq.shape[1]
    return pl.pallas_call(
        _rope_kernel,
        grid=(n // ROPE_TM,),
        in_specs=[
            pl.BlockSpec((ROPE_TM, 1), lambda i: (i, 0)),
            pl.BlockSpec((1, half), lambda i: (0, 0)),
        ],
        out_specs=[
            pl.BlockSpec((ROPE_TM, half), lambda i: (i, 0)),
            pl.BlockSpec((ROPE_TM, half), lambda i: (i, 0)),
        ],
        out_shape=[jax.ShapeDtypeStruct((n, half), F32)] * 2,
        name="rope_tables",
    )(pos_col, inv_freq)


def _inproj_kernel(x_ref, shift_ref, scale_ref, nw_ref, cos_ref, sin_ref, w_hbm, o_ref, w_vmem, sem):
    @pl.when(pl.program_id(0) == 0)
    def _():
        cp = pltpu.make_async_copy(w_hbm, w_vmem, sem)
        cp.start()
        cp.wait()

    x = x_ref[...]
    ms = jnp.mean(x * x, axis=-1, keepdims=True)
    h = x * lax.rsqrt(ms + NORM_EPS) * nw_ref[...]
    h = (h * (1.0 + scale_ref[0]) + shift_ref[0]).astype(BF16)
    cos = cos_ref[...]
    sin = sin_ref[...]
    half = HD // 2

    for j in range(PROJ // INP_TN):
        c0 = j * INP_TN
        acc = jnp.dot(h, w_vmem[:, c0:c0 + INP_TN], preferred_element_type=F32)
        sec = c0 // RW
        if sec in (0, 1):
            scale = 1.0 if sec == 0 else HD ** -0.5
            for hh in range(INP_TN // HD):
                a = acc[:, hh * HD:hh * HD + half]
                b = acc[:, hh * HD + half:(hh + 1) * HD]
                o_ref[:, c0 + hh * HD:c0 + hh * HD + half] = ((a * cos - b * sin) * scale).astype(BF16)
                o_ref[:, c0 + hh * HD + half:c0 + (hh + 1) * HD] = ((b * cos + a * sin) * scale).astype(BF16)
        elif sec == 2:
            o_ref[:, c0:c0 + INP_TN] = acc.astype(BF16)
        elif sec == 3:
            o_ref[:, c0:c0 + INP_TN] = (acc * _sigmoid(acc)).astype(BF16)
        elif sec == 4:
            o_ref[:, c0:c0 + INP_TN] = _gelu_tanh(acc).astype(BF16)
        else:
            for gg in range(INP_TN // GD):
                t = _gelu_tanh(acc[:, gg * GD:(gg + 1) * GD])
                ms_g = jnp.mean(t * t, axis=-1, keepdims=True)
                o_ref[:, c0 + gg * GD:c0 + (gg + 1) * GD] = (t * lax.rsqrt(ms_g + NORM_EPS)).astype(BF16)


def _inproj_call(x2, shift, scale, nw, w_bf, cos, sin, seq):
    n = x2.shape[0]
    tiles_per_batch = seq // INP_TM
    bmap = lambda i: (i // tiles_per_batch, 0, 0)
    return pl.pallas_call(
        _inproj_kernel,
        grid=(n // INP_TM,),
        in_specs=[
            pl.BlockSpec((INP_TM, D), lambda i: (i, 0)),
            pl.BlockSpec((1, 1, D), bmap),
            pl.BlockSpec((1, 1, D), bmap),
            pl.BlockSpec((1, D), lambda i: (0, 0)),
            pl.BlockSpec((INP_TM, HD // 2), lambda i: (i, 0)),
            pl.BlockSpec((INP_TM, HD // 2), lambda i: (i, 0)),
            pl.BlockSpec(memory_space=pl.ANY),
        ],
        out_specs=pl.BlockSpec((INP_TM, PROJ), lambda i: (i, 0)),
        out_shape=jax.ShapeDtypeStruct((n, PROJ), BF16),
        scratch_shapes=[pltpu.VMEM((D, PROJ), BF16), pltpu.SemaphoreType.DMA(())],
        compiler_params=pltpu.CompilerParams(
            dimension_semantics=("arbitrary",), vmem_limit_bytes=BIG_VMEM_LIMIT),
        name="in_proj",
    )(x2, shift, scale, nw, cos, sin, w_bf)


def _mix_kernel(cd_ref, q_ref, k_ref, v_ref, g_ref, u_ref, vs_ref, dm_ref, qd_ref, kd_ref,
                rnw_ref, ws_ref, bs_ref, o_ref, state_ref):
    @pl.when(pl.program_id(1) == 0)
    def _():
        state_ref[...] = jnp.zeros_like(state_ref)

    row = lax.broadcasted_iota(jnp.int32, (CH, CH), 0)
    col = lax.broadcasted_iota(jnp.int32, (CH, CH), 1)
    causal = row >= col

    def chunk(c, carry):
        r0 = pl.multiple_of(c * CH, CH)
        rows = pl.ds(r0, CH)
        for h in range(NH):
            cols = slice(h * HD, (h + 1) * HD)
            q = q_ref[rows, cols]
            k = k_ref[rows, cols]
            v = v_ref[rows, cols]
            st = state_ref[h]
            s = lax.dot_general(q, k, (((1,), (1,)), ((), ())), preferred_element_type=F32)
            s = s * dm_ref[h]
            intra = jnp.dot(s.astype(BF16), v, preferred_element_type=F32)
            cross = jnp.dot(q, st.astype(BF16), preferred_element_type=F32) * qd_ref[:, h:h + 1]
            kdec = (k.astype(F32) * kd_ref[:, h:h + 1]).astype(BF16)
            upd = lax.dot_general(kdec, v, (((0,), (0,)), ((), ())), preferred_element_type=F32)
            state_ref[h] = st * cd_ref[h] + upd
            o = intra + cross
            ms = jnp.mean(o * o, axis=-1, keepdims=True)
            o = o * lax.rsqrt(ms + NORM_EPS) * rnw_ref[:, cols]
            o_ref[rows, cols] = (o * g_ref[rows, cols].astype(F32)).astype(BF16)
        for g in range(NG):
            cols = slice(g * GD, (g + 1) * GD)
            wm = jnp.where(causal, ws_ref[g], 0.0).astype(BF16)
            z = jnp.dot(wm, vs_ref[rows, cols], preferred_element_type=F32) + bs_ref[:, g:g + 1]
            o_ref[rows, RW + g * GD:RW + (g + 1) * GD] = (u_ref[rows, cols].astype(F32) * z).astype(BF16)
        return carry

    lax.fori_loop(0, MIX_ROWS // CH, chunk, 0)


def _mix_call(proj, chunk_decay, decay_mask, q_decay_t, k_decay_t, rnw, ws, bs_t, bsz, seq):
    n = proj.shape[0]
    steps = seq // MIX_ROWS

    def colblk(jb):
        return pl.BlockSpec((MIX_ROWS, RW), lambda b, s, cd: (b * steps + s, jb))

    const2 = lambda b, s, cd: (0, 0)
    const3 = lambda b, s, cd: (0, 0, 0)
    grid_spec = pltpu.PrefetchScalarGridSpec(
        num_scalar_prefetch=1,
        grid=(bsz, steps),
        in_specs=[colblk(0), colblk(1), colblk(2), colblk(3), colblk(4), colblk(5),
                  pl.BlockSpec((NH, CH, CH), const3),
                  pl.BlockSpec((CH, NH), const2),
                  pl.BlockSpec((CH, NH), const2),
                  pl.BlockSpec((1, RW), const2),
                  pl.BlockSpec((NG, CH, CH), const3),
                  pl.BlockSpec((CH, NG), const2)],
        out_specs=pl.BlockSpec((MIX_ROWS, D), lambda b, s, cd: (b * steps + s, 0)),
        scratch_shapes=[pltpu.VMEM((NH, HD, HD), F32)],
    )
    return pl.pallas_call(
        _mix_kernel,
        grid_spec=grid_spec,
        out_shape=jax.ShapeDtypeStruct((n, D), BF16),
        compiler_params=pltpu.CompilerParams(
            dimension_semantics=("arbitrary", "arbitrary"), vmem_limit_bytes=VMEM_LIMIT),
        name="retention_spatial_mix",
    )(chunk_decay, proj, proj, proj, proj, proj, proj, decay_mask, q_decay_t, k_decay_t, rnw, ws, bs_t)


def _outproj_kernel(mix_ref, x_ref, gate_ref, shift_ref, scale_ref, nw_ref, wr_ref, rb_ref, w_hbm,
                    x1_ref, slot_ref, seg_ref, gw_ref, be_ref, na_ref, xg_hbm,
                    carry_ref, cur_ref, bev_ref, nfree_ref, w_vmem, hbuf, seg_v, seg_s, tail_v, tail_s, zbuf,
                    sem, ssem, zsem):
    i = pl.program_id(0)
    last = pl.num_programs(0) - 1
    nblk = xg_hbm.shape[0] // (EXP_BM * PK_SUB)

    @pl.when(i == 0)
    def _():
        cp = pltpu.make_async_copy(w_hbm, w_vmem, ssem)
        cp.start()
        carry_ref[...] = jnp.zeros_like(carry_ref)
        cur_ref[...] = jnp.zeros_like(cur_ref)
        nfree_ref[...] = jnp.zeros_like(nfree_ref)
        bev_ref[...] = jnp.full(bev_ref.shape, NE, F32)
        cp.wait()

    y = jnp.dot(mix_ref[...], w_vmem[...], preferred_element_type=F32)
    x1 = x_ref[...] + gate_ref[0] * y
    x1_ref[...] = x1
    ms = jnp.mean(x1 * x1, axis=-1, keepdims=True)
    h2 = x1 * lax.rsqrt(ms + NORM_EPS) * nw_ref[...]
    h2 = h2 * (1.0 + scale_ref[0]) + shift_ref[0]

    def scatter_wait():
        _tile_copy(hbuf, 0, xg_hbm, 0, 2 * OUT_TM, sem).wait()

    nt = (((1,), (1,)), ((), ()))
    wr = wr_ref[...]
    w_hi = wr.astype(BF16)
    w_lo = (wr - w_hi.astype(F32)).astype(BF16)
    h_hi = h2.astype(BF16)
    h_lo = (h2 - h_hi.astype(F32)).astype(BF16)
    p_hi = lax.dot_general(jnp.concatenate([w_hi, w_lo], axis=0), h_hi, nt, preferred_element_type=F32)
    p_lo = lax.dot_general(w_hi, h_lo, nt, preferred_element_type=F32)
    logits = p_hi[:NE] + p_hi[NE:] + p_lo
    scores = _sigmoid(logits)
    biased = scores + rb_ref[...]
    a = [biased[m * NEG:(m + 1) * NEG] for m in range(EPG)]
    sc = [scores[m * NEG:(m + 1) * NEG] for m in range(EPG)]
    p, q = jnp.maximum(a[0], a[1]), jnp.minimum(a[0], a[1])
    r, s = jnp.maximum(a[2], a[3]), jnp.minimum(a[2], a[3])
    group_score = jnp.maximum(p, r) + jnp.maximum(jnp.minimum(p, r), jnp.maximum(q, s))
    gidx = lax.broadcasted_iota(jnp.int32, group_score.shape, 0)
    gmax = jnp.max(group_score, axis=0, keepdims=True)
    g_sel = jnp.min(jnp.where(group_score == gmax, gidx, NEG), axis=0, keepdims=True)
    pick = gidx == g_sel
    v = [jnp.sum(jnp.where(pick, a[m], 0.0), axis=0, keepdims=True) for m in range(EPG)]
    u = [jnp.sum(jnp.where(pick, sc[m], 0.0), axis=0, keepdims=True) for m in range(EPG)]

    def first_argmax(vals):
        m = functools.reduce(jnp.maximum, vals)
        idx = jnp.full(m.shape, EPG - 1, jnp.int32)
        for t in range(EPG - 2, -1, -1):
            idx = jnp.where(vals[t] == m, t, idx)
        return idx

    def select(vals, idx):
        out = vals[EPG - 1]
        for t in range(EPG - 2, -1, -1):
            out = jnp.where(idx == t, vals[t], out)
        return out

    i1 = first_argmax(v)
    v2 = [jnp.where(i1 == t, -jnp.inf, v[t]) for t in range(EPG)]
    i2 = first_argmax(v2)
    s1 = select(u, i1)
    s2 = select(u, i2)
    denom = s1 + s2
    e1 = g_sel * EPG + i1
    e2 = g_sel * EPG + i2
    gw_ref[0:1, :] = s1 / denom
    gw_ref[1:2, :] = s2 / denom

    tm = e1.shape[1]
    eidx = lax.broadcasted_iota(jnp.int32, (NE, tm), 0)
    oh1 = eidx == e1
    oh2 = eidx == e2
    oh = jnp.where(oh1 | oh2, 1.0, 0.0)
    tr = lax.broadcasted_iota(jnp.int32, (tm, tm), 0)
    tc = lax.broadcasted_iota(jnp.int32, (tm, tm), 1)
    earlier = jnp.where(tr < tc, 1.0, 0.0).astype(BF16)
    carry = carry_ref[...]
    local = jnp.dot(oh.astype(BF16), earlier, preferred_element_type=F32)
    rank = local + carry
    count = jnp.sum(oh, axis=1, keepdims=True)
    total = carry + count
    carry_ref[...] = total

    inv_bm = 1.0 / EXP_BM
    blocks_before = jnp.floor((carry + (EXP_BM - 1)) * inv_bm)
    need = jnp.floor((total + (EXP_BM - 1)) * inv_bm) - blocks_before
    er = lax.broadcasted_iota(jnp.int32, (NE, NE), 0)
    ec = lax.broadcasted_iota(jnp.int32, (NE, NE), 1)
    lower = jnp.where(ec < er, 1.0, 0.0).astype(BF16)
    count_hi = jnp.floor(count * (1.0 / 16.0))
    lane_e = lax.broadcasted_iota(jnp.int32, (NE, LANES), 1)
    digits = jnp.where(lane_e == 0, need, jnp.where(lane_e == 1, count_hi, count - 16.0 * count_hi))
    sums = jnp.dot(lower, digits.astype(BF16), preferred_element_type=F32)
    nfree = nfree_ref[...]
    base = nfree + sums[:, 0:1]
    start = 16.0 * sums[:, 1:2] + sums[:, 2:3]
    cur = cur_ref[...]
    cur_ref[...] = jnp.where(need > 0, base + need - 1.0, cur)
    nfree_new = nfree + jnp.sum(need, axis=0, keepdims=True)
    nfree_ref[...] = nfree_new
    blk = lax.broadcasted_iota(jnp.int32, (NE, bev_ref.shape[1]), 1).astype(F32)
    hit = jnp.logical_and(blk >= base, blk < base + need)
    owner = jnp.sum(jnp.where(hit, eidx[:, 0:1].astype(F32), 0.0), axis=0, keepdims=True)
    taken = jnp.sum(jnp.where(hit, 1.0, 0.0), axis=0, keepdims=True)
    bev = jnp.where(taken > 0, owner, bev_ref[...])
    bev_ref[...] = bev
    be_ref[...] = bev.astype(jnp.int32)
    na_ref[...] = nfree_new.astype(jnp.int32)

    jblk = jnp.floor(rank * inv_bm)
    block_id = jnp.where(jblk < blocks_before, cur, base + (jblk - blocks_before))
    slot_all = block_id * EXP_BM + (rank - jblk * EXP_BM)
    slot1 = jnp.sum(jnp.where(oh1, slot_all, 0.0), axis=0, keepdims=True).astype(jnp.int32)
    slot2 = jnp.sum(jnp.where(oh2, slot_all, 0.0), axis=0, keepdims=True).astype(jnp.int32)
    slot_ref[0, 0:1, :] = slot1
    slot_ref[0, 1:2, :] = slot2

    sorted_all = start + local
    pos1 = jnp.sum(jnp.where(oh1, sorted_all, 0.0), axis=0, keepdims=True).astype(jnp.int32)
    pos2 = jnp.sum(jnp.where(oh2, sorted_all, 0.0), axis=0, keepdims=True).astype(jnp.int32)
    @pl.when(i >= 1)
    def _():
        scatter_wait()

    for part in range(2):
        prow = lax.broadcasted_iota(jnp.int32, (tm, tm), 0) + part * tm
        perm = jnp.where((prow == pos1) | (prow == pos2), 1.0, 0.0).astype(BF16)
        sorted_rows = jnp.dot(perm, h_hi, preferred_element_type=F32)
        _pack_rows(sorted_rows, hbuf.at[pl.ds(part * tm * PK_SUB, tm * PK_SUB)], tm)

    as_row = lambda col: jnp.sum(jnp.where(lane_e == eidx[:, 0:1], col, 0.0), axis=0, keepdims=True)
    j0 = jnp.floor(carry * inv_bm)
    in_block = carry - j0 * EXP_BM
    n_a = jnp.minimum(count, EXP_BM - in_block)
    block_of = lambda j: jnp.where(j < blocks_before, cur, base + (j - blocks_before))
    seg_rows = [start, n_a, block_of(j0) * EXP_BM + in_block, count - n_a, block_of(j0 + 1.0) * EXP_BM]
    seg_v[...] = jnp.concatenate(
        [as_row(col).astype(jnp.int32) for col in seg_rows] + [jnp.zeros((3, LANES), jnp.int32)], axis=0)
    seg_ref[0] = seg_v[...]
    to_smem = pltpu.make_async_copy(seg_v, seg_s, ssem)
    to_smem.start()
    to_smem.wait()

    def run(src, dst, n):
        def piece(c, carry_):
            _tile_copy(hbuf, src + c * RUN_CHUNK, xg_hbm, dst + c * RUN_CHUNK, RUN_CHUNK, sem).start()
            return carry_

        lax.fori_loop(0, n // RUN_CHUNK, piece, 0)
        size = RUN_CHUNK // 2
        while size >= 1:
            off = n & ~(2 * size - 1)

            @pl.when((n & size) != 0)
            def _(size=size, off=off):
                _tile_copy(hbuf, src + off, xg_hbm, dst + off, size, sem).start()

            size //= 2

    def scatter(e, carry_):
        n_first = seg_s[1, e]
        run(seg_s[0, e], seg_s[2, e], n_first)
        n_rest = seg_s[3, e]

        @pl.when(n_rest > 0)
        def _():
            run(seg_s[0, e] + n_first, seg_s[4, e], n_rest)

        return carry_

    lax.fori_loop(0, NE, scatter, 0)

    @pl.when(i == last)
    def _():
        scatter_wait()
        lane1 = lax.broadcasted_iota(jnp.int32, (1, LANES), 1)
        cur_row = jnp.where(lane1 == NE, nfree_new, as_row(cur_ref[...]))
        tail_v[0:1, :] = cur_row.astype(jnp.int32)
        tail_v[1:2, :] = as_row(total).astype(jnp.int32)
        cp = pltpu.make_async_copy(tail_v, tail_s, ssem)
        cp.start()
        cp.wait()
        zbuf[...] = jnp.zeros_like(zbuf)

        def fills(wait):
            def per_expert(e, carry_):
                used = tail_s[1, e] & (EXP_BM - 1)
                first = tail_s[0, e] * EXP_BM + used
                npad = jnp.where(used > 0, EXP_BM - used, 0)

                def chunk(size):
                    off = npad & ~(2 * size - 1)

                    @pl.when((npad & size) != 0)
                    def _():
                        cpz = _tile_copy(zbuf, 0, xg_hbm, first + off, size, zsem)
                        cpz.wait() if wait else cpz.start()

                _pad_chunks(chunk)
                return carry_

            lax.fori_loop(0, NE, per_expert, 0)

            def per_block(b, carry_):
                cpz = _tile_copy(zbuf, 0, xg_hbm, b * EXP_BM, EXP_BM, zsem)
                cpz.wait() if wait else cpz.start()
                return carry_

            lax.fori_loop(tail_s[0, NE], nblk, per_block, 0)

        fills(False)
        fills(True)


def _outproj_call(mix, x2, w_bf, gate, shift, scale, nw, wr_t, rb_col, seq, nb):
    n = x2.shape[0]
    tiles_per_batch = seq // OUT_TM
    bmap = lambda i: (i // tiles_per_batch, 0, 0)
    row = lambda i: (i, 0)
    lane = lambda i: (0, i)
    const = lambda i: (0, 0)
    table = lambda i: (i, 0, 0)
    return pl.pallas_call(
        _outproj_kernel,
        grid=(n // OUT_TM,),
        in_specs=[
            pl.BlockSpec((OUT_TM, D), row),
            pl.BlockSpec((OUT_TM, D), row),
            pl.BlockSpec((1, 1, D), bmap),
            pl.BlockSpec((1, 1, D), bmap),
            pl.BlockSpec((1, 1, D), bmap),
            pl.BlockSpec((1, D), const),
            pl.BlockSpec((NE, D), const),
            pl.BlockSpec((NE, 1), const),
            pl.BlockSpec(memory_space=pl.ANY),
        ],
        out_specs=[
            pl.BlockSpec((OUT_TM, D), row),
            pl.BlockSpec((1, 2, OUT_TM), table),
            pl.BlockSpec((1, 8, LANES), table),
            pl.BlockSpec((2, OUT_TM), lane),
            pl.BlockSpec((1, BLK_LANES), const),
            pl.BlockSpec((1, 1), const),
            pl.BlockSpec(memory_space=pl.ANY),
        ],
        out_shape=[
            jax.ShapeDtypeStruct((n, D), F32),
            jax.ShapeDtypeStruct((n // OUT_TM, 2, OUT_TM), jnp.int32),
            jax.ShapeDtypeStruct((n // OUT_TM, 8, LANES), jnp.int32),
            jax.ShapeDtypeStruct((2, n), F32),
            jax.ShapeDtypeStruct((1, BLK_LANES), jnp.int32),
            jax.ShapeDtypeStruct((1, 1), jnp.int32),
            jax.ShapeDtypeStruct((nb * EXP_BM * PK_SUB, LANES), jnp.uint32),
        ],
        scratch_shapes=[
            pltpu.VMEM((NE, 1), F32),
            pltpu.VMEM((NE, 1), F32),
            pltpu.VMEM((1, BLK_LANES), F32),
            pltpu.VMEM((1, 1), F32),
            pltpu.VMEM((D, D), BF16),
            pltpu.VMEM((2 * OUT_TM * PK_SUB, LANES), jnp.uint32),
            pltpu.VMEM((8, LANES), jnp.int32),
            pltpu.SMEM((8, LANES), jnp.int32),
            pltpu.VMEM((2, LANES), jnp.int32),
            pltpu.SMEM((2, LANES), jnp.int32),
            pltpu.VMEM((EXP_BM * PK_SUB, LANES), jnp.uint32),
            pltpu.SemaphoreType.DMA(()),
            pltpu.SemaphoreType.DMA(()),
            pltpu.SemaphoreType.DMA(()),
        ],
        compiler_params=pltpu.CompilerParams(
            dimension_semantics=("arbitrary",), vmem_limit_bytes=BIG_VMEM_LIMIT),
        name="out_proj_router",
    )(mix, x2, gate, shift, scale, nw, wr_t, rb_col, w_bf)


def _pack_rows(v, o_ref, rows):
    hi = lax.bitcast_convert_type(v[:, :D // 2].astype(BF16).astype(F32), jnp.uint32)
    lo = lax.bitcast_convert_type(v[:, D // 2:].astype(BF16).astype(F32), jnp.uint32)
    packed = hi | (lo >> 16)
    for j in range(PK_SUB):
        o_ref[pl.ds(j, rows, stride=PK_SUB), :] = packed[:, j * LANES:(j + 1) * LANES]


def _unpack_halves(x_ref, rows):
    his, los = [], []
    for j in range(PK_SUB):
        p = x_ref[pl.ds(j, rows, stride=PK_SUB), :]
        his.append(lax.bitcast_convert_type(p & jnp.uint32(0xFFFF0000), F32))
        los.append(lax.bitcast_convert_type(p << 16, F32))
    return his, los


def _tile_rows(row, nrows):
    start = row * PK_SUB
    if not isinstance(row, int):
        start = pl.multiple_of(start, PK_SUB)
    return pl.ds(start, nrows * PK_SUB)


def _tile_copy(src, src_row, dst, dst_row, nrows, sem):
    return pltpu.make_async_copy(src.at[_tile_rows(src_row, nrows), :], dst.at[_tile_rows(dst_row, nrows), :], sem)


def _pad_chunks(fn):
    size = EXP_BM // 2
    while size >= 1:
        fn(size)
        size //= 2


def _expert_kernel(order_ref, be_ref, na_ref, x_ref, w1_ref, w3_ref, w2_ref, o_ref, w1b, w3b, w2b):
    i = pl.program_id(0)
    active = i < na_ref[0]
    changed = jnp.logical_or(i == 0, be_ref[i] != be_ref[jnp.maximum(i - 1, 0)])

    @pl.when(jnp.logical_and(active, changed))
    def _():
        w1b[...] = w1_ref[0, 0].astype(BF16)
        w3b[...] = w3_ref[0, 0].astype(BF16)
        w2b[...] = w2_ref[0, 0].astype(BF16)

    @pl.when(active)
    def _():
        rows = EXP_BM // EXP_SUB
        for s in range(EXP_SUB):
            part = pl.ds(s * rows * PK_SUB, rows * PK_SUB)
            his, los = _unpack_halves(x_ref.at[part], rows)
            xb = jnp.concatenate([c.astype(BF16) for c in his + los], axis=1)
            h1 = jnp.dot(xb, w1b[...], preferred_element_type=F32)
            h3 = jnp.dot(xb, w3b[...], preferred_element_type=F32)
            act = (h1 * _sigmoid(h1) * h3).astype(BF16)
            _pack_rows(jnp.dot(act, w2b[...], preferred_element_type=F32), o_ref.at[part], rows)

    @pl.when(jnp.logical_not(active))
    def _():
        o_ref[...] = jnp.zeros_like(o_ref)


def _expert_call(order, block_expert, n_active, xg, w1, w3, w2, layer):
    nb = order.shape[0]
    live = lambda i, od, be, na: (od[jnp.minimum(i, na[0] - 1)], 0)
    wmap = lambda i, od, be, na: (layer, be[jnp.minimum(i, na[0] - 1)], 0, 0)
    grid_spec = pltpu.PrefetchScalarGridSpec(
        num_scalar_prefetch=3,
        grid=(nb,),
        in_specs=[
            pl.BlockSpec((EXP_BM * PK_SUB, LANES), live),
            pl.BlockSpec((1, 1, D, FF), wmap),
            pl.BlockSpec((1, 1, D, FF), wmap),
            pl.BlockSpec((1, 1, FF, D), wmap),
        ],
        out_specs=pl.BlockSpec((EXP_BM * PK_SUB, LANES), lambda i, od, be, na: (od[i], 0)),
        scratch_shapes=[pltpu.VMEM((D, FF), BF16), pltpu.VMEM((D, FF), BF16), pltpu.VMEM((FF, D), BF16)],
    )
    return pl.pallas_call(
        _expert_kernel,
        grid_spec=grid_spec,
        out_shape=jax.ShapeDtypeStruct((nb * EXP_BM * PK_SUB, LANES), jnp.uint32),
        compiler_params=pltpu.CompilerParams(
            dimension_semantics=("arbitrary",), vmem_limit_bytes=VMEM_LIMIT),
        name="expert_blocks",
    )(order, block_expert, n_active, xg, w1, w3, w2)


def _combine_kernel(slot_ref, x_ref, gate_ref, gw_ref, fw_ref, y_hbm, o_ref, ybuf, sem, *, final_norm):
    base = (pl.program_id(0) % (OUT_TM // CMB_TM)) * CMB_TM

    def issue(r, carry):
        for k in range(2):
            _tile_copy(y_hbm, slot_ref[0, k, base + r], ybuf.at[k], r, 1, sem).start(priority=k)
        return carry

    lax.fori_loop(0, CMB_TM, issue, 0)

    def drain(r, carry):
        for k in range(2):
            _tile_copy(y_hbm, 0, ybuf.at[k], r, 1, sem).wait()
        return carry

    lax.fori_loop(0, CMB_TM, drain, 0)
    gw = gw_ref[...]
    h0, l0 = _unpack_halves(ybuf.at[0], CMB_TM)
    h1, l1 = _unpack_halves(ybuf.at[1], CMB_TM)
    g0, g1 = gw[:, 0:1], gw[:, 1:2]
    moe = jnp.concatenate([g0 * a + g1 * b for a, b in zip(h0 + l0, h1 + l1)], axis=1)
    x2 = x_ref[...] + gate_ref[0] * moe
    if final_norm:
        ms = jnp.mean(x2 * x2, axis=-1, keepdims=True)
        x2 = x2 * lax.rsqrt(ms + NORM_EPS) * fw_ref[...]
    o_ref[...] = x2


def _combine_call(slots, x1, gate, gw_col, fw, yg, seq, final_norm):
    n = x1.shape[0]
    tiles_per_batch = seq // CMB_TM
    per_table = OUT_TM // CMB_TM
    return pl.pallas_call(
        functools.partial(_combine_kernel, final_norm=final_norm),
        grid=(n // CMB_TM,),
        in_specs=[
            pl.BlockSpec((1, 2, OUT_TM), lambda i: (i // per_table, 0, 0), memory_space=pltpu.SMEM),
            pl.BlockSpec((CMB_TM, D), lambda i: (i, 0)),
            pl.BlockSpec((1, 1, D), lambda i: (i // tiles_per_batch, 0, 0)),
            pl.BlockSpec((CMB_TM, 2), lambda i: (i, 0)),
            pl.BlockSpec((1, D), lambda i: (0, 0)),
            pl.BlockSpec(memory_space=pl.ANY),
        ],
        out_specs=pl.BlockSpec((CMB_TM, D), lambda i: (i, 0)),
        out_shape=jax.ShapeDtypeStruct((n, D), F32),
        scratch_shapes=[pltpu.VMEM((2, CMB_TM * PK_SUB, LANES), jnp.uint32), pltpu.SemaphoreType.DMA(())],
        compiler_params=pltpu.CompilerParams(
            dimension_semantics=("arbitrary",), vmem_limit_bytes=VMEM_LIMIT),
        name="moe_combine",
    )(slots, x1, gate, gw_col, fw, yg)


def _decay_tables():
    log_gamma = jnp.log1p(-jnp.exp2(-5.0 - jnp.arange(NH, dtype=F32)))
    pos = jnp.arange(CH, dtype=F32)
    diff = pos[:, None] - pos[None, :]
    mask = jnp.where(diff >= 0, jnp.exp(log_gamma[:, None, None] * jnp.maximum(diff, 0.0)), 0.0)
    q_decay = jnp.exp(log_gamma[:, None] * (pos + 1.0))
    k_decay = jnp.exp(log_gamma[:, None] * (CH - 1.0 - pos))
    chunk_decay = jnp.exp(log_gamma * CH)
    return mask.astype(F32), q_decay.T, k_decay.T, chunk_decay


def kernel(x, c, positions, w_ada, b_ada, norm1_w, norm2_w, w_in, w_out, ret_norm_w, sg_w_s, sg_b_s,
           w_router, router_bias, w1, w3, w2, final_norm_w):
    bsz, seq, d = x.shape
    n = bsz * seq
    assert d == D and seq % INP_TM == 0 and seq % MIX_ROWS == 0 and n % CMB_TM == 0

    c_pad = jnp.zeros((8, D), F32).at[:bsz].set(c)
    mod = _ada_call(c_pad, w_ada, b_ada)[:, :bsz]
    mod = mod.reshape(N_LAYERS, bsz, N_MOD, 1, D)

    half = HD // 2
    inv_freq = (ROPE_THETA ** (-jnp.arange(half, dtype=F32) / half)).reshape(1, half)
    cos, sin = _rope_call(positions.reshape(n, 1), inv_freq)
    decay_mask, q_decay_t, k_decay_t, chunk_decay = _decay_tables()

    perm = (jnp.arange(NE) % NEG) * EPG + jnp.arange(NE) // NEG
    wr_t = w_router.T[perm]
    rb_col = router_bias[perm].reshape(NE, 1)

    nb = n * 2 // EXP_BM + NE
    assert nb <= BLK_LANES and n % OUT_TM == 0
    xs = x.reshape(n, D)
    for l in range(N_LAYERS):
        shift1, scale1, gate1, shift2, scale2, gate2 = [mod[l, :, t] for t in range(N_MOD)]
        proj = _inproj_call(xs, shift1, scale1, norm1_w[l].reshape(1, D), w_in[l].astype(BF16), cos, sin, seq)
        mix = _mix_call(proj, chunk_decay, decay_mask, q_decay_t, k_decay_t, ret_norm_w[l].reshape(1, RW),
                        sg_w_s[l], sg_b_s[l].T, bsz, seq)
        x1, slots, runs, gw, owner, n_active, xg = _outproj_call(
            mix, xs, w_out[l].astype(BF16), gate1, shift2, scale2, norm2_w[l].reshape(1, D), wr_t, rb_col, seq, nb)

        owner = owner.reshape(BLK_LANES)
        ids = jnp.arange(BLK_LANES, dtype=jnp.int32)
        ahead = (owner[None, :] < owner[:, None]) | ((owner[None, :] == owner[:, None]) & (ids[None, :] < ids[:, None]))
        pos = jnp.sum(ahead, axis=1)
        at = pos[None, :] == ids[:, None]
        order = jnp.sum(jnp.where(at, ids[None, :], 0), axis=1)[:nb].astype(jnp.int32)
        block_expert = jnp.minimum(jnp.sum(jnp.where(at, owner[None, :], 0), axis=1)[:nb], NE - 1).astype(jnp.int32)

        yg = _expert_call(order, block_expert, n_active.reshape(1), xg, w1, w3, w2, l)
        xs = _combine_call(slots, x1, gate2, gw.T, final_norm_w.reshape(1, D), yg, seq,
                           final_norm=(l == N_LAYERS - 1))
    return xs.reshape(bsz, seq, D)
```

```python
import functools

import jax
import jax.numpy as jnp
from jax import lax
from jax.experimental import pallas as pl
from jax.experimental.pallas import tpu as pltpu

F32 = jnp.float32
BF16 = jnp.bfloat16
HIGHEST = lax.Precision.HIGHEST

D = 2048
N_LAYERS = 2
RW = D // 2
NH = 4
HD = RW // NH
CH = 128
ROPE_THETA = 10000.0
SGW = D - RW
NG = 8
GD = SGW // NG
PROJ = 4 * RW + 2 * SGW
NE = 32
NEG = 8
EPG = NE // NEG
FF = 512
N_MOD = 6
NORM_EPS = 1e-6

VMEM_LIMIT = 48 * 1024 * 1024
BIG_VMEM_LIMIT = 56 * 1024 * 1024

ADA_TN = 1024
ROPE_TM = 2048
INP_TM = 256
INP_TN = 512
MIX_ROWS = 512
OUT_TM = 512
RUN_CHUNK = 16
EXP_BM = 512
EXP_SUB = 2
CMB_TM = OUT_TM
BLK_LANES = 256

LANES = 128
PK_SUB = D // 2 // LANES


def _sigmoid(v):
    return 1.0 / (1.0 + jnp.exp(-v))


def _gelu_tanh(v):
    return 0.5 * v * (1.0 + jnp.tanh(0.7978845608028654 * (v + 0.044715 * (v * v * v))))


def _ada_kernel(c_ref, w_ref, b_ref, o_ref):
    c = c_ref[...]
    ca = c * _sigmoid(c)
    o_ref[0] = jnp.dot(ca, w_ref[0], precision=HIGHEST, preferred_element_type=F32) + b_ref[0]


def _ada_call(c_pad, w_ada, b_ada):
    depth, _, ncol = w_ada.shape
    return pl.pallas_call(
        _ada_kernel,
        grid=(depth, ncol // ADA_TN),
        in_specs=[
            pl.BlockSpec((8, D), lambda l, j: (0, 0)),
            pl.BlockSpec((1, D, ADA_TN), lambda l, j: (l, 0, j)),
            pl.BlockSpec((1, 1, ADA_TN), lambda l, j: (l, 0, j)),
        ],
        out_specs=pl.BlockSpec((1, 8, ADA_TN), lambda l, j: (l, 0, j)),
        out_shape=jax.ShapeDtypeStruct((depth, 8, ncol), F32),
        compiler_params=pltpu.CompilerParams(vmem_limit_bytes=VMEM_LIMIT),
        name="ada_mod",
    )(c_pad, w_ada, b_ada.reshape(depth, 1, ncol))


def _rope_kernel(pos_ref, freq_ref, cos_ref, sin_ref):
    ang = pos_ref[...].astype(F32) * freq_ref[...]
    cos_ref[...] = jnp.cos(ang)
    sin_ref[...] = jnp.sin(ang)


def _rope_call(pos_col, inv_freq):
    n = pos_col.shape[0]
    half = inv_freq.shape[1]
    return pl.pallas_call(
        _rope_kernel,
        grid=(n // ROPE_TM,),
        in_specs=[
            pl.BlockSpec((ROPE_TM, 1), lambda i: (i, 0)),
            pl.BlockSpec((1, half), lambda i: (0, 0)),
        ],
        out_specs=[
            pl.BlockSpec((ROPE_TM, half), lambda i: (i, 0)),
            pl.BlockSpec((ROPE_TM, half), lambda i: (i, 0)),
        ],
        out_shape=[jax.ShapeDtypeStruct((n, half), F32)] * 2,
        name="rope_tables",
    )(pos_col, inv_freq)


def _inproj_kernel(x_ref, shift_ref, scale_ref, nw_ref, cos_ref, sin_ref, w_hbm, o_ref, w_vmem, sem):
    @pl.when(pl.program_id(0) == 0)
    def _():
        cp = pltpu.make_async_copy(w_hbm, w_vmem, sem)
        cp.start()
        cp.wait()

    x = x_ref[...]
    ms = jnp.mean(x * x, axis=-1, keepdims=True)
    h = x * lax.rsqrt(ms + NORM_EPS) * nw_ref[...]
    h = (h * (1.0 + scale_ref[0]) + shift_ref[0]).astype(BF16)
    cos = cos_ref[...]
    sin = sin_ref[...]
    half = HD // 2

    for j in range(PROJ // INP_TN):
        c0 = j * INP_TN
        acc = jnp.dot(h, w_vmem[:, c0:c0 + INP_TN], preferred_element_type=F32)
        sec = c0 // RW
        if sec in (0, 1):
            scale = 1.0 if sec == 0 else HD ** -0.5
            for hh in range(INP_TN // HD):
                a = acc[:, hh * HD:hh * HD + half]
                b = acc[:, hh * HD + half:(hh + 1) * HD]
                o_ref[:, c0 + hh * HD:c0 + hh * HD + half] = ((a * cos - b * sin) * scale).astype(BF16)
                o_ref[:, c0 + hh * HD + half:c0 + (hh + 1) * HD] = ((b * cos + a * sin) * scale).astype(BF16)
        elif sec == 2:
            o_ref[:, c0:c0 + INP_TN] = acc.astype(BF16)
        elif sec == 3:
            o_ref[:, c0:c0 + INP_TN] = (acc * _sigmoid(acc)).astype(BF16)
        elif sec == 4:
            o_ref[:, c0:c0 + INP_TN] = _gelu_tanh(acc).astype(BF16)
        else:
            for gg in range(INP_TN // GD):
                t = _gelu_tanh(acc[:, gg * GD:(gg + 1) * GD])
                ms_g = jnp.mean(t * t, axis=-1, keepdims=True)
                o_ref[:, c0 + gg * GD:c0 + (gg + 1) * GD] = (t * lax.rsqrt(ms_g + NORM_EPS)).astype(BF16)


def _inproj_call(x2, shift, scale, nw, w_bf, cos, sin, seq):
    n = x2.shape[0]
    tiles_per_batch = seq // INP_TM
    bmap = lambda i: (i // tiles_per_batch, 0, 0)
    return pl.pallas_call(
        _inproj_kernel,
        grid=(n // INP_TM,),
        in_specs=[
            pl.BlockSpec((INP_TM, D), lambda i: (i, 0)),
            pl.BlockSpec((1, 1, D), bmap),
            pl.BlockSpec((1, 1, D), bmap),
            pl.BlockSpec((1, D), lambda i: (0, 0)),
            pl.BlockSpec((INP_TM, HD // 2), lambda i: (i, 0)),
            pl.BlockSpec((INP_TM, HD // 2), lambda i: (i, 0)),
            pl.BlockSpec(memory_space=pl.ANY),
        ],
        out_specs=pl.BlockSpec((INP_TM, PROJ), lambda i: (i, 0)),
        out_shape=jax.ShapeDtypeStruct((n, PROJ), BF16),
        scratch_shapes=[pltpu.VMEM((D, PROJ), BF16), pltpu.SemaphoreType.DMA(())],
        compiler_params=pltpu.CompilerParams(
            dimension_semantics=("arbitrary",), vmem_limit_bytes=BIG_VMEM_LIMIT),
        name="in_proj",
    )(x2, shift, scale, nw, cos, sin, w_bf)


def _mix_kernel(cd_ref, q_ref, k_ref, v_ref, g_ref, u_ref, vs_ref, dm_ref, qd_ref, kd_ref,
                rnw_ref, ws_ref, bs_ref, o_ref, state_ref):
    @pl.when(pl.program_id(1) == 0)
    def _():
        state_ref[...] = jnp.zeros_like(state_ref)

    row = lax.broadcasted_iota(jnp.int32, (CH, CH), 0)
    col = lax.broadcasted_iota(jnp.int32, (CH, CH), 1)
    causal = row >= col

    chunks = [slice(c * CH, (c + 1) * CH) for c in range(MIX_ROWS // CH)]

    for g in range(NG):
        cols = slice(g * GD, (g + 1) * GD)
        wm = jnp.where(causal, ws_ref[g], 0.0).astype(BF16)
        vs_wide = jnp.concatenate([vs_ref[rows, cols] for rows in chunks], axis=1)
        z = jnp.dot(wm, vs_wide, preferred_element_type=F32) + bs_ref[:, g:g + 1]
        for c, rows in enumerate(chunks):
            zc = z[:, c * GD:(c + 1) * GD]
            o_ref[rows, RW + g * GD:RW + (g + 1) * GD] = (u_ref[rows, cols].astype(F32) * zc).astype(BF16)

    for h in range(NH):
        cols = slice(h * HD, (h + 1) * HD)
        st = state_ref[h]
        for rows in chunks:
            q = q_ref[rows, cols]
            k = k_ref[rows, cols]
            v = v_ref[rows, cols]
            s = lax.dot_general(q, k, (((1,), (1,)), ((), ())), preferred_element_type=F32)
            s = s * dm_ref[h]
            intra = jnp.dot(s.astype(BF16), v, preferred_element_type=F32)
            cross = jnp.dot(q, st.astype(BF16), preferred_element_type=F32) * qd_ref[:, h:h + 1]
            kdec = (k.astype(F32) * kd_ref[:, h:h + 1]).astype(BF16)
            upd = lax.dot_general(kdec, v, (((0,), (0,)), ((), ())), preferred_element_type=F32)
            st = st * cd_ref[h] + upd
            o = intra + cross
            ms = jnp.mean(o * o, axis=-1, keepdims=True)
            o = o * lax.rsqrt(ms + NORM_EPS) * rnw_ref[:, cols]
            o_ref[rows, cols] = (o * g_ref[rows, cols].astype(F32)).astype(BF16)
        state_ref[h] = st


def _mix_call(proj, chunk_decay, decay_mask, q_decay_t, k_decay_t, rnw, ws, bs_t, bsz, seq):
    n = proj.shape[0]
    steps = seq // MIX_ROWS

    def colblk(jb):
        return pl.BlockSpec((MIX_ROWS, RW), lambda b, s, cd: (b * steps + s, jb))

    const2 = lambda b, s, cd: (0, 0)
    const3 = lambda b, s, cd: (0, 0, 0)
    grid_spec = pltpu.PrefetchScalarGridSpec(
        num_scalar_prefetch=1,
        grid=(bsz, steps),
        in_specs=[colblk(0), colblk(1), colblk(2), colblk(3), colblk(4), colblk(5),
                  pl.BlockSpec((NH, CH, CH), const3),
                  pl.BlockSpec((CH, NH), const2),
                  pl.BlockSpec((CH, NH), const2),
                  pl.BlockSpec((1, RW), const2),
                  pl.BlockSpec((NG, CH, CH), const3),
                  pl.BlockSpec((CH, NG), const2)],
        out_specs=pl.BlockSpec((MIX_ROWS, D), lambda b, s, cd: (b * steps + s, 0)),
        scratch_shapes=[pltpu.VMEM((NH, HD, HD), F32)],
    )
    return pl.pallas_call(
        _mix_kernel,
        grid_spec=grid_spec,
        out_shape=jax.ShapeDtypeStruct((n, D), BF16),
        compiler_params=pltpu.CompilerParams(
            dimension_semantics=("arbitrary", "arbitrary"), vmem_limit_bytes=VMEM_LIMIT),
        name="retention_spatial_mix",
    )(chunk_decay, proj, proj, proj, proj, proj, proj, decay_mask, q_decay_t, k_decay_t, rnw, ws, bs_t)


def _outproj_kernel(mix_ref, x_ref, gate_ref, shift_ref, scale_ref, nw_ref, wr_ref, rb_ref, w_hbm,
                    x1_ref, pos_ref, seg_ref, gw_ref, be_ref, na_ref, xg_hbm,
                    carry_ref, cur_ref, bev_ref, nfree_ref, w_vmem, hbuf, seg_v, seg_s, tail_v, tail_s, zbuf,
                    sem, ssem, zsem):
    i = pl.program_id(0)
    last = pl.num_programs(0) - 1
    nblk = xg_hbm.shape[0] // (EXP_BM * PK_SUB)

    @pl.when(i == 0)
    def _():
        cp = pltpu.make_async_copy(w_hbm, w_vmem, ssem)
        cp.start()
        carry_ref[...] = jnp.zeros_like(carry_ref)
        cur_ref[...] = jnp.zeros_like(cur_ref)
        nfree_ref[...] = jnp.zeros_like(nfree_ref)
        bev_ref[...] = jnp.full(bev_ref.shape, NE, F32)
        cp.wait()

    y = jnp.dot(mix_ref[...], w_vmem[...], preferred_element_type=F32)
    x1 = x_ref[...] + gate_ref[0] * y
    x1_ref[...] = x1
    ms = jnp.mean(x1 * x1, axis=-1, keepdims=True)
    h2 = x1 * lax.rsqrt(ms + NORM_EPS) * nw_ref[...]
    h2 = h2 * (1.0 + scale_ref[0]) + shift_ref[0]

    def scatter_wait():
        _tile_copy(hbuf, 0, xg_hbm, 0, 2 * OUT_TM, sem).wait()

    nt = (((1,), (1,)), ((), ()))
    wr = wr_ref[...]
    w_hi = wr.astype(BF16)
    w_lo = (wr - w_hi.astype(F32)).astype(BF16)
    h_hi = h2.astype(BF16)
    h_lo = (h2 - h_hi.astype(F32)).astype(BF16)
    p_hi = lax.dot_general(jnp.concatenate([w_hi, w_lo], axis=0), h_hi, nt, preferred_element_type=F32)
    p_lo = lax.dot_general(w_hi, h_lo, nt, preferred_element_type=F32)
    logits = p_hi[:NE] + p_hi[NE:] + p_lo
    scores = _sigmoid(logits)
    biased = scores + rb_ref[...]
    a = [biased[m * NEG:(m + 1) * NEG] for m in range(EPG)]
    sc = [scores[m * NEG:(m + 1) * NEG] for m in range(EPG)]
    p, q = jnp.maximum(a[0], a[1]), jnp.minimum(a[0], a[1])
    r, s = jnp.maximum(a[2], a[3]), jnp.minimum(a[2], a[3])
    group_score = jnp.maximum(p, r) + jnp.maximum(jnp.minimum(p, r), jnp.maximum(q, s))
    gidx = lax.broadcasted_iota(jnp.int32, group_score.shape, 0)
    gmax = jnp.max(group_score, axis=0, keepdims=True)
    g_sel = jnp.min(jnp.where(group_score == gmax, gidx, NEG), axis=0, keepdims=True)
    pick = gidx == g_sel
    v = [jnp.sum(jnp.where(pick, a[m], 0.0), axis=0, keepdims=True) for m in range(EPG)]
    u = [jnp.sum(jnp.where(pick, sc[m], 0.0), axis=0, keepdims=True) for m in range(EPG)]

    def first_argmax(vals):
        m = functools.reduce(jnp.maximum, vals)
        idx = jnp.full(m.shape, EPG - 1, jnp.int32)
        for t in range(EPG - 2, -1, -1):
            idx = jnp.where(vals[t] == m, t, idx)
        return idx

    def select(vals, idx):
        out = vals[EPG - 1]
        for t in range(EPG - 2, -1, -1):
            out = jnp.where(idx == t, vals[t], out)
        return out

    i1 = first_argmax(v)
    v2 = [jnp.where(i1 == t, -jnp.inf, v[t]) for t in range(EPG)]
    i2 = first_argmax(v2)
    s1 = select(u, i1)
    s2 = select(u, i2)
    denom = s1 + s2
    e1 = g_sel * EPG + i1
    e2 = g_sel * EPG + i2
    gw_ref[0:1, :] = s1 / denom
    gw_ref[1:2, :] = s2 / denom

    tm = e1.shape[1]
    eidx = lax.broadcasted_iota(jnp.int32, (NE, tm), 0)
    oh1 = eidx == e1
    oh2 = eidx == e2
    oh = jnp.where(oh1 | oh2, 1.0, 0.0)
    tr = lax.broadcasted_iota(jnp.int32, (tm, tm), 0)
    tc = lax.broadcasted_iota(jnp.int32, (tm, tm), 1)
    earlier = jnp.where(tr < tc, 1.0, 0.0).astype(BF16)
    carry = carry_ref[...]
    local = jnp.dot(oh.astype(BF16), earlier, preferred_element_type=F32)
    rank = local + carry
    count = jnp.sum(oh, axis=1, keepdims=True)
    total = carry + count
    carry_ref[...] = total

    inv_bm = 1.0 / EXP_BM
    blocks_before = jnp.floor((carry + (EXP_BM - 1)) * inv_bm)
    need = jnp.floor((total + (EXP_BM - 1)) * inv_bm) - blocks_before
    er = lax.broadcasted_iota(jnp.int32, (NE, NE), 0)
    ec = lax.broadcasted_iota(jnp.int32, (NE, NE), 1)
    lower = jnp.where(ec < er, 1.0, 0.0).astype(BF16)
    count_hi = jnp.floor(count * (1.0 / 16.0))
    lane_e = lax.broadcasted_iota(jnp.int32, (NE, LANES), 1)
    digits = jnp.where(lane_e == 0, need, jnp.where(lane_e == 1, count_hi, count - 16.0 * count_hi))
    sums = jnp.dot(lower, digits.astype(BF16), preferred_element_type=F32)
    nfree = nfree_ref[...]
    base = nfree + sums[:, 0:1]
    start = 16.0 * sums[:, 1:2] + sums[:, 2:3]
    cur = cur_ref[...]
    cur_ref[...] = jnp.where(need > 0, base + need - 1.0, cur)
    nfree_new = nfree + jnp.sum(need, axis=0, keepdims=True)
    nfree_ref[...] = nfree_new
    blk = lax.broadcasted_iota(jnp.int32, (NE, bev_ref.shape[1]), 1).astype(F32)
    hit = jnp.logical_and(blk >= base, blk < base + need)
    owner = jnp.sum(jnp.where(hit, eidx[:, 0:1].astype(F32), 0.0), axis=0, keepdims=True)
    taken = jnp.sum(jnp.where(hit, 1.0, 0.0), axis=0, keepdims=True)
    bev = jnp.where(taken > 0, owner, bev_ref[...])
    bev_ref[...] = bev
    be_ref[...] = bev.astype(jnp.int32)
    na_ref[...] = nfree_new.astype(jnp.int32)

    sorted_all = start + local
    pos1 = jnp.sum(jnp.where(oh1, sorted_all, 0.0), axis=0, keepdims=True).astype(jnp.int32)
    pos2 = jnp.sum(jnp.where(oh2, sorted_all, 0.0), axis=0, keepdims=True).astype(jnp.int32)
    pos_ref[0, 0:1, :] = pos1
    pos_ref[0, 1:2, :] = pos2

    @pl.when(i >= 1)
    def _():
        scatter_wait()

    for part in range(2):
        prow = lax.broadcasted_iota(jnp.int32, (tm, tm), 0) + part * tm
        perm = jnp.where((prow == pos1) | (prow == pos2), 1.0, 0.0).astype(BF16)
        sorted_rows = jnp.dot(perm, h_hi, preferred_element_type=F32)
        _pack_rows(sorted_rows, hbuf.at[pl.ds(part * tm * PK_SUB, tm * PK_SUB)], tm)

    as_row = lambda col: jnp.sum(jnp.where(lane_e == eidx[:, 0:1], col, 0.0), axis=0, keepdims=True)
    j0 = jnp.floor(carry * inv_bm)
    in_block = carry - j0 * EXP_BM
    n_a = jnp.minimum(count, EXP_BM - in_block)
    block_of = lambda j: jnp.where(j < blocks_before, cur, base + (j - blocks_before))
    seg_rows = [start, n_a, block_of(j0) * EXP_BM + in_block, count - n_a, block_of(j0 + 1.0) * EXP_BM]
    seg_v[...] = jnp.concatenate(
        [as_row(col).astype(jnp.int32) for col in seg_rows] + [jnp.zeros((3, LANES), jnp.int32)], axis=0)
    seg_ref[0] = seg_v[...]
    to_smem = pltpu.make_async_copy(seg_v, seg_s, ssem)
    to_smem.start()
    to_smem.wait()

    def scatter(e, carry_):
        n_first = seg_s[1, e]
        _start_run(hbuf, seg_s[0, e], xg_hbm, seg_s[2, e], n_first, sem)
        n_rest = seg_s[3, e]

        @pl.when(n_rest > 0)
        def _():
            _start_run(hbuf, seg_s[0, e] + n_first, xg_hbm, seg_s[4, e], n_rest, sem)

        return carry_

    lax.fori_loop(0, NE, scatter, 0)

    @pl.when(i == last)
    def _():
        scatter_wait()
        lane1 = lax.broadcasted_iota(jnp.int32, (1, LANES), 1)
        cur_row = jnp.where(lane1 == NE, nfree_new, as_row(cur_ref[...]))
        tail_v[0:1, :] = cur_row.astype(jnp.int32)
        tail_v[1:2, :] = as_row(total).astype(jnp.int32)
        cp = pltpu.make_async_copy(tail_v, tail_s, ssem)
        cp.start()
        cp.wait()
        zbuf[...] = jnp.zeros_like(zbuf)

        def fills(wait):
            def per_expert(e, carry_):
                used = tail_s[1, e] & (EXP_BM - 1)
                first = tail_s[0, e] * EXP_BM + used
                npad = jnp.where(used > 0, EXP_BM - used, 0)

                def chunk(size):
                    off = npad & ~(2 * size - 1)

                    @pl.when((npad & size) != 0)
                    def _():
                        cpz = _tile_copy(zbuf, 0, xg_hbm, first + off, size, zsem)
                        cpz.wait() if wait else cpz.start()

                _pad_chunks(chunk)
                return carry_

            lax.fori_loop(0, NE, per_expert, 0)

            def per_block(b, carry_):
                cpz = _tile_copy(zbuf, 0, xg_hbm, b * EXP_BM, EXP_BM, zsem)
                cpz.wait() if wait else cpz.start()
                return carry_

            lax.fori_loop(tail_s[0, NE], nblk, per_block, 0)

        fills(False)
        fills(True)


def _outproj_call(mix, x2, w_bf, gate, shift, scale, nw, wr_t, rb_col, seq, nb):
    n = x2.shape[0]
    tiles_per_batch = seq // OUT_TM
    bmap = lambda i: (i // tiles_per_batch, 0, 0)
    row = lambda i: (i, 0)
    lane = lambda i: (0, i)
    const = lambda i: (0, 0)
    table = lambda i: (i, 0, 0)
    return pl.pallas_call(
        _outproj_kernel,
        grid=(n // OUT_TM,),
        in_specs=[
            pl.BlockSpec((OUT_TM, D), row),
            pl.BlockSpec((OUT_TM, D), row),
            pl.BlockSpec((1, 1, D), bmap),
            pl.BlockSpec((1, 1, D), bmap),
            pl.BlockSpec((1, 1, D), bmap),
            pl.BlockSpec((1, D), const),
            pl.BlockSpec((NE, D), const),
            pl.BlockSpec((NE, 1), const),
            pl.BlockSpec(memory_space=pl.ANY),
        ],
        out_specs=[
            pl.BlockSpec((OUT_TM, D), row),
            pl.BlockSpec((1, 2, OUT_TM), table),
            pl.BlockSpec((1, 8, LANES), table),
            pl.BlockSpec((2, OUT_TM), lane),
            pl.BlockSpec((1, BLK_LANES), const),
            pl.BlockSpec((1, 1), const),
            pl.BlockSpec(memory_space=pl.ANY),
        ],
        out_shape=[
            jax.ShapeDtypeStruct((n, D), F32),
            jax.ShapeDtypeStruct((n // OUT_TM, 2, OUT_TM), jnp.int32),
            jax.ShapeDtypeStruct((n // OUT_TM, 8, LANES), jnp.int32),
            jax.ShapeDtypeStruct((2, n), F32),
            jax.ShapeDtypeStruct((1, BLK_LANES), jnp.int32),
            jax.ShapeDtypeStruct((1, 1), jnp.int32),
            jax.ShapeDtypeStruct((nb * EXP_BM * PK_SUB, LANES), jnp.uint32),
        ],
        scratch_shapes=[
            pltpu.VMEM((NE, 1), F32),
            pltpu.VMEM((NE, 1), F32),
            pltpu.VMEM((1, BLK_LANES), F32),
            pltpu.VMEM((1, 1), F32),
            pltpu.VMEM((D, D), BF16),
            pltpu.VMEM((2 * OUT_TM * PK_SUB, LANES), jnp.uint32),
            pltpu.VMEM((8, LANES), jnp.int32),
            pltpu.SMEM((8, LANES), jnp.int32),
            pltpu.VMEM((2, LANES), jnp.int32),
            pltpu.SMEM((2, LANES), jnp.int32),
            pltpu.VMEM((EXP_BM * PK_SUB, LANES), jnp.uint32),
            pltpu.SemaphoreType.DMA(()),
            pltpu.SemaphoreType.DMA(()),
            pltpu.SemaphoreType.DMA(()),
        ],
        compiler_params=pltpu.CompilerParams(
            dimension_semantics=("arbitrary",), vmem_limit_bytes=BIG_VMEM_LIMIT),
        name="out_proj_router",
    )(mix, x2, gate, shift, scale, nw, wr_t, rb_col, w_bf)


def _pack_rows(v, o_ref, rows):
    hi = lax.bitcast_convert_type(v[:, :D // 2].astype(BF16).astype(F32), jnp.uint32)
    lo = lax.bitcast_convert_type(v[:, D // 2:].astype(BF16).astype(F32), jnp.uint32)
    packed = hi | (lo >> 16)
    for j in range(PK_SUB):
        o_ref[pl.ds(j, rows, stride=PK_SUB), :] = packed[:, j * LANES:(j + 1) * LANES]


def _unpack_halves(x_ref, rows):
    his, los = [], []
    for j in range(PK_SUB):
        p = x_ref[pl.ds(j, rows, stride=PK_SUB), :]
        his.append(lax.bitcast_convert_type(p & jnp.uint32(0xFFFF0000), F32))
        los.append(lax.bitcast_convert_type(p << 16, F32))
    return his, los


def _tile_rows(row, nrows):
    start = row * PK_SUB
    if not isinstance(row, int):
        start = pl.multiple_of(start, PK_SUB)
    return pl.ds(start, nrows * PK_SUB)


def _tile_copy(src, src_row, dst, dst_row, nrows, sem):
    return pltpu.make_async_copy(src.at[_tile_rows(src_row, nrows), :], dst.at[_tile_rows(dst_row, nrows), :], sem)


def _start_run(src, src_row, dst, dst_row, n, sem):
    def piece(c, carry):
        _tile_copy(src, src_row + c * RUN_CHUNK, dst, dst_row + c * RUN_CHUNK, RUN_CHUNK, sem).start()
        return carry

    lax.fori_loop(0, n // RUN_CHUNK, piece, 0)
    size = RUN_CHUNK // 2
    while size >= 1:
        off = n & ~(2 * size - 1)

        @pl.when((n & size) != 0)
        def _(size=size, off=off):
            _tile_copy(src, src_row + off, dst, dst_row + off, size, sem).start()

        size //= 2


def _pad_chunks(fn):
    size = EXP_BM // 2
    while size >= 1:
        fn(size)
        size //= 2


def _expert_kernel(order_ref, be_ref, na_ref, x_ref, w1_ref, w3_ref, w2_ref, o_ref, w1b, w3b, w2b):
    i = pl.program_id(0)
    active = i < na_ref[0]
    changed = jnp.logical_or(i == 0, be_ref[i] != be_ref[jnp.maximum(i - 1, 0)])

    @pl.when(jnp.logical_and(active, changed))
    def _():
        w1b[...] = w1_ref[0, 0].astype(BF16)
        w3b[...] = w3_ref[0, 0].astype(BF16)
        w2b[...] = w2_ref[0, 0].astype(BF16)

    @pl.when(active)
    def _():
        rows = EXP_BM // EXP_SUB
        for s in range(EXP_SUB):
            part = pl.ds(s * rows * PK_SUB, rows * PK_SUB)
            his, los = _unpack_halves(x_ref.at[part], rows)
            xb = jnp.concatenate([c.astype(BF16) for c in his + los], axis=1)
            h1 = jnp.dot(xb, w1b[...], preferred_element_type=F32)
            h3 = jnp.dot(xb, w3b[...], preferred_element_type=F32)
            act = (h1 * _sigmoid(h1) * h3).astype(BF16)
            _pack_rows(jnp.dot(act, w2b[...], preferred_element_type=F32), o_ref.at[part], rows)

    @pl.when(jnp.logical_not(active))
    def _():
        o_ref[...] = jnp.zeros_like(o_ref)


def _expert_call(order, block_expert, n_active, xg, w1, w3, w2, layer):
    nb = order.shape[0]
    live = lambda i, od, be, na: (od[jnp.minimum(i, na[0] - 1)], 0)
    wmap = lambda i, od, be, na: (layer, be[jnp.minimum(i, na[0] - 1)], 0, 0)
    grid_spec = pltpu.PrefetchScalarGridSpec(
        num_scalar_prefetch=3,
        grid=(nb,),
        in_specs=[
            pl.BlockSpec((EXP_BM * PK_SUB, LANES), live),
            pl.BlockSpec((1, 1, D, FF), wmap),
            pl.BlockSpec((1, 1, D, FF), wmap),
            pl.BlockSpec((1, 1, FF, D), wmap),
        ],
        out_specs=pl.BlockSpec((EXP_BM * PK_SUB, LANES), lambda i, od, be, na: (od[i], 0)),
        scratch_shapes=[pltpu.VMEM((D, FF), BF16), pltpu.VMEM((D, FF), BF16), pltpu.VMEM((FF, D), BF16)],
    )
    return pl.pallas_call(
        _expert_kernel,
        grid_spec=grid_spec,
        out_shape=jax.ShapeDtypeStruct((nb * EXP_BM * PK_SUB, LANES), jnp.uint32),
        compiler_params=pltpu.CompilerParams(
            dimension_semantics=("arbitrary",), vmem_limit_bytes=VMEM_LIMIT),
        name="expert_blocks",
    )(order, block_expert, n_active, xg, w1, w3, w2)


def _combine_kernel(seg_ref, x_ref, gate_ref, gw_ref, pos_ref, fw_ref, y_hbm, o_ref, ybuf, sem, *, final_norm):
    def gather(e, carry):
        n_first = seg_ref[0, 1, e]
        _start_run(y_hbm, seg_ref[0, 2, e], ybuf, seg_ref[0, 0, e], n_first, sem)
        n_rest = seg_ref[0, 3, e]

        @pl.when(n_rest > 0)
        def _():
            _start_run(y_hbm, seg_ref[0, 4, e], ybuf, seg_ref[0, 0, e] + n_first, n_rest, sem)

        return carry

    lax.fori_loop(0, NE, gather, 0)
    _tile_copy(y_hbm, 0, ybuf, 0, 2 * CMB_TM, sem).wait()

    his, los = _unpack_halves(ybuf, 2 * CMB_TM)
    ys = jnp.concatenate([c.astype(BF16) for c in his + los], axis=1)
    gw = gw_ref[...]
    pos = pos_ref[...]
    col = lax.broadcasted_iota(jnp.int32, (CMB_TM, 2 * CMB_TM), 1)
    picked = [jnp.dot(jnp.where(col == pos[:, k:k + 1], 1.0, 0.0).astype(BF16), ys, preferred_element_type=F32)
              for k in range(2)]
    moe = gw[:, 0:1] * picked[0] + gw[:, 1:2] * picked[1]
    x2 = x_ref[...] + gate_ref[0] * moe
    if final_norm:
        ms = jnp.mean(x2 * x2, axis=-1, keepdims=True)
        x2 = x2 * lax.rsqrt(ms + NORM_EPS) * fw_ref[...]
    o_ref[...] = x2


def _combine_call(runs, x1, gate, gw_col, pos_col, fw, yg, seq, final_norm):
    n = x1.shape[0]
    tiles_per_batch = seq // CMB_TM
    return pl.pallas_call(
        functools.partial(_combine_kernel, final_norm=final_norm),
        grid=(n // CMB_TM,),
        in_specs=[
            pl.BlockSpec((1, 8, LANES), lambda i: (i, 0, 0), memory_space=pltpu.SMEM),
            pl.BlockSpec((CMB_TM, D), lambda i: (i, 0)),
            pl.BlockSpec((1, 1, D), lambda i: (i // tiles_per_batch, 0, 0)),
            pl.BlockSpec((CMB_TM, 2), lambda i: (i, 0)),
            pl.BlockSpec((CMB_TM, 2), lambda i: (i, 0)),
            pl.BlockSpec((1, D), lambda i: (0, 0)),
            pl.BlockSpec(memory_space=pl.ANY),
        ],
        out_specs=pl.BlockSpec((CMB_TM, D), lambda i: (i, 0)),
        out_shape=jax.ShapeDtypeStruct((n, D), F32),
        scratch_shapes=[pltpu.VMEM((2 * CMB_TM * PK_SUB, LANES), jnp.uint32), pltpu.SemaphoreType.DMA(())],
        compiler_params=pltpu.CompilerParams(
            dimension_semantics=("arbitrary",), vmem_limit_bytes=VMEM_LIMIT),
        name="moe_combine",
    )(runs, x1, gate, gw_col, pos_col, fw, yg)


def _decay_tables():
    log_gamma = jnp.log1p(-jnp.exp2(-5.0 - jnp.arange(NH, dtype=F32)))
    pos = jnp.arange(CH, dtype=F32)
    diff = pos[:, None] - pos[None, :]
    mask = jnp.where(diff >= 0, jnp.exp(log_gamma[:, None, None] * jnp.maximum(diff, 0.0)), 0.0)
    q_decay = jnp.exp(log_gamma[:, None] * (pos + 1.0))
    k_decay = jnp.exp(log_gamma[:, None] * (CH - 1.0 - pos))
    chunk_decay = jnp.exp(log_gamma * CH)
    return mask.astype(F32), q_decay.T, k_decay.T, chunk_decay


def kernel(x, c, positions, w_ada, b_ada, norm1_w, norm2_w, w_in, w_out, ret_norm_w, sg_w_s, sg_b_s,
           w_router, router_bias, w1, w3, w2, final_norm_w):
    bsz, seq, d = x.shape
    n = bsz * seq
    assert d == D and seq % INP_TM == 0 and seq % MIX_ROWS == 0 and n % CMB_TM == 0

    c_pad = jnp.zeros((8, D), F32).at[:bsz].set(c)
    mod = _ada_call(c_pad, w_ada, b_ada)[:, :bsz]
    mod = mod.reshape(N_LAYERS, bsz, N_MOD, 1, D)

    half = HD // 2
    inv_freq = (ROPE_THETA ** (-jnp.arange(half, dtype=F32) / half)).reshape(1, half)
    cos, sin = _rope_call(positions.reshape(n, 1), inv_freq)
    decay_mask, q_decay_t, k_decay_t, chunk_decay = _decay_tables()

    perm = (jnp.arange(NE) % NEG) * EPG + jnp.arange(NE) // NEG
    wr_t = w_router.T[perm]
    rb_col = router_bias[perm].reshape(NE, 1)

    nb = n * 2 // EXP_BM + NE
    assert nb <= BLK_LANES and n % OUT_TM == 0
    xs = x.reshape(n, D)
    for l in range(N_LAYERS):
        shift1, scale1, gate1, shift2, scale2, gate2 = [mod[l, :, t] for t in range(N_MOD)]
        proj = _inproj_call(xs, shift1, scale1, norm1_w[l].reshape(1, D), w_in[l].astype(BF16), cos, sin, seq)
        mix = _mix_call(proj, chunk_decay, decay_mask, q_decay_t, k_decay_t, ret_norm_w[l].reshape(1, RW),
                        sg_w_s[l], sg_b_s[l].T, bsz, seq)
        x1, sorted_pos, runs, gw, owner, n_active, xg = _outproj_call(
            mix, xs, w_out[l].astype(BF16), gate1, shift2, scale2, norm2_w[l].reshape(1, D), wr_t, rb_col, seq, nb)

        owner = owner.reshape(BLK_LANES)
        ids = jnp.arange(BLK_LANES, dtype=jnp.int32)
        ahead = (owner[None, :] < owner[:, None]) | ((owner[None, :] == owner[:, None]) & (ids[None, :] < ids[:, None]))
        pos = jnp.sum(ahead, axis=1)
        at = pos[None, :] == ids[:, None]
        order = jnp.sum(jnp.where(at, ids[None, :], 0), axis=1)[:nb].astype(jnp.int32)
        block_expert = jnp.minimum(jnp.sum(jnp.where(at, owner[None, :], 0), axis=1)[:nb], NE - 1).astype(jnp.int32)

        yg = _expert_call(order, block_expert, n_active.reshape(1), xg, w1, w3, w2, l)
        pos_col = sorted_pos.transpose(0, 2, 1).reshape(n, 2)
        xs = _combine_call(runs, x1, gate2, gw.T, pos_col, final_norm_w.reshape(1, D), yg, seq,
                           final_norm=(l == N_LAYERS - 1))
    return xs.reshape(bsz, seq, D)
```

```python
import functools

import jax
import jax.numpy as jnp
from jax import lax
from jax.experimental import pallas as pl
from jax.experimental.pallas import tpu as pltpu

F32 = jnp.float32
BF16 = jnp.bfloat16
HIGHEST = lax.Precision.HIGHEST

D = 2048
N_LAYERS = 2
RW = D // 2
NH = 4
HD = RW // NH
CH = 128
ROPE_THETA = 10000.0
SGW = D - RW
NG = 8
GD = SGW // NG
PROJ = 4 * RW + 2 * SGW
NE = 32
NEG = 8
EPG = NE // NEG
FF = 512
N_MOD = 6
NORM_EPS = 1e-6

VMEM_LIMIT = 48 * 1024 * 1024
BIG_VMEM_LIMIT = 56 * 1024 * 1024

ADA_TN = 1024
ROPE_TM = 2048
INP_TM = 256
INP_TN = 512
MIX_ROWS = 512
OUT_TM = 512
RUN_CHUNK = 16
EXP_BM = 512
EXP_SUB = 2
CMB_TM = OUT_TM
BLK_LANES = 256

LANES = 128
PK_SUB = D // 2 // LANES


def _sigmoid(v):
    return 1.0 / (1.0 + jnp.exp(-v))


def _gelu_tanh(v):
    return 0.5 * v * (1.0 + jnp.tanh(0.7978845608028654 * (v + 0.044715 * (v * v * v))))


def _ada_kernel(c_ref, w_ref, b_ref, o_ref):
    c = c_ref[...]
    ca = c * _sigmoid(c)
    o_ref[0] = jnp.dot(ca, w_ref[0], precision=HIGHEST, preferred_element_type=F32) + b_ref[0]


def _ada_call(c_pad, w_ada, b_ada):
    depth, _, ncol = w_ada.shape
    return pl.pallas_call(
        _ada_kernel,
        grid=(depth, ncol // ADA_TN),
        in_specs=[
            pl.BlockSpec((8, D), lambda l, j: (0, 0)),
            pl.BlockSpec((1, D, ADA_TN), lambda l, j: (l, 0, j)),
            pl.BlockSpec((1, 1, ADA_TN), lambda l, j: (l, 0, j)),
        ],
        out_specs=pl.BlockSpec((1, 8, ADA_TN), lambda l, j: (l, 0, j)),
        out_shape=jax.ShapeDtypeStruct((depth, 8, ncol), F32),
        compiler_params=pltpu.CompilerParams(vmem_limit_bytes=VMEM_LIMIT),
        name="ada_mod",
    )(c_pad, w_ada, b_ada.reshape(depth, 1, ncol))


def _rope_kernel(pos_ref, freq_ref, cos_ref, sin_ref):
    ang = pos_ref[...].astype(F32) * freq_ref[...]
    cos_ref[...] = jnp.cos(ang)
    sin_ref[...] = jnp.sin(ang)


def _rope_call(pos_col, inv_freq):
    n = pos_col.shape[0]
    half = inv_freq.shape[1]
    return pl.pallas_call(
        _rope_kernel,
        grid=(n // ROPE_TM,),
        in_specs=[
            pl.BlockSpec((ROPE_TM, 1), lambda i: (i, 0)),
            pl.BlockSpec((1, half), lambda i: (0, 0)),
        ],
        out_specs=[
            pl.BlockSpec((ROPE_TM, half), lambda i: (i, 0)),
            pl.BlockSpec((ROPE_TM, half), lambda i: (i, 0)),
        ],
        out_shape=[jax.ShapeDtypeStruct((n, half), F32)] * 2,
        name="rope_tables",
    )(pos_col, inv_freq)


def _inproj_kernel(x_ref, shift_ref, scale_ref, nw_ref, cos_ref, sin_ref, w_hbm, o_ref, w_vmem, sem):
    @pl.when(pl.program_id(0) == 0)
    def _():
        cp = pltpu.make_async_copy(w_hbm, w_vmem, sem)
        cp.start()
        cp.wait()

    x = x_ref[...]
    ms = jnp.mean(x * x, axis=-1, keepdims=True)
    h = x * lax.rsqrt(ms + NORM_EPS) * nw_ref[...]
    h = (h * (1.0 + scale_ref[0]) + shift_ref[0]).astype(BF16)
    cos = cos_ref[...]
    sin = sin_ref[...]
    half = HD // 2

    for j in range(PROJ // INP_TN):
        c0 = j * INP_TN
        acc = jnp.dot(h, w_vmem[:, c0:c0 + INP_TN], preferred_element_type=F32)
        sec = c0 // RW
        if sec in (0, 1):
            scale = 1.0 if sec == 0 else HD ** -0.5
            for hh in range(INP_TN // HD):
                a = acc[:, hh * HD:hh * HD + half]
                b = acc[:, hh * HD + half:(hh + 1) * HD]
                o_ref[:, c0 + hh * HD:c0 + hh * HD + half] = ((a * cos - b * sin) * scale).astype(BF16)
                o_ref[:, c0 + hh * HD + half:c0 + (hh + 1) * HD] = ((b * cos + a * sin) * scale).astype(BF16)
        elif sec == 2:
            o_ref[:, c0:c0 + INP_TN] = acc.astype(BF16)
        elif sec == 3:
            o_ref[:, c0:c0 + INP_TN] = (acc * _sigmoid(acc)).astype(BF16)
        elif sec == 4:
            o_ref[:, c0:c0 + INP_TN] = _gelu_tanh(acc).astype(BF16)
        else:
            for gg in range(INP_TN // GD):
                t = _gelu_tanh(acc[:, gg * GD:(gg + 1) * GD])
                ms_g = jnp.mean(t * t, axis=-1, keepdims=True)
                o_ref[:, c0 + gg * GD:c0 + (gg + 1) * GD] = (t * lax.rsqrt(ms_g + NORM_EPS)).astype(BF16)


def _inproj_call(x2, shift, scale, nw, w_bf, cos, sin, seq):
    n = x2.shape[0]
    tiles_per_batch = seq // INP_TM
    bmap = lambda i: (i // tiles_per_batch, 0, 0)
    return pl.pallas_call(
        _inproj_kernel,
        grid=(n // INP_TM,),
        in_specs=[
            pl.BlockSpec((INP_TM, D), lambda i: (i, 0)),
            pl.BlockSpec((1, 1, D), bmap),
            pl.BlockSpec((1, 1, D), bmap),
            pl.BlockSpec((1, D), lambda i: (0, 0)),
            pl.BlockSpec((INP_TM, HD // 2), lambda i: (i, 0)),
            pl.BlockSpec((INP_TM, HD // 2), lambda i: (i, 0)),
            pl.BlockSpec(memory_space=pl.ANY),
        ],
        out_specs=pl.BlockSpec((INP_TM, PROJ), lambda i: (i, 0)),
        out_shape=jax.ShapeDtypeStruct((n, PROJ), BF16),
        scratch_shapes=[pltpu.VMEM((D, PROJ), BF16), pltpu.SemaphoreType.DMA(())],
        compiler_params=pltpu.CompilerParams(
            dimension_semantics=("arbitrary",), vmem_limit_bytes=BIG_VMEM_LIMIT),
        name="in_proj",
    )(x2, shift, scale, nw, cos, sin, w_bf)


def _mix_kernel(cd_ref, q_ref, k_ref, v_ref, g_ref, u_ref, vs_ref, dm_ref, qd_ref, kd_ref,
                rnw_ref, ws_ref, bs_ref, o_ref, state_ref):
    @pl.when(pl.program_id(1) == 0)
    def _():
        state_ref[...] = jnp.zeros_like(state_ref)

    row = lax.broadcasted_iota(jnp.int32, (CH, CH), 0)
    col = lax.broadcasted_iota(jnp.int32, (CH, CH), 1)
    causal = row >= col

    chunks = [slice(c * CH, (c + 1) * CH) for c in range(MIX_ROWS // CH)]

    for g in range(NG):
        cols = slice(g * GD, (g + 1) * GD)
        wm = jnp.where(causal, ws_ref[g], 0.0).astype(BF16)
        vs_wide = jnp.concatenate([vs_ref[rows, cols] for rows in chunks], axis=1)
        z = jnp.dot(wm, vs_wide, preferred_element_type=F32) + bs_ref[:, g:g + 1]
        for c, rows in enumerate(chunks):
            zc = z[:, c * GD:(c + 1) * GD]
            o_ref[rows, RW + g * GD:RW + (g + 1) * GD] = (u_ref[rows, cols].astype(F32) * zc).astype(BF16)

    for h in range(NH):
        cols = slice(h * HD, (h + 1) * HD)
        st = state_ref[h]
        for rows in chunks:
            q = q_ref[rows, cols]
            k = k_ref[rows, cols]
            v = v_ref[rows, cols]
            s = lax.dot_general(q, k, (((1,), (1,)), ((), ())), preferred_element_type=F32)
            s = s * dm_ref[h]
            intra = jnp.dot(s.astype(BF16), v, preferred_element_type=F32)
            cross = jnp.dot(q, st.astype(BF16), preferred_element_type=F32) * qd_ref[:, h:h + 1]
            kdec = (k.astype(F32) * kd_ref[:, h:h + 1]).astype(BF16)
            upd = lax.dot_general(kdec, v, (((0,), (0,)), ((), ())), preferred_element_type=F32)
            st = st * cd_ref[h] + upd
            o = intra + cross
            ms = jnp.mean(o * o, axis=-1, keepdims=True)
            o = o * lax.rsqrt(ms + NORM_EPS) * rnw_ref[:, cols]
            o_ref[rows, cols] = (o * g_ref[rows, cols].astype(F32)).astype(BF16)
        state_ref[h] = st


def _mix_call(proj, chunk_decay, decay_mask, q_decay_t, k_decay_t, rnw, ws, bs_t, bsz, seq):
    n = proj.shape[0]
    steps = seq // MIX_ROWS

    def colblk(jb):
        return pl.BlockSpec((MIX_ROWS, RW), lambda b, s, cd: (b * steps + s, jb))

    const2 = lambda b, s, cd: (0, 0)
    const3 = lambda b, s, cd: (0, 0, 0)
    grid_spec = pltpu.PrefetchScalarGridSpec(
        num_scalar_prefetch=1,
        grid=(bsz, steps),
        in_specs=[colblk(0), colblk(1), colblk(2), colblk(3), colblk(4), colblk(5),
                  pl.BlockSpec((NH, CH, CH), const3),
                  pl.BlockSpec((CH, NH), const2),
                  pl.BlockSpec((CH, NH), const2),
                  pl.BlockSpec((1, RW), const2),
                  pl.BlockSpec((NG, CH, CH), const3),
                  pl.BlockSpec((CH, NG), const2)],
        out_specs=pl.BlockSpec((MIX_ROWS, D), lambda b, s, cd: (b * steps + s, 0)),
        scratch_shapes=[pltpu.VMEM((NH, HD, HD), F32)],
    )
    return pl.pallas_call(
        _mix_kernel,
        grid_spec=grid_spec,
        out_shape=jax.ShapeDtypeStruct((n, D), BF16),
        compiler_params=pltpu.CompilerParams(
            dimension_semantics=("arbitrary", "arbitrary"), vmem_limit_bytes=VMEM_LIMIT),
        name="retention_spatial_mix",
    )(chunk_decay, proj, proj, proj, proj, proj, proj, decay_mask, q_decay_t, k_decay_t, rnw, ws, bs_t)


def _outproj_kernel(mix_ref, x_ref, gate_ref, shift_ref, scale_ref, nw_ref, wr_ref, rb_ref, w_hbm,
                    x1_ref, pos_ref, seg_ref, gw_ref, be_ref, na_ref, xg_hbm,
                    carry_ref, cur_ref, bev_ref, nfree_ref, w_vmem, hbuf, seg_v, seg_s, tail_v, tail_s, zbuf,
                    sem, ssem, zsem):
    i = pl.program_id(0)
    last = pl.num_programs(0) - 1
    nblk = xg_hbm.shape[0] // (EXP_BM * PK_SUB)

    @pl.when(i == 0)
    def _():
        cp = pltpu.make_async_copy(w_hbm, w_vmem, ssem)
        cp.start()
        carry_ref[...] = jnp.zeros_like(carry_ref)
        cur_ref[...] = jnp.zeros_like(cur_ref)
        nfree_ref[...] = jnp.zeros_like(nfree_ref)
        bev_ref[...] = jnp.full(bev_ref.shape, NE, F32)
        cp.wait()

    y = jnp.dot(mix_ref[...], w_vmem[...], preferred_element_type=F32)
    x1 = x_ref[...] + gate_ref[0] * y
    x1_ref[...] = x1
    ms = jnp.mean(x1 * x1, axis=-1, keepdims=True)
    h2 = x1 * lax.rsqrt(ms + NORM_EPS) * nw_ref[...]
    h2 = h2 * (1.0 + scale_ref[0]) + shift_ref[0]

    def scatter_wait():
        _tile_copy(hbuf, 0, xg_hbm, 0, 2 * OUT_TM, sem).wait()

    nt = (((1,), (1,)), ((), ()))
    wr = wr_ref[...]
    w_hi = wr.astype(BF16)
    w_lo = (wr - w_hi.astype(F32)).astype(BF16)
    h_hi = h2.astype(BF16)
    h_lo = (h2 - h_hi.astype(F32)).astype(BF16)
    p_hi = lax.dot_general(jnp.concatenate([w_hi, w_lo], axis=0), h_hi, nt, preferred_element_type=F32)
    p_lo = lax.dot_general(w_hi, h_lo, nt, preferred_element_type=F32)
    logits = p_hi[:NE] + p_hi[NE:] + p_lo
    scores = _sigmoid(logits)
    biased = scores + rb_ref[...]
    a = [biased[m * NEG:(m + 1) * NEG] for m in range(EPG)]
    sc = [scores[m * NEG:(m + 1) * NEG] for m in range(EPG)]
    p, q = jnp.maximum(a[0], a[1]), jnp.minimum(a[0], a[1])
    r, s = jnp.maximum(a[2], a[3]), jnp.minimum(a[2], a[3])
    group_score = jnp.maximum(p, r) + jnp.maximum(jnp.minimum(p, r), jnp.maximum(q, s))
    gidx = lax.broadcasted_iota(jnp.int32, group_score.shape, 0)
    gmax = jnp.max(group_score, axis=0, keepdims=True)
    g_sel = jnp.min(jnp.where(group_score == gmax, gidx, NEG), axis=0, keepdims=True)
    pick = gidx == g_sel
    v = [jnp.sum(jnp.where(pick, a[m], 0.0), axis=0, keepdims=True) for m in range(EPG)]
    u = [jnp.sum(jnp.where(pick, sc[m], 0.0), axis=0, keepdims=True) for m in range(EPG)]

    def first_argmax(vals):
        m = functools.reduce(jnp.maximum, vals)
        idx = jnp.full(m.shape, EPG - 1, jnp.int32)
        for t in range(EPG - 2, -1, -1):
            idx = jnp.where(vals[t] == m, t, idx)
        return idx

    def select(vals, idx):
        out = vals[EPG - 1]
        for t in range(EPG - 2, -1, -1):
            out = jnp.where(idx == t, vals[t], out)
        return out

    i1 = first_argmax(v)
    v2 = [jnp.where(i1 == t, -jnp.inf, v[t]) for t in range(EPG)]
    i2 = first_argmax(v2)
    s1 = select(u, i1)
    s2 = select(u, i2)
    denom = s1 + s2
    e1 = g_sel * EPG + i1
    e2 = g_sel * EPG + i2
    gw_ref[0:1, :] = s1 / denom
    gw_ref[1:2, :] = s2 / denom

    tm = e1.shape[1]
    eidx = lax.broadcasted_iota(jnp.int32, (NE, tm), 0)
    oh1 = eidx == e1
    oh2 = eidx == e2
    oh = jnp.where(oh1 | oh2, 1.0, 0.0)
    tr = lax.broadcasted_iota(jnp.int32, (tm, tm), 0)
    tc = lax.broadcasted_iota(jnp.int32, (tm, tm), 1)
    earlier = jnp.where(tr < tc, 1.0, 0.0).astype(BF16)
    carry = carry_ref[...]
    local = jnp.dot(oh.astype(BF16), earlier, preferred_element_type=F32)
    rank = local + carry
    count = jnp.sum(oh, axis=1, keepdims=True)
    total = carry + count
    carry_ref[...] = total

    inv_bm = 1.0 / EXP_BM
    blocks_before = jnp.floor((carry + (EXP_BM - 1)) * inv_bm)
    need = jnp.floor((total + (EXP_BM - 1)) * inv_bm) - blocks_before
    er = lax.broadcasted_iota(jnp.int32, (NE, NE), 0)
    ec = lax.broadcasted_iota(jnp.int32, (NE, NE), 1)
    lower = jnp.where(ec < er, 1.0, 0.0).astype(BF16)
    count_hi = jnp.floor(count * (1.0 / 16.0))
    lane_e = lax.broadcasted_iota(jnp.int32, (NE, LANES), 1)
    digits = jnp.where(lane_e == 0, need, jnp.where(lane_e == 1, count_hi, count - 16.0 * count_hi))
    sums = jnp.dot(lower, digits.astype(BF16), preferred_element_type=F32)
    nfree = nfree_ref[...]
    base = nfree + sums[:, 0:1]
    start = 16.0 * sums[:, 1:2] + sums[:, 2:3]
    cur = cur_ref[...]
    cur_ref[...] = jnp.where(need > 0, base + need - 1.0, cur)
    nfree_new = nfree + jnp.sum(need, axis=0, keepdims=True)
    nfree_ref[...] = nfree_new
    blk = lax.broadcasted_iota(jnp.int32, (NE, bev_ref.shape[1]), 1).astype(F32)
    hit = jnp.logical_and(blk >= base, blk < base + need)
    owner = jnp.sum(jnp.where(hit, eidx[:, 0:1].astype(F32), 0.0), axis=0, keepdims=True)
    taken = jnp.sum(jnp.where(hit, 1.0, 0.0), axis=0, keepdims=True)
    bev = jnp.where(taken > 0, owner, bev_ref[...])
    bev_ref[...] = bev
    be_ref[...] = bev.astype(jnp.int32)
    na_ref[...] = nfree_new.astype(jnp.int32)

    sorted_all = start + local
    pos1 = jnp.sum(jnp.where(oh1, sorted_all, 0.0), axis=0, keepdims=True).astype(jnp.int32)
    pos2 = jnp.sum(jnp.where(oh2, sorted_all, 0.0), axis=0, keepdims=True).astype(jnp.int32)
    pos_ref[0, 0:1, :] = pos1
    pos_ref[0, 1:2, :] = pos2

    @pl.when(i >= 1)
    def _():
        scatter_wait()

    for part in range(2):
        prow = lax.broadcasted_iota(jnp.int32, (tm, tm), 0) + part * tm
        perm = jnp.where((prow == pos1) | (prow == pos2), 1.0, 0.0).astype(BF16)
        sorted_rows = jnp.dot(perm, h_hi, preferred_element_type=F32)
        _pack_rows(sorted_rows, hbuf.at[pl.ds(part * tm * PK_SUB, tm * PK_SUB)], tm)

    as_row = lambda col: jnp.sum(jnp.where(lane_e == eidx[:, 0:1], col, 0.0), axis=0, keepdims=True)
    j0 = jnp.floor(carry * inv_bm)
    in_block = carry - j0 * EXP_BM
    n_a = jnp.minimum(count, EXP_BM - in_block)
    block_of = lambda j: jnp.where(j < blocks_before, cur, base + (j - blocks_before))
    seg_rows = [start, n_a, block_of(j0) * EXP_BM + in_block, count - n_a, block_of(j0 + 1.0) * EXP_BM]
    seg_v[...] = jnp.concatenate(
        [as_row(col).astype(jnp.int32) for col in seg_rows] + [jnp.zeros((3, LANES), jnp.int32)], axis=0)
    seg_ref[0] = seg_v[...]
    to_smem = pltpu.make_async_copy(seg_v, seg_s, ssem)
    to_smem.start()
    to_smem.wait()

    def scatter(e, carry_):
        n_first = seg_s[1, e]
        _start_run(hbuf, seg_s[0, e], xg_hbm, seg_s[2, e], n_first, sem)
        n_rest = seg_s[3, e]

        @pl.when(n_rest > 0)
        def _():
            _start_run(hbuf, seg_s[0, e] + n_first, xg_hbm, seg_s[4, e], n_rest, sem)

        return carry_

    lax.fori_loop(0, NE, scatter, 0)

    @pl.when(i == last)
    def _():
        scatter_wait()
        lane1 = lax.broadcasted_iota(jnp.int32, (1, LANES), 1)
        cur_row = jnp.where(lane1 == NE, nfree_new, as_row(cur_ref[...]))
        tail_v[0:1, :] = cur_row.astype(jnp.int32)
        tail_v[1:2, :] = as_row(total).astype(jnp.int32)
        cp = pltpu.make_async_copy(tail_v, tail_s, ssem)
        cp.start()
        cp.wait()
        zbuf[...] = jnp.zeros_like(zbuf)

        def fills(wait):
            def per_expert(e, carry_):
                used = tail_s[1, e] & (EXP_BM - 1)
                first = tail_s[0, e] * EXP_BM + used
                npad = jnp.where(used > 0, EXP_BM - used, 0)

                def chunk(size):
                    off = npad & ~(2 * size - 1)

                    @pl.when((npad & size) != 0)
                    def _():
                        cpz = _tile_copy(zbuf, 0, xg_hbm, first + off, size, zsem)
                        cpz.wait() if wait else cpz.start()

                _pad_chunks(chunk)
                return carry_

            lax.fori_loop(0, NE, per_expert, 0)

            def per_block(b, carry_):
                cpz = _tile_copy(zbuf, 0, xg_hbm, b * EXP_BM, EXP_BM, zsem)
                cpz.wait() if wait else cpz.start()
                return carry_

            lax.fori_loop(tail_s[0, NE], nblk, per_block, 0)

        fills(False)
        fills(True)


def _outproj_call(mix, x2, w_bf, gate, shift, scale, nw, wr_t, rb_col, seq, nb):
    n = x2.shape[0]
    tiles_per_batch = seq // OUT_TM
    bmap = lambda i: (i // tiles_per_batch, 0, 0)
    row = lambda i: (i, 0)
    lane = lambda i: (0, i)
    const = lambda i: (0, 0)
    table = lambda i: (i, 0, 0)
    return pl.pallas_call(
        _outproj_kernel,
        grid=(n // OUT_TM,),
        in_specs=[
            pl.BlockSpec((OUT_TM, D), row),
            pl.BlockSpec((OUT_TM, D), row),
            pl.BlockSpec((1, 1, D), bmap),
            pl.BlockSpec((1, 1, D), bmap),
            pl.BlockSpec((1, 1, D), bmap),
            pl.BlockSpec((1, D), const),
            pl.BlockSpec((NE, D), const),
            pl.BlockSpec((NE, 1), const),
            pl.BlockSpec(memory_space=pl.ANY),
        ],
        out_specs=[
            pl.BlockSpec((OUT_TM, D), row),
            pl.BlockSpec((1, 2, OUT_TM), table),
            pl.BlockSpec((1, 8, LANES), table),
            pl.BlockSpec((2, OUT_TM), lane),
            pl.BlockSpec((1, BLK_LANES), const),
            pl.BlockSpec((1, 1), const),
            pl.BlockSpec(memory_space=pl.ANY),
        ],
        out_shape=[
            jax.ShapeDtypeStruct((n, D), F32),
            jax.ShapeDtypeStruct((n // OUT_TM, 2, OUT_TM), jnp.int32),
            jax.ShapeDtypeStruct((n // OUT_TM, 8, LANES), jnp.int32),
            jax.ShapeDtypeStruct((2, n), F32),
            jax.ShapeDtypeStruct((1, BLK_LANES), jnp.int32),
            jax.ShapeDtypeStruct((1, 1), jnp.int32),
            jax.ShapeDtypeStruct((nb * EXP_BM * PK_SUB, LANES), jnp.uint32),
        ],
        scratch_shapes=[
            pltpu.VMEM((NE, 1), F32),
            pltpu.VMEM((NE, 1), F32),
            pltpu.VMEM((1, BLK_LANES), F32),
            pltpu.VMEM((1, 1), F32),
            pltpu.VMEM((D, D), BF16),
            pltpu.VMEM((2 * OUT_TM * PK_SUB, LANES), jnp.uint32),
            pltpu.VMEM((8, LANES), jnp.int32),
            pltpu.SMEM((8, LANES), jnp.int32),
            pltpu.VMEM((2, LANES), jnp.int32),
            pltpu.SMEM((2, LANES), jnp.int32),
            pltpu.VMEM((EXP_BM * PK_SUB, LANES), jnp.uint32),
            pltpu.SemaphoreType.DMA(()),
            pltpu.SemaphoreType.DMA(()),
            pltpu.SemaphoreType.DMA(()),
        ],
        compiler_params=pltpu.CompilerParams(
            dimension_semantics=("arbitrary",), vmem_limit_bytes=BIG_VMEM_LIMIT),
        name="out_proj_router",
    )(mix, x2, gate, shift, scale, nw, wr_t, rb_col, w_bf)


def _pack_rows(v, o_ref, rows):
    hi = lax.bitcast_convert_type(v[:, :D // 2].astype(BF16).astype(F32), jnp.uint32)
    lo = lax.bitcast_convert_type(v[:, D // 2:].astype(BF16).astype(F32), jnp.uint32)
    packed = hi | (lo >> 16)
    for j in range(PK_SUB):
        o_ref[pl.ds(j, rows, stride=PK_SUB), :] = packed[:, j * LANES:(j + 1) * LANES]


def _unpack_halves(x_ref, rows):
    his, los = [], []
    for j in range(PK_SUB):
        p = x_ref[pl.ds(j, rows, stride=PK_SUB), :]
        his.append(lax.bitcast_convert_type(p & jnp.uint32(0xFFFF0000), F32))
        los.append(lax.bitcast_convert_type(p << 16, F32))
    return his, los


def _tile_rows(row, nrows):
    start = row * PK_SUB
    if not isinstance(row, int):
        start = pl.multiple_of(start, PK_SUB)
    return pl.ds(start, nrows * PK_SUB)


def _tile_copy(src, src_row, dst, dst_row, nrows, sem):
    return pltpu.make_async_copy(src.at[_tile_rows(src_row, nrows), :], dst.at[_tile_rows(dst_row, nrows), :], sem)


def _start_run(src, src_row, dst, dst_row, n, sem):
    def piece(c, carry):
        _tile_copy(src, src_row + c * RUN_CHUNK, dst, dst_row + c * RUN_CHUNK, RUN_CHUNK, sem).start()
        return carry

    lax.fori_loop(0, n // RUN_CHUNK, piece, 0)
    size = RUN_CHUNK // 2
    while size >= 1:
        off = n & ~(2 * size - 1)

        @pl.when((n & size) != 0)
        def _(size=size, off=off):
            _tile_copy(src, src_row + off, dst, dst_row + off, size, sem).start()

        size //= 2


def _pad_chunks(fn):
    size = EXP_BM // 2
    while size >= 1:
        fn(size)
        size //= 2


def _expert_kernel(order_ref, be_ref, na_ref, x_ref, w1_ref, w3_ref, w2_ref, o_ref, w1b, w3b, w2b):
    i = pl.program_id(0)
    active = i < na_ref[0]
    changed = jnp.logical_or(i == 0, be_ref[i] != be_ref[jnp.maximum(i - 1, 0)])

    @pl.when(jnp.logical_and(active, changed))
    def _():
        w1b[...] = w1_ref[0, 0].astype(BF16)
        w3b[...] = w3_ref[0, 0].astype(BF16)
        w2b[...] = w2_ref[0, 0].astype(BF16)

    @pl.when(active)
    def _():
        rows = EXP_BM // EXP_SUB
        for s in range(EXP_SUB):
            part = pl.ds(s * rows * PK_SUB, rows * PK_SUB)
            his, los = _unpack_halves(x_ref.at[part], rows)
            xb = jnp.concatenate([c.astype(BF16) for c in his + los], axis=1)
            h1 = jnp.dot(xb, w1b[...], preferred_element_type=F32)
            h3 = jnp.dot(xb, w3b[...], preferred_element_type=F32)
            act = (h1 * _sigmoid(h1) * h3).astype(BF16)
            _pack_rows(jnp.dot(act, w2b[...], preferred_element_type=F32), o_ref.at[part], rows)

    @pl.when(jnp.logical_not(active))
    def _():
        o_ref[...] = jnp.zeros_like(o_ref)


def _expert_call(order, block_expert, n_active, xg, w1, w3, w2, layer):
    nb = order.shape[0]
    live = lambda i, od, be, na: (od[jnp.minimum(i, na[0] - 1)], 0)
    wmap = lambda i, od, be, na: (layer, be[jnp.minimum(i, na[0] - 1)], 0, 0)
    grid_spec = pltpu.PrefetchScalarGridSpec(
        num_scalar_prefetch=3,
        grid=(nb,),
        in_specs=[
            pl.BlockSpec((EXP_BM * PK_SUB, LANES), live),
            pl.BlockSpec((1, 1, D, FF), wmap),
            pl.BlockSpec((1, 1, D, FF), wmap),
            pl.BlockSpec((1, 1, FF, D), wmap),
        ],
        out_specs=pl.BlockSpec((EXP_BM * PK_SUB, LANES), lambda i, od, be, na: (od[i], 0)),
        scratch_shapes=[pltpu.VMEM((D, FF), BF16), pltpu.VMEM((D, FF), BF16), pltpu.VMEM((FF, D), BF16)],
    )
    return pl.pallas_call(
        _expert_kernel,
        grid_spec=grid_spec,
        out_shape=jax.ShapeDtypeStruct((nb * EXP_BM * PK_SUB, LANES), jnp.uint32),
        compiler_params=pltpu.CompilerParams(
            dimension_semantics=("arbitrary",), vmem_limit_bytes=VMEM_LIMIT),
        name="expert_blocks",
    )(order, block_expert, n_active, xg, w1, w3, w2)


def _combine_kernel(seg_ref, seg_next_ref, x_ref, gate_ref, gw_ref, pos_ref, fw_ref, y_hbm, o_ref, ybuf, sem, *,
                    final_norm):
    i = pl.program_id(0)
    last = pl.num_programs(0) - 1

    def fetch(table, parity):
        dst = ybuf.at[parity]

        def gather(e, carry):
            n_first = table[0, 1, e]
            _start_run(y_hbm, table[0, 2, e], dst, table[0, 0, e], n_first, sem.at[parity])
            n_rest = table[0, 3, e]

            @pl.when(n_rest > 0)
            def _():
                _start_run(y_hbm, table[0, 4, e], dst, table[0, 0, e] + n_first, n_rest, sem.at[parity])

            return carry

        lax.fori_loop(0, NE, gather, 0)

    par = i % 2

    @pl.when(i == 0)
    def _():
        fetch(seg_ref, 0)

    @pl.when(i < last)
    def _():
        fetch(seg_next_ref, 1 - par)

    cur = ybuf.at[par]
    _tile_copy(y_hbm, 0, cur, 0, 2 * CMB_TM, sem.at[par]).wait()

    his, los = _unpack_halves(cur, 2 * CMB_TM)
    ys = jnp.concatenate([c.astype(BF16) for c in his + los], axis=1)
    gw = gw_ref[...]
    pos = pos_ref[...]
    col = lax.broadcasted_iota(jnp.int32, (CMB_TM, 2 * CMB_TM), 1)
    selector = jnp.where(col == pos[:, 0:1], gw[:, 0:1], jnp.where(col == pos[:, 1:2], gw[:, 1:2], 0.0))
    moe = jnp.dot(selector.astype(BF16), ys, preferred_element_type=F32)
    x2 = x_ref[...] + gate_ref[0] * moe
    if final_norm:
        ms = jnp.mean(x2 * x2, axis=-1, keepdims=True)
        x2 = x2 * lax.rsqrt(ms + NORM_EPS) * fw_ref[...]
    o_ref[...] = x2


def _combine_call(runs, x1, gate, gw_col, pos_col, fw, yg, seq, final_norm):
    n = x1.shape[0]
    tiles_per_batch = seq // CMB_TM
    nt = n // CMB_TM
    return pl.pallas_call(
        functools.partial(_combine_kernel, final_norm=final_norm),
        grid=(nt,),
        in_specs=[
            pl.BlockSpec((1, 8, LANES), lambda i: (i, 0, 0), memory_space=pltpu.SMEM),
            pl.BlockSpec((1, 8, LANES), lambda i: (jnp.minimum(i + 1, nt - 1), 0, 0), memory_space=pltpu.SMEM),
            pl.BlockSpec((CMB_TM, D), lambda i: (i, 0)),
            pl.BlockSpec((1, 1, D), lambda i: (i // tiles_per_batch, 0, 0)),
            pl.BlockSpec((CMB_TM, 2), lambda i: (i, 0)),
            pl.BlockSpec((CMB_TM, 2), lambda i: (i, 0)),
            pl.BlockSpec((1, D), lambda i: (0, 0)),
            pl.BlockSpec(memory_space=pl.ANY),
        ],
        out_specs=pl.BlockSpec((CMB_TM, D), lambda i: (i, 0)),
        out_shape=jax.ShapeDtypeStruct((n, D), F32),
        scratch_shapes=[pltpu.VMEM((2, 2 * CMB_TM * PK_SUB, LANES), jnp.uint32), pltpu.SemaphoreType.DMA((2,))],
        compiler_params=pltpu.CompilerParams(
            dimension_semantics=("arbitrary",), vmem_limit_bytes=VMEM_LIMIT),
        name="moe_combine",
    )(runs, runs, x1, gate, gw_col, pos_col, fw, yg)


def _decay_tables():
    log_gamma = jnp.log1p(-jnp.exp2(-5.0 - jnp.arange(NH, dtype=F32)))
    pos = jnp.arange(CH, dtype=F32)
    diff = pos[:, None] - pos[None, :]
    mask = jnp.where(diff >= 0, jnp.exp(log_gamma[:, None, None] * jnp.maximum(diff, 0.0)), 0.0)
    q_decay = jnp.exp(log_gamma[:, None] * (pos + 1.0))
    k_decay = jnp.exp(log_gamma[:, None] * (CH - 1.0 - pos))
    chunk_decay = jnp.exp(log_gamma * CH)
    return mask.astype(F32), q_decay.T, k_decay.T, chunk_decay


def kernel(x, c, positions, w_ada, b_ada, norm1_w, norm2_w, w_in, w_out, ret_norm_w, sg_w_s, sg_b_s,
           w_router, router_bias, w1, w3, w2, final_norm_w):
    bsz, seq, d = x.shape
    n = bsz * seq
    assert d == D and seq % INP_TM == 0 and seq % MIX_ROWS == 0 and n % CMB_TM == 0

    c_pad = jnp.zeros((8, D), F32).at[:bsz].set(c)
    mod = _ada_call(c_pad, w_ada, b_ada)[:, :bsz]
    mod = mod.reshape(N_LAYERS, bsz, N_MOD, 1, D)

    half = HD // 2
    inv_freq = (ROPE_THETA ** (-jnp.arange(half, dtype=F32) / half)).reshape(1, half)
    cos, sin = _rope_call(positions.reshape(n, 1), inv_freq)
    decay_mask, q_decay_t, k_decay_t, chunk_decay = _decay_tables()

    perm = (jnp.arange(NE) % NEG) * EPG + jnp.arange(NE) // NEG
    wr_t = w_router.T[perm]
    rb_col = router_bias[perm].reshape(NE, 1)

    nb = n * 2 // EXP_BM + NE
    assert nb <= BLK_LANES and n % OUT_TM == 0
    xs = x.reshape(n, D)
    for l in range(N_LAYERS):
        shift1, scale1, gate1, shift2, scale2, gate2 = [mod[l, :, t] for t in range(N_MOD)]
        proj = _inproj_call(xs, shift1, scale1, norm1_w[l].reshape(1, D), w_in[l].astype(BF16), cos, sin, seq)
        mix = _mix_call(proj, chunk_decay, decay_mask, q_decay_t, k_decay_t, ret_norm_w[l].reshape(1, RW),
                        sg_w_s[l], sg_b_s[l].T, bsz, seq)
        x1, sorted_pos, runs, gw, owner, n_active, xg = _outproj_call(
            mix, xs, w_out[l].astype(BF16), gate1, shift2, scale2, norm2_w[l].reshape(1, D), wr_t, rb_col, seq, nb)

        owner = owner.reshape(BLK_LANES)
        ids = jnp.arange(BLK_LANES, dtype=jnp.int32)
        ahead = (owner[None, :] < owner[:, None]) | ((owner[None, :] == owner[:, None]) & (ids[None, :] < ids[:, None]))
        pos = jnp.sum(ahead, axis=1)
        at = pos[None, :] == ids[:, None]
        order = jnp.sum(jnp.where(at, ids[None, :], 0), axis=1)[:nb].astype(jnp.int32)
        block_expert = jnp.minimum(jnp.sum(jnp.where(at, owner[None, :], 0), axis=1)[:nb], NE - 1).astype(jnp.int32)

        yg = _expert_call(order, block_expert, n_active.reshape(1), xg, w1, w3, w2, l)
        pos_col = sorted_pos.transpose(0, 2, 1).reshape(n, 2)
        xs = _combine_call(runs, x1, gate2, gw.T, pos_col, final_norm_w.reshape(1, D), yg, seq,
                           final_norm=(l == N_LAYERS - 1))
    return xs.reshape(bsz, seq, D)
```

```python
import functools

import jax
import jax.numpy as jnp
from jax import lax
from jax.experimental import pallas as pl
from jax.experimental.pallas import tpu as pltpu

F32 = jnp.float32
BF16 = jnp.bfloat16
HIGHEST = lax.Precision.HIGHEST

D = 2048
N_LAYERS = 2
RW = D // 2
NH = 4
HD = RW // NH
CH = 128
ROPE_THETA = 10000.0
SGW = D - RW
NG = 8
GD = SGW // NG
PROJ = 4 * RW + 2 * SGW
NE = 32
NEG = 8
EPG = NE // NEG
FF = 512
N_MOD = 6
NORM_EPS = 1e-6

VMEM_LIMIT = 48 * 1024 * 1024
BIG_VMEM_LIMIT = 56 * 1024 * 1024

ADA_TN = 1024
ROPE_TM = 2048
INP_TM = 256
INP_TN = 512
MIX_ROWS = 1024
OUT_TM = 512
RUN_UNROLL = 4
RUN_SHORT = 64
EXP_BM = 512
EXP_SUB = 2
CMB_TM = OUT_TM
BLK_LANES = 256

LANES = 128
PK_SUB = D // 2 // LANES


def _sigmoid(v):
    return 1.0 / (1.0 + jnp.exp(-v))


def _gelu_tanh(v):
    return 0.5 * v * (1.0 + jnp.tanh(0.7978845608028654 * (v + 0.044715 * (v * v * v))))


def _ada_kernel(c_ref, w_ref, b_ref, o_ref):
    c = c_ref[...]
    ca = c * _sigmoid(c)
    o_ref[0] = jnp.dot(ca, w_ref[0], precision=HIGHEST, preferred_element_type=F32) + b_ref[0]


def _ada_call(c_pad, w_ada, b_ada):
    depth, _, ncol = w_ada.shape
    return pl.pallas_call(
        _ada_kernel,
        grid=(depth, ncol // ADA_TN),
        in_specs=[
            pl.BlockSpec((8, D), lambda l, j: (0, 0)),
            pl.BlockSpec((1, D, ADA_TN), lambda l, j: (l, 0, j)),
            pl.BlockSpec((1, 1, ADA_TN), lambda l, j: (l, 0, j)),
        ],
        out_specs=pl.BlockSpec((1, 8, ADA_TN), lambda l, j: (l, 0, j)),
        out_shape=jax.ShapeDtypeStruct((depth, 8, ncol), F32),
        compiler_params=pltpu.CompilerParams(vmem_limit_bytes=VMEM_LIMIT),
        name="ada_mod",
    )(c_pad, w_ada, b_ada.reshape(depth, 1, ncol))


def _rope_kernel(pos_ref, freq_ref, cos_ref, sin_ref):
    ang = pos_ref[...].astype(F32) * freq_ref[...]
    cos_ref[...] = jnp.cos(ang)
    sin_ref[...] = jnp.sin(ang)


def _rope_call(pos_col, inv_freq):
    n = pos_col.shape[0]
    half = inv_freq.shape[1]
    return pl.pallas_call(
        _rope_kernel,
        grid=(n // ROPE_TM,),
        in_specs=[
            pl.BlockSpec((ROPE_TM, 1), lambda i: (i, 0)),
            pl.BlockSpec((1, half), lambda i: (0, 0)),
        ],
        out_specs=[
            pl.BlockSpec((ROPE_TM, half), lambda i: (i, 0)),
            pl.BlockSpec((ROPE_TM, half), lambda i: (i, 0)),
        ],
        out_shape=[jax.ShapeDtypeStruct((n, half), F32)] * 2,
        name="rope_tables",
    )(pos_col, inv_freq)


def _inproj_kernel(x_ref, shift_ref, scale_ref, nw_ref, cos_ref, sin_ref, w_hbm, o_ref, w_vmem, sem):
    @pl.when(pl.program_id(0) == 0)
    def _():
        cp = pltpu.make_async_copy(w_hbm, w_vmem, sem)
        cp.start()
        cp.wait()

    x = x_ref[...]
    ms = jnp.mean(x * x, axis=-1, keepdims=True)
    h = x * lax.rsqrt(ms + NORM_EPS) * nw_ref[...]
    h = (h * (1.0 + scale_ref[0]) + shift_ref[0]).astype(BF16)
    cos = cos_ref[...]
    sin = sin_ref[...]
    half = HD // 2

    for j in range(PROJ // INP_TN):
        c0 = j * INP_TN
        acc = jnp.dot(h, w_vmem[:, c0:c0 + INP_TN], preferred_element_type=F32)
        sec = c0 // RW
        if sec in (0, 1):
            scale = 1.0 if sec == 0 else HD ** -0.5
            for hh in range(INP_TN // HD):
                a = acc[:, hh * HD:hh * HD + half]
                b = acc[:, hh * HD + half:(hh + 1) * HD]
                o_ref[:, c0 + hh * HD:c0 + hh * HD + half] = ((a * cos - b * sin) * scale).astype(BF16)
                o_ref[:, c0 + hh * HD + half:c0 + (hh + 1) * HD] = ((b * cos + a * sin) * scale).astype(BF16)
        elif sec == 2:
            o_ref[:, c0:c0 + INP_TN] = acc.astype(BF16)
        elif sec == 3:
            o_ref[:, c0:c0 + INP_TN] = (acc * _sigmoid(acc)).astype(BF16)
        elif sec == 4:
            o_ref[:, c0:c0 + INP_TN] = _gelu_tanh(acc).astype(BF16)
        else:
            for gg in range(INP_TN // GD):
                t = _gelu_tanh(acc[:, gg * GD:(gg + 1) * GD])
                ms_g = jnp.mean(t * t, axis=-1, keepdims=True)
                o_ref[:, c0 + gg * GD:c0 + (gg + 1) * GD] = (t * lax.rsqrt(ms_g + NORM_EPS)).astype(BF16)


def _inproj_call(x2, shift, scale, nw, w_bf, cos, sin, seq):
    n = x2.shape[0]
    tiles_per_batch = seq // INP_TM
    bmap = lambda i: (i // tiles_per_batch, 0, 0)
    return pl.pallas_call(
        _inproj_kernel,
        grid=(n // INP_TM,),
        in_specs=[
            pl.BlockSpec((INP_TM, D), lambda i: (i, 0)),
            pl.BlockSpec((1, 1, D), bmap),
            pl.BlockSpec((1, 1, D), bmap),
            pl.BlockSpec((1, D), lambda i: (0, 0)),
            pl.BlockSpec((INP_TM, HD // 2), lambda i: (i, 0)),
            pl.BlockSpec((INP_TM, HD // 2), lambda i: (i, 0)),
            pl.BlockSpec(memory_space=pl.ANY),
        ],
        out_specs=pl.BlockSpec((INP_TM, PROJ), lambda i: (i, 0)),
        out_shape=jax.ShapeDtypeStruct((n, PROJ), BF16),
        scratch_shapes=[pltpu.VMEM((D, PROJ), BF16), pltpu.SemaphoreType.DMA(())],
        compiler_params=pltpu.CompilerParams(
            dimension_semantics=("arbitrary",), vmem_limit_bytes=BIG_VMEM_LIMIT),
        name="in_proj",
    )(x2, shift, scale, nw, cos, sin, w_bf)


def _mix_kernel(cd_ref, q_ref, k_ref, v_ref, g_ref, u_ref, vs_ref, dm_ref, qd_ref, kd_ref,
                rnw_ref, ws_ref, bs_ref, o_ref, state_ref):
    @pl.when(pl.program_id(1) == 0)
    def _():
        state_ref[...] = jnp.zeros_like(state_ref)

    row = lax.broadcasted_iota(jnp.int32, (CH, CH), 0)
    col = lax.broadcasted_iota(jnp.int32, (CH, CH), 1)
    causal = row >= col

    chunks = [slice(c * CH, (c + 1) * CH) for c in range(MIX_ROWS // CH)]

    for g in range(NG):
        cols = slice(g * GD, (g + 1) * GD)
        wm = jnp.where(causal, ws_ref[g], 0.0).astype(BF16)
        vs_wide = jnp.concatenate([vs_ref[rows, cols] for rows in chunks], axis=1)
        z = jnp.dot(wm, vs_wide, preferred_element_type=F32) + bs_ref[:, g:g + 1]
        for c, rows in enumerate(chunks):
            zc = z[:, c * GD:(c + 1) * GD]
            o_ref[rows, RW + g * GD:RW + (g + 1) * GD] = (u_ref[rows, cols].astype(F32) * zc).astype(BF16)

    for h in range(NH):
        cols = slice(h * HD, (h + 1) * HD)
        st = state_ref[h]
        for rows in chunks:
            q = q_ref[rows, cols]
            k = k_ref[rows, cols]
            v = v_ref[rows, cols]
            s = lax.dot_general(q, k, (((1,), (1,)), ((), ())), preferred_element_type=F32)
            s = s * dm_ref[h]
            intra = jnp.dot(s.astype(BF16), v, preferred_element_type=F32)
            cross = jnp.dot(q, st.astype(BF16), preferred_element_type=F32) * qd_ref[:, h:h + 1]
            kdec = (k.astype(F32) * kd_ref[:, h:h + 1]).astype(BF16)
            upd = lax.dot_general(kdec, v, (((0,), (0,)), ((), ())), preferred_element_type=F32)
            st = st * cd_ref[h] + upd
            o = intra + cross
            ms = jnp.mean(o * o, axis=-1, keepdims=True)
            o = o * lax.rsqrt(ms + NORM_EPS) * rnw_ref[:, cols]
            o_ref[rows, cols] = (o * g_ref[rows, cols].astype(F32)).astype(BF16)
        state_ref[h] = st


def _mix_call(proj, chunk_decay, decay_mask, q_decay_t, k_decay_t, rnw, ws, bs_t, bsz, seq):
    n = proj.shape[0]
    steps = seq // MIX_ROWS

    def colblk(jb):
        return pl.BlockSpec((MIX_ROWS, RW), lambda b, s, cd: (b * steps + s, jb))

    const2 = lambda b, s, cd: (0, 0)
    const3 = lambda b, s, cd: (0, 0, 0)
    grid_spec = pltpu.PrefetchScalarGridSpec(
        num_scalar_prefetch=1,
        grid=(bsz, steps),
        in_specs=[colblk(0), colblk(1), colblk(2), colblk(3), colblk(4), colblk(5),
                  pl.BlockSpec((NH, CH, CH), const3),
                  pl.BlockSpec((CH, NH), const2),
                  pl.BlockSpec((CH, NH), const2),
                  pl.BlockSpec((1, RW), const2),
                  pl.BlockSpec((NG, CH, CH), const3),
                  pl.BlockSpec((CH, NG), const2)],
        out_specs=pl.BlockSpec((MIX_ROWS, D), lambda b, s, cd: (b * steps + s, 0)),
        scratch_shapes=[pltpu.VMEM((NH, HD, HD), F32)],
    )
    return pl.pallas_call(
        _mix_kernel,
        grid_spec=grid_spec,
        out_shape=jax.ShapeDtypeStruct((n, D), BF16),
        compiler_params=pltpu.CompilerParams(
            dimension_semantics=("arbitrary", "arbitrary"), vmem_limit_bytes=VMEM_LIMIT),
        name="retention_spatial_mix",
    )(chunk_decay, proj, proj, proj, proj, proj, proj, decay_mask, q_decay_t, k_decay_t, rnw, ws, bs_t)


def _outproj_kernel(mix_ref, x_ref, gate_ref, shift_ref, scale_ref, nw_ref, wr_ref, rb_ref, w_hbm,
                    x1_ref, pos_ref, seg_ref, gw_ref, be_ref, na_ref, xg_hbm,
                    carry_ref, cur_ref, bev_ref, nfree_ref, w_vmem, hbuf, seg_v, seg_s, tail_v, tail_s, zbuf,
                    sem, ssem, zsem):
    i = pl.program_id(0)
    last = pl.num_programs(0) - 1
    nblk = xg_hbm.shape[0] // (EXP_BM * PK_SUB)

    @pl.when(i == 0)
    def _():
        cp = pltpu.make_async_copy(w_hbm, w_vmem, ssem)
        cp.start()
        carry_ref[...] = jnp.zeros_like(carry_ref)
        cur_ref[...] = jnp.zeros_like(cur_ref)
        nfree_ref[...] = jnp.zeros_like(nfree_ref)
        bev_ref[...] = jnp.full(bev_ref.shape, NE, F32)
        cp.wait()

    y = jnp.dot(mix_ref[...], w_vmem[...], preferred_element_type=F32)
    x1 = x_ref[...] + gate_ref[0] * y
    x1_ref[...] = x1
    ms = jnp.mean(x1 * x1, axis=-1, keepdims=True)
    h2 = x1 * lax.rsqrt(ms + NORM_EPS) * nw_ref[...]
    h2 = h2 * (1.0 + scale_ref[0]) + shift_ref[0]

    def scatter_wait():
        _tile_copy(hbuf, 0, xg_hbm, 0, 2 * OUT_TM, sem).wait()

    nt = (((1,), (1,)), ((), ()))
    wr = wr_ref[...]
    w_hi = wr.astype(BF16)
    w_lo = (wr - w_hi.astype(F32)).astype(BF16)
    h_hi = h2.astype(BF16)
    h_lo = (h2 - h_hi.astype(F32)).astype(BF16)
    p_hi = lax.dot_general(jnp.concatenate([w_hi, w_lo], axis=0), h_hi, nt, preferred_element_type=F32)
    p_lo = lax.dot_general(w_hi, h_lo, nt, preferred_element_type=F32)
    logits = p_hi[:NE] + p_hi[NE:] + p_lo
    scores = _sigmoid(logits)
    biased = scores + rb_ref[...]
    a = [biased[m * NEG:(m + 1) * NEG] for m in range(EPG)]
    sc = [scores[m * NEG:(m + 1) * NEG] for m in range(EPG)]
    p, q = jnp.maximum(a[0], a[1]), jnp.minimum(a[0], a[1])
    r, s = jnp.maximum(a[2], a[3]), jnp.minimum(a[2], a[3])
    group_score = jnp.maximum(p, r) + jnp.maximum(jnp.minimum(p, r), jnp.maximum(q, s))
    gidx = lax.broadcasted_iota(jnp.int32, group_score.shape, 0)
    gmax = jnp.max(group_score, axis=0, keepdims=True)
    g_sel = jnp.min(jnp.where(group_score == gmax, gidx, NEG), axis=0, keepdims=True)
    pick = gidx == g_sel
    v = [jnp.sum(jnp.where(pick, a[m], 0.0), axis=0, keepdims=True) for m in range(EPG)]
    u = [jnp.sum(jnp.where(pick, sc[m], 0.0), axis=0, keepdims=True) for m in range(EPG)]

    def first_argmax(vals):
        m = functools.reduce(jnp.maximum, vals)
        idx = jnp.full(m.shape, EPG - 1, jnp.int32)
        for t in range(EPG - 2, -1, -1):
            idx = jnp.where(vals[t] == m, t, idx)
        return idx

    def select(vals, idx):
        out = vals[EPG - 1]
        for t in range(EPG - 2, -1, -1):
            out = jnp.where(idx == t, vals[t], out)
        return out

    i1 = first_argmax(v)
    v2 = [jnp.where(i1 == t, -jnp.inf, v[t]) for t in range(EPG)]
    i2 = first_argmax(v2)
    s1 = select(u, i1)
    s2 = select(u, i2)
    denom = s1 + s2
    e1 = g_sel * EPG + i1
    e2 = g_sel * EPG + i2
    gw_ref[0:1, :] = s1 / denom
    gw_ref[1:2, :] = s2 / denom

    tm = e1.shape[1]
    eidx = lax.broadcasted_iota(jnp.int32, (NE, tm), 0)
    oh1 = eidx == e1
    oh2 = eidx == e2
    oh = jnp.where(oh1 | oh2, 1.0, 0.0)
    tr = lax.broadcasted_iota(jnp.int32, (tm, tm), 0)
    tc = lax.broadcasted_iota(jnp.int32, (tm, tm), 1)
    earlier = jnp.where(tr < tc, 1.0, 0.0).astype(BF16)
    carry = carry_ref[...]
    local = jnp.dot(oh.astype(BF16), earlier, preferred_element_type=F32)
    rank = local + carry
    count = jnp.sum(oh, axis=1, keepdims=True)
    total = carry + count
    carry_ref[...] = total

    inv_bm = 1.0 / EXP_BM
    blocks_before = jnp.floor((carry + (EXP_BM - 1)) * inv_bm)
    need = jnp.floor((total + (EXP_BM - 1)) * inv_bm) - blocks_before
    er = lax.broadcasted_iota(jnp.int32, (NE, NE), 0)
    ec = lax.broadcasted_iota(jnp.int32, (NE, NE), 1)
    lower = jnp.where(ec < er, 1.0, 0.0).astype(BF16)
    count_hi = jnp.floor(count * (1.0 / 16.0))
    lane_e = lax.broadcasted_iota(jnp.int32, (NE, LANES), 1)
    digits = jnp.where(lane_e == 0, need, jnp.where(lane_e == 1, count_hi, count - 16.0 * count_hi))
    sums = jnp.dot(lower, digits.astype(BF16), preferred_element_type=F32)
    nfree = nfree_ref[...]
    base = nfree + sums[:, 0:1]
    start = 16.0 * sums[:, 1:2] + sums[:, 2:3]
    cur = cur_ref[...]
    cur_ref[...] = jnp.where(need > 0, base + need - 1.0, cur)
    nfree_new = nfree + jnp.sum(need, axis=0, keepdims=True)
    nfree_ref[...] = nfree_new
    blk = lax.broadcasted_iota(jnp.int32, (NE, bev_ref.shape[1]), 1).astype(F32)
    hit = jnp.logical_and(blk >= base, blk < base + need)
    owner = jnp.sum(jnp.where(hit, eidx[:, 0:1].astype(F32), 0.0), axis=0, keepdims=True)
    taken = jnp.sum(jnp.where(hit, 1.0, 0.0), axis=0, keepdims=True)
    bev = jnp.where(taken > 0, owner, bev_ref[...])
    bev_ref[...] = bev
    be_ref[...] = bev.astype(jnp.int32)
    na_ref[...] = nfree_new.astype(jnp.int32)

    sorted_all = start + local
    pos1 = jnp.sum(jnp.where(oh1, sorted_all, 0.0), axis=0, keepdims=True).astype(jnp.int32)
    pos2 = jnp.sum(jnp.where(oh2, sorted_all, 0.0), axis=0, keepdims=True).astype(jnp.int32)
    pos_ref[0, 0:1, :] = pos1
    pos_ref[0, 1:2, :] = pos2

    @pl.when(i >= 1)
    def _():
        scatter_wait()

    for part in range(2):
        prow = lax.broadcasted_iota(jnp.int32, (tm, tm), 0) + part * tm
        perm = jnp.where((prow == pos1) | (prow == pos2), 1.0, 0.0).astype(BF16)
        sorted_rows = jnp.dot(perm, h_hi, preferred_element_type=F32)
        _pack_rows(sorted_rows, hbuf.at[pl.ds(part * tm * PK_SUB, tm * PK_SUB)], tm)

    as_row = lambda col: jnp.sum(jnp.where(lane_e == eidx[:, 0:1], col, 0.0), axis=0, keepdims=True)
    j0 = jnp.floor(carry * inv_bm)
    in_block = carry - j0 * EXP_BM
    n_a = jnp.minimum(count, EXP_BM - in_block)
    block_of = lambda j: jnp.where(j < blocks_before, cur, base + (j - blocks_before))
    seg_rows = [start, n_a, block_of(j0) * EXP_BM + in_block, count - n_a, block_of(j0 + 1.0) * EXP_BM]
    seg_v[...] = jnp.concatenate(
        [as_row(col).astype(jnp.int32) for col in seg_rows] + [jnp.zeros((3, LANES), jnp.int32)], axis=0)
    seg_ref[0] = seg_v[...]
    to_smem = pltpu.make_async_copy(seg_v, seg_s, ssem)
    to_smem.start()
    to_smem.wait()

    def scatter(group, carry_):
        _start_tile_runs(seg_s, lambda t, r, e: t[r, e], group, hbuf, xg_hbm, sem, to_slots=True)
        return carry_

    lax.fori_loop(0, NE // RUN_UNROLL, scatter, 0)

    @pl.when(i == last)
    def _():
        scatter_wait()
        lane1 = lax.broadcasted_iota(jnp.int32, (1, LANES), 1)
        cur_row = jnp.where(lane1 == NE, nfree_new, as_row(cur_ref[...]))
        tail_v[0:1, :] = cur_row.astype(jnp.int32)
        tail_v[1:2, :] = as_row(total).astype(jnp.int32)
        cp = pltpu.make_async_copy(tail_v, tail_s, ssem)
        cp.start()
        cp.wait()
        zbuf[...] = jnp.zeros_like(zbuf)

        def fills(wait):
            def per_expert(e, carry_):
                used = tail_s[1, e] & (EXP_BM - 1)
                first = tail_s[0, e] * EXP_BM + used
                npad = jnp.where(used > 0, EXP_BM - used, 0)

                def chunk(size):
                    off = npad & ~(2 * size - 1)

                    @pl.when((npad & size) != 0)
                    def _():
                        cpz = _tile_copy(zbuf, 0, xg_hbm, first + off, size, zsem)
                        cpz.wait() if wait else cpz.start()

                _pad_chunks(chunk)
                return carry_

            lax.fori_loop(0, NE, per_expert, 0)

            def per_block(b, carry_):
                cpz = _tile_copy(zbuf, 0, xg_hbm, b * EXP_BM, EXP_BM, zsem)
                cpz.wait() if wait else cpz.start()
                return carry_

            lax.fori_loop(tail_s[0, NE], nblk, per_block, 0)

        fills(False)
        fills(True)


def _outproj_call(mix, x2, w_bf, gate, shift, scale, nw, wr_t, rb_col, seq, nb):
    n = x2.shape[0]
    tiles_per_batch = seq // OUT_TM
    bmap = lambda i: (i // tiles_per_batch, 0, 0)
    row = lambda i: (i, 0)
    lane = lambda i: (0, i)
    const = lambda i: (0, 0)
    table = lambda i: (i, 0, 0)
    return pl.pallas_call(
        _outproj_kernel,
        grid=(n // OUT_TM,),
        in_specs=[
            pl.BlockSpec((OUT_TM, D), row),
            pl.BlockSpec((OUT_TM, D), row),
            pl.BlockSpec((1, 1, D), bmap),
            pl.BlockSpec((1, 1, D), bmap),
            pl.BlockSpec((1, 1, D), bmap),
            pl.BlockSpec((1, D), const),
            pl.BlockSpec((NE, D), const),
            pl.BlockSpec((NE, 1), const),
            pl.BlockSpec(memory_space=pl.ANY),
        ],
        out_specs=[
            pl.BlockSpec((OUT_TM, D), row),
            pl.BlockSpec((1, 2, OUT_TM), table),
            pl.BlockSpec((1, 8, LANES), table),
            pl.BlockSpec((2, OUT_TM), lane),
            pl.BlockSpec((1, BLK_LANES), const),
            pl.BlockSpec((1, 1), const),
            pl.BlockSpec(memory_space=pl.ANY),
        ],
        out_shape=[
            jax.ShapeDtypeStruct((n, D), F32),
            jax.ShapeDtypeStruct((n // OUT_TM, 2, OUT_TM), jnp.int32),
            jax.ShapeDtypeStruct((n // OUT_TM, 8, LANES), jnp.int32),
            jax.ShapeDtypeStruct((2, n), F32),
            jax.ShapeDtypeStruct((1, BLK_LANES), jnp.int32),
            jax.ShapeDtypeStruct((1, 1), jnp.int32),
            jax.ShapeDtypeStruct((nb * EXP_BM * PK_SUB, LANES), jnp.uint32),
        ],
        scratch_shapes=[
            pltpu.VMEM((NE, 1), F32),
            pltpu.VMEM((NE, 1), F32),
            pltpu.VMEM((1, BLK_LANES), F32),
            pltpu.VMEM((1, 1), F32),
            pltpu.VMEM((D, D), BF16),
            pltpu.VMEM((2 * OUT_TM * PK_SUB, LANES), jnp.uint32),
            pltpu.VMEM((8, LANES), jnp.int32),
            pltpu.SMEM((8, LANES), jnp.int32),
            pltpu.VMEM((2, LANES), jnp.int32),
            pltpu.SMEM((2, LANES), jnp.int32),
            pltpu.VMEM((EXP_BM * PK_SUB, LANES), jnp.uint32),
            pltpu.SemaphoreType.DMA(()),
            pltpu.SemaphoreType.DMA(()),
            pltpu.SemaphoreType.DMA(()),
        ],
        compiler_params=pltpu.CompilerParams(
            dimension_semantics=("arbitrary",), vmem_limit_bytes=BIG_VMEM_LIMIT),
        name="out_proj_router",
    )(mix, x2, gate, shift, scale, nw, wr_t, rb_col, w_bf)


def _pack_rows(v, o_ref, rows):
    hi = lax.bitcast_convert_type(v[:, :D // 2].astype(BF16).astype(F32), jnp.uint32)
    lo = lax.bitcast_convert_type(v[:, D // 2:].astype(BF16).astype(F32), jnp.uint32)
    packed = hi | (lo >> 16)
    for j in range(PK_SUB):
        o_ref[pl.ds(j, rows, stride=PK_SUB), :] = packed[:, j * LANES:(j + 1) * LANES]


def _unpack_halves(x_ref, rows):
    his, los = [], []
    for j in range(PK_SUB):
        p = x_ref[pl.ds(j, rows, stride=PK_SUB), :]
        his.append(lax.bitcast_convert_type(p & jnp.uint32(0xFFFF0000), F32))
        los.append(lax.bitcast_convert_type(p << 16, F32))
    return his, los


def _tile_rows(row, nrows):
    start = row * PK_SUB
    if not isinstance(row, int):
        start = pl.multiple_of(start, PK_SUB)
    return pl.ds(start, nrows * PK_SUB)


def _tile_copy(src, src_row, dst, dst_row, nrows, sem):
    return pltpu.make_async_copy(src.at[_tile_rows(src_row, nrows), :], dst.at[_tile_rows(dst_row, nrows), :], sem)


def _if_rare(cond, body):
    def once(_, carry):
        body()
        return carry

    lax.fori_loop(0, cond.astype(jnp.int32), once, 0)


RUN_SIZES = [OUT_TM >> s for s in range(OUT_TM.bit_length())]
RUN_SIZES_SHORT = [s for s in RUN_SIZES if s < RUN_SHORT]
RUN_SIZES_LONG = [s for s in RUN_SIZES if s >= RUN_SHORT]


def _start_run(src, src_row, dst, dst_row, n, sem, sizes):
    for size in sizes:
        off = n & ~(2 * size - 1)

        @pl.when((n & size) != 0)
        def _(size=size, off=off):
            _tile_copy(src, src_row + off, dst, dst_row + off, size, sem).start()


def _start_tile_runs(table, read, group, src, dst, sem, to_slots):
    runs = []
    for u in range(RUN_UNROLL):
        e = group * RUN_UNROLL + u
        start, n_first, slot_first, n_rest, slot_rest = [read(table, r, e) for r in range(5)]
        first = (start, slot_first) if to_slots else (slot_first, start)
        rest = (start + n_first, slot_rest) if to_slots else (slot_rest, start + n_first)
        runs.append((first, n_first, rest, n_rest))
    for first, n_first, _, _ in runs:
        _start_run(src, first[0], dst, first[1], n_first, sem, RUN_SIZES_SHORT)
    rare = [(nf >= RUN_SHORT) | (nr > 0) for _, nf, _, nr in runs]

    def long_and_rest():
        for (first, n_first, rest, n_rest), cond in zip(runs, rare):
            def one(first=first, n_first=n_first, rest=rest, n_rest=n_rest):
                _start_run(src, first[0], dst, first[1], n_first, sem, RUN_SIZES_LONG)
                _start_run(src, rest[0], dst, rest[1], n_rest, sem, RUN_SIZES)

            _if_rare(cond, one)

    _if_rare(functools.reduce(jnp.logical_or, rare), long_and_rest)


def _pad_chunks(fn):
    size = EXP_BM // 2
    while size >= 1:
        fn(size)
        size //= 2


def _expert_kernel(order_ref, be_ref, na_ref, x_ref, w1_ref, w3_ref, w2_ref, o_ref, w1b, w3b, w2b):
    i = pl.program_id(0)
    active = i < na_ref[0]
    changed = jnp.logical_or(i == 0, be_ref[i] != be_ref[jnp.maximum(i - 1, 0)])

    @pl.when(jnp.logical_and(active, changed))
    def _():
        w1b[...] = w1_ref[0, 0].astype(BF16)
        w3b[...] = w3_ref[0, 0].astype(BF16)
        w2b[...] = w2_ref[0, 0].astype(BF16)

    @pl.when(active)
    def _():
        rows = EXP_BM // EXP_SUB
        for s in range(EXP_SUB):
            part = pl.ds(s * rows * PK_SUB, rows * PK_SUB)
            his, los = _unpack_halves(x_ref.at[part], rows)
            xb = jnp.concatenate([c.astype(BF16) for c in his + los], axis=1)
            h1 = jnp.dot(xb, w1b[...], preferred_element_type=F32)
            h3 = jnp.dot(xb, w3b[...], preferred_element_type=F32)
            act = (h1 * _sigmoid(h1) * h3).astype(BF16)
            _pack_rows(jnp.dot(act, w2b[...], preferred_element_type=F32), o_ref.at[part], rows)

    @pl.when(jnp.logical_not(active))
    def _():
        o_ref[...] = jnp.zeros_like(o_ref)


def _expert_call(order, block_expert, n_active, xg, w1, w3, w2, layer):
    nb = order.shape[0]
    live = lambda i, od, be, na: (od[jnp.minimum(i, na[0] - 1)], 0)
    wmap = lambda i, od, be, na: (layer, be[jnp.minimum(i, na[0] - 1)], 0, 0)
    grid_spec = pltpu.PrefetchScalarGridSpec(
        num_scalar_prefetch=3,
        grid=(nb,),
        in_specs=[
            pl.BlockSpec((EXP_BM * PK_SUB, LANES), live),
            pl.BlockSpec((1, 1, D, FF), wmap),
            pl.BlockSpec((1, 1, D, FF), wmap),
            pl.BlockSpec((1, 1, FF, D), wmap),
        ],
        out_specs=pl.BlockSpec((EXP_BM * PK_SUB, LANES), lambda i, od, be, na: (od[i], 0)),
        scratch_shapes=[pltpu.VMEM((D, FF), BF16), pltpu.VMEM((D, FF), BF16), pltpu.VMEM((FF, D), BF16)],
    )
    return pl.pallas_call(
        _expert_kernel,
        grid_spec=grid_spec,
        out_shape=jax.ShapeDtypeStruct((nb * EXP_BM * PK_SUB, LANES), jnp.uint32),
        compiler_params=pltpu.CompilerParams(
            dimension_semantics=("arbitrary",), vmem_limit_bytes=VMEM_LIMIT),
        name="expert_blocks",
    )(order, block_expert, n_active, xg, w1, w3, w2)


def _combine_kernel(seg_ref, seg_next_ref, x_ref, gate_ref, gw_ref, pos_ref, fw_ref, y_hbm, o_ref, ybuf, sem, *,
                    final_norm):
    i = pl.program_id(0)
    last = pl.num_programs(0) - 1

    def fetch(table, parity):
        dst = ybuf.at[parity]

        def gather(group, carry):
            _start_tile_runs(table, lambda t, r, e: t[0, r, e], group, y_hbm, dst, sem.at[parity], to_slots=False)
            return carry

        lax.fori_loop(0, NE // RUN_UNROLL, gather, 0)

    par = i % 2

    @pl.when(i == 0)
    def _():
        fetch(seg_ref, 0)

    @pl.when(i < last)
    def _():
        fetch(seg_next_ref, 1 - par)

    cur = ybuf.at[par]
    _tile_copy(y_hbm, 0, cur, 0, 2 * CMB_TM, sem.at[par]).wait()

    his, los = _unpack_halves(cur, 2 * CMB_TM)
    ys = jnp.concatenate([c.astype(BF16) for c in his + los], axis=1)
    gw = gw_ref[...]
    pos = pos_ref[...]
    col = lax.broadcasted_iota(jnp.int32, (CMB_TM, 2 * CMB_TM), 1)
    selector = jnp.where(col == pos[:, 0:1], gw[:, 0:1], jnp.where(col == pos[:, 1:2], gw[:, 1:2], 0.0))
    moe = jnp.dot(selector.astype(BF16), ys, preferred_element_type=F32)
    x2 = x_ref[...] + gate_ref[0] * moe
    if final_norm:
        ms = jnp.mean(x2 * x2, axis=-1, keepdims=True)
        x2 = x2 * lax.rsqrt(ms + NORM_EPS) * fw_ref[...]
    o_ref[...] = x2


def _combine_call(runs, x1, gate, gw_col, pos_col, fw, yg, seq, final_norm):
    n = x1.shape[0]
    tiles_per_batch = seq // CMB_TM
    nt = n // CMB_TM
    return pl.pallas_call(
        functools.partial(_combine_kernel, final_norm=final_norm),
        grid=(nt,),
        in_specs=[
            pl.BlockSpec((1, 8, LANES), lambda i: (i, 0, 0), memory_space=pltpu.SMEM),
            pl.BlockSpec((1, 8, LANES), lambda i: (jnp.minimum(i + 1, nt - 1), 0, 0), memory_space=pltpu.SMEM),
            pl.BlockSpec((CMB_TM, D), lambda i: (i, 0)),
            pl.BlockSpec((1, 1, D), lambda i: (i // tiles_per_batch, 0, 0)),
            pl.BlockSpec((CMB_TM, 2), lambda i: (i, 0)),
            pl.BlockSpec((CMB_TM, 2), lambda i: (i, 0)),
            pl.BlockSpec((1, D), lambda i: (0, 0)),
            pl.BlockSpec(memory_space=pl.ANY),
        ],
        out_specs=pl.BlockSpec((CMB_TM, D), lambda i: (i, 0)),
        out_shape=jax.ShapeDtypeStruct((n, D), F32),
        scratch_shapes=[pltpu.VMEM((2, 2 * CMB_TM * PK_SUB, LANES), jnp.uint32), pltpu.SemaphoreType.DMA((2,))],
        compiler_params=pltpu.CompilerParams(
            dimension_semantics=("arbitrary",), vmem_limit_bytes=VMEM_LIMIT),
        name="moe_combine",
    )(runs, runs, x1, gate, gw_col, pos_col, fw, yg)


def _decay_tables():
    log_gamma = jnp.log1p(-jnp.exp2(-5.0 - jnp.arange(NH, dtype=F32)))
    pos = jnp.arange(CH, dtype=F32)
    diff = pos[:, None] - pos[None, :]
    mask = jnp.where(diff >= 0, jnp.exp(log_gamma[:, None, None] * jnp.maximum(diff, 0.0)), 0.0)
    q_decay = jnp.exp(log_gamma[:, None] * (pos + 1.0))
    k_decay = jnp.exp(log_gamma[:, None] * (CH - 1.0 - pos))
    chunk_decay = jnp.exp(log_gamma * CH)
    return mask.astype(F32), q_decay.T, k_decay.T, chunk_decay


def kernel(x, c, positions, w_ada, b_ada, norm1_w, norm2_w, w_in, w_out, ret_norm_w, sg_w_s, sg_b_s,
           w_router, router_bias, w1, w3, w2, final_norm_w):
    bsz, seq, d = x.shape
    n = bsz * seq
    assert d == D and seq % INP_TM == 0 and seq % MIX_ROWS == 0 and n % CMB_TM == 0

    c_pad = jnp.zeros((8, D), F32).at[:bsz].set(c)
    mod = _ada_call(c_pad, w_ada, b_ada)[:, :bsz]
    mod = mod.reshape(N_LAYERS, bsz, N_MOD, 1, D)

    half = HD // 2
    inv_freq = (ROPE_THETA ** (-jnp.arange(half, dtype=F32) / half)).reshape(1, half)
    cos, sin = _rope_call(positions.reshape(n, 1), inv_freq)
    decay_mask, q_decay_t, k_decay_t, chunk_decay = _decay_tables()

    perm = (jnp.arange(NE) % NEG) * EPG + jnp.arange(NE) // NEG
    wr_t = w_router.T[perm]
    rb_col = router_bias[perm].reshape(NE, 1)

    nb = n * 2 // EXP_BM + NE
    assert nb <= BLK_LANES and n % OUT_TM == 0
    xs = x.reshape(n, D)
    for l in range(N_LAYERS):
        shift1, scale1, gate1, shift2, scale2, gate2 = [mod[l, :, t] for t in range(N_MOD)]
        proj = _inproj_call(xs, shift1, scale1, norm1_w[l].reshape(1, D), w_in[l].astype(BF16), cos, sin, seq)
        mix = _mix_call(proj, chunk_decay, decay_mask, q_decay_t, k_decay_t, ret_norm_w[l].reshape(1, RW),
                        sg_w_s[l], sg_b_s[l].T, bsz, seq)
        x1, sorted_pos, runs, gw, owner, n_active, xg = _outproj_call(
            mix, xs, w_out[l].astype(BF16), gate1, shift2, scale2, norm2_w[l].reshape(1, D), wr_t, rb_col, seq, nb)

        owner = owner.reshape(BLK_LANES)
        ids = jnp.arange(BLK_LANES, dtype=jnp.int32)
        ahead = (owner[None, :] < owner[:, None]) | ((owner[None, :] == owner[:, None]) & (ids[None, :] < ids[:, None]))
        pos = jnp.sum(ahead, axis=1)
        at = pos[None, :] == ids[:, None]
        order = jnp.sum(jnp.where(at, ids[None, :], 0), axis=1)[:nb].astype(jnp.int32)
        block_expert = jnp.minimum(jnp.sum(jnp.where(at, owner[None, :], 0), axis=1)[:nb], NE - 1).astype(jnp.int32)

        yg = _expert_call(order, block_expert, n_active.reshape(1), xg, w1, w3, w2, l)
        pos_col = sorted_pos.transpose(0, 2, 1).reshape(n, 2)
        xs = _combine_call(runs, x1, gate2, gw.T, pos_col, final_norm_w.reshape(1, D), yg, seq,
                           final_norm=(l == N_LAYERS - 1))
    return xs.reshape(bsz, seq, D)
```

```python
import functools

import jax
import jax.numpy as jnp
from jax import lax
from jax.experimental import pallas as pl
from jax.experimental.pallas import tpu as pltpu

F32 = jnp.float32
BF16 = jnp.bfloat16

D = 2048
N_LAYERS = 2
RW = D // 2
NH = 4
HD = RW // NH
CH = 128
ROPE_THETA = 10000.0
SGW = D - RW
NG = 8
GD = SGW // NG
PROJ = 4 * RW + 2 * SGW
NE = 32
NEG = 8
EPG = NE // NEG
FF = 512
N_MOD = 6
NORM_EPS = 1e-6

VMEM_LIMIT = 48 * 1024 * 1024
BIG_VMEM_LIMIT = 56 * 1024 * 1024

ADA_TN = 1024
ROPE_TM = 2048
INP_TM = 256
INP_TN = 512
MIX_ROWS = 1024
OUT_TM = 512
RUN_UNROLL = 4
RUN_SHORT = 64
EXP_BM = 512
EXP_SUB = 2
CMB_TM = OUT_TM
BLK_LANES = 256

LANES = 128
PK_SUB = D // 2 // LANES


def _sigmoid(v):
    return 1.0 / (1.0 + jnp.exp(-v))


def _gelu_tanh(v):
    return 0.5 * v * (1.0 + jnp.tanh(0.7978845608028654 * (v + 0.044715 * (v * v * v))))


def _ada_kernel(c_ref, w_ref, b_ref, o_ref, *, bsz):
    c = c_ref[...]
    ca = c * _sigmoid(c)
    w = w_ref[0]
    rows = [jnp.sum(w * ca[:, b:b + 1], axis=0, keepdims=True) for b in range(bsz)]
    rows.append(jnp.zeros((8 - bsz, ADA_TN), F32))
    o_ref[0] = jnp.concatenate(rows, axis=0) + b_ref[0]


def _ada_call(c_cols, w_ada, b_ada, bsz):
    depth, _, ncol = w_ada.shape
    return pl.pallas_call(
        functools.partial(_ada_kernel, bsz=bsz),
        grid=(depth, ncol // ADA_TN),
        in_specs=[
            pl.BlockSpec((D, 8), lambda l, j: (0, 0)),
            pl.BlockSpec((1, D, ADA_TN), lambda l, j: (l, 0, j)),
            pl.BlockSpec((1, 1, ADA_TN), lambda l, j: (l, 0, j)),
        ],
        out_specs=pl.BlockSpec((1, 8, ADA_TN), lambda l, j: (l, 0, j)),
        out_shape=jax.ShapeDtypeStruct((depth, 8, ncol), F32),
        compiler_params=pltpu.CompilerParams(vmem_limit_bytes=VMEM_LIMIT),
        name="ada_mod",
    )(c_cols, w_ada, b_ada.reshape(depth, 1, ncol))


def _rope_kernel(pos_ref, freq_ref, cos_ref, sin_ref):
    ang = pos_ref[...].astype(F32) * freq_ref[...]
    cos_ref[...] = jnp.cos(ang)
    sin_ref[...] = jnp.sin(ang)


def _rope_call(pos_col, inv_freq):
    n = pos_col.shape[0]
    half = inv_freq.shape[1]
    return pl.pallas_call(
        _rope_kernel,
        grid=(n // ROPE_TM,),
        in_specs=[
            pl.BlockSpec((ROPE_TM, 1), lambda i: (i, 0)),
            pl.BlockSpec((1, half), lambda i: (0, 0)),
        ],
        out_specs=[
            pl.BlockSpec((ROPE_TM, half), lambda i: (i, 0)),
            pl.BlockSpec((ROPE_TM, half), lambda i: (i, 0)),
        ],
        out_shape=[jax.ShapeDtypeStruct((n, half), F32)] * 2,
        name="rope_tables",
    )(pos_col, inv_freq)


def _inproj_kernel(x_ref, shift_ref, scale_ref, nw_ref, cos_ref, sin_ref, w_hbm, o_ref, w_vmem, sem):
    @pl.when(pl.program_id(0) == 0)
    def _():
        cp = pltpu.make_async_copy(w_hbm, w_vmem, sem)
        cp.start()
        cp.wait()

    x = x_ref[...]
    ms = jnp.mean(x * x, axis=-1, keepdims=True)
    h = x * lax.rsqrt(ms + NORM_EPS) * nw_ref[...]
    h = (h * (1.0 + scale_ref[0]) + shift_ref[0]).astype(BF16)
    cos = cos_ref[...]
    sin = sin_ref[...]
    half = HD // 2

    for j in range(PROJ // INP_TN):
        c0 = j * INP_TN
        acc = jnp.dot(h, w_vmem[:, c0:c0 + INP_TN], preferred_element_type=F32)
        sec = c0 // RW
        if sec in (0, 1):
            scale = 1.0 if sec == 0 else HD ** -0.5
            for hh in range(INP_TN // HD):
                a = acc[:, hh * HD:hh * HD + half]
                b = acc[:, hh * HD + half:(hh + 1) * HD]
                o_ref[:, c0 + hh * HD:c0 + hh * HD + half] = ((a * cos - b * sin) * scale).astype(BF16)
                o_ref[:, c0 + hh * HD + half:c0 + (hh + 1) * HD] = ((b * cos + a * sin) * scale).astype(BF16)
        elif sec == 2:
            o_ref[:, c0:c0 + INP_TN] = acc.astype(BF16)
        elif sec == 3:
            o_ref[:, c0:c0 + INP_TN] = (acc * _sigmoid(acc)).astype(BF16)
        elif sec == 4:
            o_ref[:, c0:c0 + INP_TN] = _gelu_tanh(acc).astype(BF16)
        else:
            for gg in range(INP_TN // GD):
                t = _gelu_tanh(acc[:, gg * GD:(gg + 1) * GD])
                ms_g = jnp.mean(t * t, axis=-1, keepdims=True)
                o_ref[:, c0 + gg * GD:c0 + (gg + 1) * GD] = (t * lax.rsqrt(ms_g + NORM_EPS)).astype(BF16)


def _inproj_call(x2, shift, scale, nw, w_bf, cos, sin, seq):
    n = x2.shape[0]
    tiles_per_batch = seq // INP_TM
    bmap = lambda i: (i // tiles_per_batch, 0, 0)
    return pl.pallas_call(
        _inproj_kernel,
        grid=(n // INP_TM,),
        in_specs=[
            pl.BlockSpec((INP_TM, D), lambda i: (i, 0)),
            pl.BlockSpec((1, 1, D), bmap),
            pl.BlockSpec((1, 1, D), bmap),
            pl.BlockSpec((1, D), lambda i: (0, 0)),
            pl.BlockSpec((INP_TM, HD // 2), lambda i: (i, 0)),
            pl.BlockSpec((INP_TM, HD // 2), lambda i: (i, 0)),
            pl.BlockSpec(memory_space=pl.ANY),
        ],
        out_specs=pl.BlockSpec((INP_TM, PROJ), lambda i: (i, 0)),
        out_shape=jax.ShapeDtypeStruct((n, PROJ), BF16),
        scratch_shapes=[pltpu.VMEM((D, PROJ), BF16), pltpu.SemaphoreType.DMA(())],
        compiler_params=pltpu.CompilerParams(
            dimension_semantics=("arbitrary",), vmem_limit_bytes=BIG_VMEM_LIMIT),
        name="in_proj",
    )(x2, shift, scale, nw, cos, sin, w_bf)


def _mix_kernel(cd_ref, q_ref, k_ref, v_ref, g_ref, u_ref, vs_ref, dm_ref, qd_ref, kd_ref,
                rnw_ref, ws_ref, bs_ref, o_ref, state_ref):
    @pl.when(pl.program_id(1) == 0)
    def _():
        state_ref[...] = jnp.zeros_like(state_ref)

    row = lax.broadcasted_iota(jnp.int32, (CH, CH), 0)
    col = lax.broadcasted_iota(jnp.int32, (CH, CH), 1)
    causal = row >= col

    chunks = [slice(c * CH, (c + 1) * CH) for c in range(MIX_ROWS // CH)]

    for g in range(NG):
        cols = slice(g * GD, (g + 1) * GD)
        wm = jnp.where(causal, ws_ref[g], 0.0).astype(BF16)
        vs_wide = jnp.concatenate([vs_ref[rows, cols] for rows in chunks], axis=1)
        z = jnp.dot(wm, vs_wide, preferred_element_type=F32) + bs_ref[:, g:g + 1]
        for c, rows in enumerate(chunks):
            zc = z[:, c * GD:(c + 1) * GD]
            o_ref[rows, RW + g * GD:RW + (g + 1) * GD] = (u_ref[rows, cols].astype(F32) * zc).astype(BF16)

    for h in range(NH):
        cols = slice(h * HD, (h + 1) * HD)
        st = state_ref[h]
        for rows in chunks:
            q = q_ref[rows, cols]
            k = k_ref[rows, cols]
            v = v_ref[rows, cols]
            s = lax.dot_general(q, k, (((1,), (1,)), ((), ())), preferred_element_type=F32)
            s = s * dm_ref[h]
            intra = jnp.dot(s.astype(BF16), v, preferred_element_type=F32)
            cross = jnp.dot(q, st.astype(BF16), preferred_element_type=F32) * qd_ref[:, h:h + 1]
            kdec = (k.astype(F32) * kd_ref[:, h:h + 1]).astype(BF16)
            upd = lax.dot_general(kdec, v, (((0,), (0,)), ((), ())), preferred_element_type=F32)
            st = st * cd_ref[h] + upd
            o = intra + cross
            ms = jnp.mean(o * o, axis=-1, keepdims=True)
            o = o * lax.rsqrt(ms + NORM_EPS) * rnw_ref[:, cols]
            o_ref[rows, cols] = (o * g_ref[rows, cols].astype(F32)).astype(BF16)
        state_ref[h] = st


def _mix_call(proj, chunk_decay, decay_mask, q_decay_t, k_decay_t, rnw, ws, bs_t, bsz, seq):
    n = proj.shape[0]
    steps = seq // MIX_ROWS

    def colblk(jb):
        return pl.BlockSpec((MIX_ROWS, RW), lambda b, s, cd: (b * steps + s, jb))

    const2 = lambda b, s, cd: (0, 0)
    const3 = lambda b, s, cd: (0, 0, 0)
    grid_spec = pltpu.PrefetchScalarGridSpec(
        num_scalar_prefetch=1,
        grid=(bsz, steps),
        in_specs=[colblk(0), colblk(1), colblk(2), colblk(3), colblk(4), colblk(5),
                  pl.BlockSpec((NH, CH, CH), const3),
                  pl.BlockSpec((CH, NH), const2),
                  pl.BlockSpec((CH, NH), const2),
                  pl.BlockSpec((1, RW), const2),
                  pl.BlockSpec((NG, CH, CH), const3),
                  pl.BlockSpec((CH, NG), const2)],
        out_specs=pl.BlockSpec((MIX_ROWS, D), lambda b, s, cd: (b * steps + s, 0)),
        scratch_shapes=[pltpu.VMEM((NH, HD, HD), F32)],
    )
    return pl.pallas_call(
        _mix_kernel,
        grid_spec=grid_spec,
        out_shape=jax.ShapeDtypeStruct((n, D), BF16),
        compiler_params=pltpu.CompilerParams(
            dimension_semantics=("arbitrary", "arbitrary"), vmem_limit_bytes=VMEM_LIMIT),
        name="retention_spatial_mix",
    )(chunk_decay, proj, proj, proj, proj, proj, proj, decay_mask, q_decay_t, k_decay_t, rnw, ws, bs_t)


def _outproj_kernel(mix_ref, x_ref, gate_ref, shift_ref, scale_ref, nw_ref, wr_ref, rb_ref, w_hbm,
                    x1_ref, pos_ref, seg_ref, gw_ref, be_ref, na_ref, xg_hbm,
                    carry_ref, cur_ref, bev_ref, nfree_ref, w_vmem, hbuf, seg_v, seg_s, tail_v, tail_s, zbuf,
                    sem, ssem, zsem):
    i = pl.program_id(0)
    last = pl.num_programs(0) - 1
    nblk = xg_hbm.shape[0] // (EXP_BM * PK_SUB)

    @pl.when(i == 0)
    def _():
        cp = pltpu.make_async_copy(w_hbm, w_vmem, ssem)
        cp.start()
        carry_ref[...] = jnp.zeros_like(carry_ref)
        cur_ref[...] = jnp.zeros_like(cur_ref)
        nfree_ref[...] = jnp.zeros_like(nfree_ref)
        bev_ref[...] = jnp.full(bev_ref.shape, NE, F32)
        cp.wait()

    y = jnp.dot(mix_ref[...], w_vmem[...], preferred_element_type=F32)
    x1 = x_ref[...] + gate_ref[0] * y
    x1_ref[...] = x1
    ms = jnp.mean(x1 * x1, axis=-1, keepdims=True)
    h2 = x1 * lax.rsqrt(ms + NORM_EPS) * nw_ref[...]
    h2 = h2 * (1.0 + scale_ref[0]) + shift_ref[0]

    def scatter_wait():
        _tile_copy(hbuf, 0, xg_hbm, 0, 2 * OUT_TM, sem).wait()

    nt = (((1,), (1,)), ((), ()))
    wr = wr_ref[...]
    w_hi = wr.astype(BF16)
    w_lo = (wr - w_hi.astype(F32)).astype(BF16)
    h_hi = h2.astype(BF16)
    h_lo = (h2 - h_hi.astype(F32)).astype(BF16)
    p_hi = lax.dot_general(jnp.concatenate([w_hi, w_lo], axis=0), h_hi, nt, preferred_element_type=F32)
    p_lo = lax.dot_general(w_hi, h_lo, nt, preferred_element_type=F32)
    logits = p_hi[:NE] + p_hi[NE:] + p_lo
    scores = _sigmoid(logits)
    biased = scores + rb_ref[...]
    a = [biased[m * NEG:(m + 1) * NEG] for m in range(EPG)]
    sc = [scores[m * NEG:(m + 1) * NEG] for m in range(EPG)]
    p, q = jnp.maximum(a[0], a[1]), jnp.minimum(a[0], a[1])
    r, s = jnp.maximum(a[2], a[3]), jnp.minimum(a[2], a[3])
    group_score = jnp.maximum(p, r) + jnp.maximum(jnp.minimum(p, r), jnp.maximum(q, s))
    gidx = lax.broadcasted_iota(jnp.int32, group_score.shape, 0)
    gmax = jnp.max(group_score, axis=0, keepdims=True)
    g_sel = jnp.min(jnp.where(group_score == gmax, gidx, NEG), axis=0, keepdims=True)
    pick = gidx == g_sel
    v = [jnp.sum(jnp.where(pick, a[m], 0.0), axis=0, keepdims=True) for m in range(EPG)]
    u = [jnp.sum(jnp.where(pick, sc[m], 0.0), axis=0, keepdims=True) for m in range(EPG)]

    def first_argmax(vals):
        m = functools.reduce(jnp.maximum, vals)
        idx = jnp.full(m.shape, EPG - 1, jnp.int32)
        for t in range(EPG - 2, -1, -1):
            idx = jnp.where(vals[t] == m, t, idx)
        return idx

    def select(vals, idx):
        out = vals[EPG - 1]
        for t in range(EPG - 2, -1, -1):
            out = jnp.where(idx == t, vals[t], out)
        return out

    i1 = first_argmax(v)
    v2 = [jnp.where(i1 == t, -jnp.inf, v[t]) for t in range(EPG)]
    i2 = first_argmax(v2)
    s1 = select(u, i1)
    s2 = select(u, i2)
    denom = s1 + s2
    e1 = g_sel * EPG + i1
    e2 = g_sel * EPG + i2
    gw_ref[0:1, :] = s1 / denom
    gw_ref[1:2, :] = s2 / denom

    tm = e1.shape[1]
    eidx = lax.broadcasted_iota(jnp.int32, (NE, tm), 0)
    oh1 = eidx == e1
    oh2 = eidx == e2
    oh = jnp.where(oh1 | oh2, 1.0, 0.0)
    tr = lax.broadcasted_iota(jnp.int32, (tm, tm), 0)
    tc = lax.broadcasted_iota(jnp.int32, (tm, tm), 1)
    earlier = jnp.where(tr < tc, 1.0, 0.0).astype(BF16)
    carry = carry_ref[...]
    local = jnp.dot(oh.astype(BF16), earlier, preferred_element_type=F32)
    rank = local + carry
    count = jnp.sum(oh, axis=1, keepdims=True)
    total = carry + count
    carry_ref[...] = total

    inv_bm = 1.0 / EXP_BM
    blocks_before = jnp.floor((carry + (EXP_BM - 1)) * inv_bm)
    need = jnp.floor((total + (EXP_BM - 1)) * inv_bm) - blocks_before
    er = lax.broadcasted_iota(jnp.int32, (NE, NE), 0)
    ec = lax.broadcasted_iota(jnp.int32, (NE, NE), 1)
    lower = jnp.where(ec < er, 1.0, 0.0).astype(BF16)
    count_hi = jnp.floor(count * (1.0 / 16.0))
    lane_e = lax.broadcasted_iota(jnp.int32, (NE, LANES), 1)
    digits = jnp.where(lane_e == 0, need, jnp.where(lane_e == 1, count_hi, count - 16.0 * count_hi))
    sums = jnp.dot(lower, digits.astype(BF16), preferred_element_type=F32)
    nfree = nfree_ref[...]
    base = nfree + sums[:, 0:1]
    start = 16.0 * sums[:, 1:2] + sums[:, 2:3]
    cur = cur_ref[...]
    cur_ref[...] = jnp.where(need > 0, base + need - 1.0, cur)
    nfree_new = nfree + jnp.sum(need, axis=0, keepdims=True)
    nfree_ref[...] = nfree_new
    blk = lax.broadcasted_iota(jnp.int32, (NE, bev_ref.shape[1]), 1).astype(F32)
    hit = jnp.logical_and(blk >= base, blk < base + need)
    owner = jnp.sum(jnp.where(hit, eidx[:, 0:1].astype(F32), 0.0), axis=0, keepdims=True)
    taken = jnp.sum(jnp.where(hit, 1.0, 0.0), axis=0, keepdims=True)
    bev = jnp.where(taken > 0, owner, bev_ref[...])
    bev_ref[...] = bev
    be_ref[...] = bev.astype(jnp.int32)
    na_ref[...] = nfree_new.astype(jnp.int32)

    sorted_all = start + local
    pos1 = jnp.sum(jnp.where(oh1, sorted_all, 0.0), axis=0, keepdims=True).astype(jnp.int32)
    pos2 = jnp.sum(jnp.where(oh2, sorted_all, 0.0), axis=0, keepdims=True).astype(jnp.int32)
    pos_ref[0, 0:1, :] = pos1
    pos_ref[0, 1:2, :] = pos2

    as_row = lambda col: jnp.sum(jnp.where(lane_e == eidx[:, 0:1], col, 0.0), axis=0, keepdims=True)
    j0 = jnp.floor(carry * inv_bm)
    in_block = carry - j0 * EXP_BM
    n_a = jnp.minimum(count, EXP_BM - in_block)
    block_of = lambda j: jnp.where(j < blocks_before, cur, base + (j - blocks_before))
    seg_rows = [start, n_a, block_of(j0) * EXP_BM + in_block, count - n_a, block_of(j0 + 1.0) * EXP_BM]
    seg_v[...] = jnp.concatenate(
        [as_row(col).astype(jnp.int32) for col in seg_rows] + [jnp.zeros((3, LANES), jnp.int32)], axis=0)
    seg_ref[0] = seg_v[...]
    to_smem = pltpu.make_async_copy(seg_v, seg_s, ssem)
    to_smem.start()

    @pl.when(i >= 1)
    def _():
        scatter_wait()

    for part in range(2):
        prow = lax.broadcasted_iota(jnp.int32, (tm, tm), 0) + part * tm
        perm = jnp.where((prow == pos1) | (prow == pos2), 1.0, 0.0).astype(BF16)
        sorted_rows = jnp.dot(perm, h_hi, preferred_element_type=F32)
        _pack_rows(sorted_rows, hbuf.at[pl.ds(part * tm * PK_SUB, tm * PK_SUB)], tm)

    to_smem.wait()

    def scatter(group, carry_):
        _start_tile_runs(seg_s, lambda t, r, e: t[r, e], group, hbuf, xg_hbm, sem, to_slots=True)
        return carry_

    lax.fori_loop(0, NE // RUN_UNROLL, scatter, 0)

    @pl.when(i == last)
    def _():
        scatter_wait()
        lane1 = lax.broadcasted_iota(jnp.int32, (1, LANES), 1)
        cur_row = jnp.where(lane1 == NE, nfree_new, as_row(cur_ref[...]))
        tail_v[0:1, :] = cur_row.astype(jnp.int32)
        tail_v[1:2, :] = as_row(total).astype(jnp.int32)
        cp = pltpu.make_async_copy(tail_v, tail_s, ssem)
        cp.start()
        cp.wait()
        zbuf[...] = jnp.zeros_like(zbuf)

        def fills(wait):
            def per_expert(e, carry_):
                used = tail_s[1, e] & (EXP_BM - 1)
                first = tail_s[0, e] * EXP_BM + used
                npad = jnp.where(used > 0, EXP_BM - used, 0)

                def chunk(size):
                    off = npad & ~(2 * size - 1)

                    @pl.when((npad & size) != 0)
                    def _():
                        cpz = _tile_copy(zbuf, 0, xg_hbm, first + off, size, zsem)
                        cpz.wait() if wait else cpz.start()

                _pad_chunks(chunk)
                return carry_

            lax.fori_loop(0, NE, per_expert, 0)

            def per_block(b, carry_):
                cpz = _tile_copy(zbuf, 0, xg_hbm, b * EXP_BM, EXP_BM, zsem)
                cpz.wait() if wait else cpz.start()
                return carry_

            lax.fori_loop(tail_s[0, NE], nblk, per_block, 0)

        fills(False)
        fills(True)


def _outproj_call(mix, x2, w_bf, gate, shift, scale, nw, wr_t, rb_col, seq, nb):
    n = x2.shape[0]
    tiles_per_batch = seq // OUT_TM
    bmap = lambda i: (i // tiles_per_batch, 0, 0)
    row = lambda i: (i, 0)
    lane = lambda i: (0, i)
    const = lambda i: (0, 0)
    table = lambda i: (i, 0, 0)
    return pl.pallas_call(
        _outproj_kernel,
        grid=(n // OUT_TM,),
        in_specs=[
            pl.BlockSpec((OUT_TM, D), row),
            pl.BlockSpec((OUT_TM, D), row),
            pl.BlockSpec((1, 1, D), bmap),
            pl.BlockSpec((1, 1, D), bmap),
            pl.BlockSpec((1, 1, D), bmap),
            pl.BlockSpec((1, D), const),
            pl.BlockSpec((NE, D), const),
            pl.BlockSpec((NE, 1), const),
            pl.BlockSpec(memory_space=pl.ANY),
        ],
        out_specs=[
            pl.BlockSpec((OUT_TM, D), row),
            pl.BlockSpec((1, 2, OUT_TM), table),
            pl.BlockSpec((1, 8, LANES), table),
            pl.BlockSpec((2, OUT_TM), lane),
            pl.BlockSpec((1, BLK_LANES), const),
            pl.BlockSpec((1, 1), const),
            pl.BlockSpec(memory_space=pl.ANY),
        ],
        out_shape=[
            jax.ShapeDtypeStruct((n, D), F32),
            jax.ShapeDtypeStruct((n // OUT_TM, 2, OUT_TM), jnp.int32),
            jax.ShapeDtypeStruct((n // OUT_TM, 8, LANES), jnp.int32),
            jax.ShapeDtypeStruct((2, n), F32),
            jax.ShapeDtypeStruct((1, BLK_LANES), jnp.int32),
            jax.ShapeDtypeStruct((1, 1), jnp.int32),
            jax.ShapeDtypeStruct((nb * EXP_BM * PK_SUB, LANES), jnp.uint32),
        ],
        scratch_shapes=[
            pltpu.VMEM((NE, 1), F32),
            pltpu.VMEM((NE, 1), F32),
            pltpu.VMEM((1, BLK_LANES), F32),
            pltpu.VMEM((1, 1), F32),
            pltpu.VMEM((D, D), BF16),
            pltpu.VMEM((2 * OUT_TM * PK_SUB, LANES), jnp.uint32),
            pltpu.VMEM((8, LANES), jnp.int32),
            pltpu.SMEM((8, LANES), jnp.int32),
            pltpu.VMEM((2, LANES), jnp.int32),
            pltpu.SMEM((2, LANES), jnp.int32),
            pltpu.VMEM((EXP_BM * PK_SUB, LANES), jnp.uint32),
            pltpu.SemaphoreType.DMA(()),
            pltpu.SemaphoreType.DMA(()),
            pltpu.SemaphoreType.DMA(()),
        ],
        compiler_params=pltpu.CompilerParams(
            dimension_semantics=("arbitrary",), vmem_limit_bytes=BIG_VMEM_LIMIT),
        name="out_proj_router",
    )(mix, x2, gate, shift, scale, nw, wr_t, rb_col, w_bf)


def _pack_rows(v, o_ref, rows):
    hi = lax.bitcast_convert_type(v[:, :D // 2].astype(BF16).astype(F32), jnp.uint32)
    lo = lax.bitcast_convert_type(v[:, D // 2:].astype(BF16).astype(F32), jnp.uint32)
    packed = hi | (lo >> 16)
    for j in range(PK_SUB):
        o_ref[pl.ds(j, rows, stride=PK_SUB), :] = packed[:, j * LANES:(j + 1) * LANES]


def _unpack_halves(x_ref, rows):
    his, los = [], []
    for j in range(PK_SUB):
        p = x_ref[pl.ds(j, rows, stride=PK_SUB), :]
        his.append(lax.bitcast_convert_type(p & jnp.uint32(0xFFFF0000), F32))
        los.append(lax.bitcast_convert_type(p << 16, F32))
    return his, los


def _tile_rows(row, nrows):
    start = row * PK_SUB
    if not isinstance(row, int):
        start = pl.multiple_of(start, PK_SUB)
    return pl.ds(start, nrows * PK_SUB)


def _tile_copy(src, src_row, dst, dst_row, nrows, sem):
    return pltpu.make_async_copy(src.at[_tile_rows(src_row, nrows), :], dst.at[_tile_rows(dst_row, nrows), :], sem)


def _if_rare(cond, body):
    def once(_, carry):
        body()
        return carry

    lax.fori_loop(0, cond.astype(jnp.int32), once, 0)


RUN_SIZES = [OUT_TM >> s for s in range(OUT_TM.bit_length())]
RUN_SIZES_SHORT = [s for s in RUN_SIZES if s < RUN_SHORT]
RUN_SIZES_LONG = [s for s in RUN_SIZES if s >= RUN_SHORT]


def _start_run(src, src_row, dst, dst_row, n, sem, sizes):
    for size in sizes:
        off = n & ~(2 * size - 1)

        @pl.when((n & size) != 0)
        def _(size=size, off=off):
            _tile_copy(src, src_row + off, dst, dst_row + off, size, sem).start()


def _start_tile_runs(table, read, group, src, dst, sem, to_slots):
    runs = []
    for u in range(RUN_UNROLL):
        e = group * RUN_UNROLL + u
        start, n_first, slot_first, n_rest, slot_rest = [read(table, r, e) for r in range(5)]
        first = (start, slot_first) if to_slots else (slot_first, start)
        rest = (start + n_first, slot_rest) if to_slots else (slot_rest, start + n_first)
        runs.append((first, n_first, rest, n_rest))
    for first, n_first, _, _ in runs:
        _start_run(src, first[0], dst, first[1], n_first, sem, RUN_SIZES_SHORT)
    rare = [(nf >= RUN_SHORT) | (nr > 0) for _, nf, _, nr in runs]

    def long_and_rest():
        for (first, n_first, rest, n_rest), cond in zip(runs, rare):
            def one(first=first, n_first=n_first, rest=rest, n_rest=n_rest):
                _start_run(src, first[0], dst, first[1], n_first, sem, RUN_SIZES_LONG)
                _start_run(src, rest[0], dst, rest[1], n_rest, sem, RUN_SIZES)

            _if_rare(cond, one)

    _if_rare(functools.reduce(jnp.logical_or, rare), long_and_rest)


def _pad_chunks(fn):
    size = EXP_BM // 2
    while size >= 1:
        fn(size)
        size //= 2


def _expert_kernel(order_ref, be_ref, na_ref, x_ref, w1_ref, w3_ref, w2_ref, o_ref, w1b, w3b, w2b):
    i = pl.program_id(0)
    active = i < na_ref[0]
    changed = jnp.logical_or(i == 0, be_ref[i] != be_ref[jnp.maximum(i - 1, 0)])

    @pl.when(jnp.logical_and(active, changed))
    def _():
        w1b[...] = w1_ref[0, 0].astype(BF16)
        w3b[...] = w3_ref[0, 0].astype(BF16)
        w2b[...] = w2_ref[0, 0].astype(BF16)

    @pl.when(active)
    def _():
        rows = EXP_BM // EXP_SUB
        for s in range(EXP_SUB):
            part = pl.ds(s * rows * PK_SUB, rows * PK_SUB)
            his, los = _unpack_halves(x_ref.at[part], rows)
            xb = jnp.concatenate([c.astype(BF16) for c in his + los], axis=1)
            h1 = jnp.dot(xb, w1b[...], preferred_element_type=F32)
            h3 = jnp.dot(xb, w3b[...], preferred_element_type=F32)
            act = (h1 * _sigmoid(h1) * h3).astype(BF16)
            _pack_rows(jnp.dot(act, w2b[...], preferred_element_type=F32), o_ref.at[part], rows)

    @pl.when(jnp.logical_not(active))
    def _():
        o_ref[...] = jnp.zeros_like(o_ref)


def _expert_call(order, block_expert, n_active, xg, w1, w3, w2, layer):
    nb = order.shape[0]
    live = lambda i, od, be, na: (od[jnp.minimum(i, na[0] - 1)], 0)
    wmap = lambda i, od, be, na: (layer, be[jnp.minimum(i, na[0] - 1)], 0, 0)
    grid_spec = pltpu.PrefetchScalarGridSpec(
        num_scalar_prefetch=3,
        grid=(nb,),
        in_specs=[
            pl.BlockSpec((EXP_BM * PK_SUB, LANES), live),
            pl.BlockSpec((1, 1, D, FF), wmap),
            pl.BlockSpec((1, 1, D, FF), wmap),
            pl.BlockSpec((1, 1, FF, D), wmap),
        ],
        out_specs=pl.BlockSpec((EXP_BM * PK_SUB, LANES), lambda i, od, be, na: (od[i], 0)),
        scratch_shapes=[pltpu.VMEM((D, FF), BF16), pltpu.VMEM((D, FF), BF16), pltpu.VMEM((FF, D), BF16)],
    )
    return pl.pallas_call(
        _expert_kernel,
        grid_spec=grid_spec,
        out_shape=jax.ShapeDtypeStruct((nb * EXP_BM * PK_SUB, LANES), jnp.uint32),
        compiler_params=pltpu.CompilerParams(
            dimension_semantics=("arbitrary",), vmem_limit_bytes=VMEM_LIMIT),
        name="expert_blocks",
    )(order, block_expert, n_active, xg, w1, w3, w2)


def _combine_kernel(seg_ref, seg_next_ref, x_ref, gate_ref, gw_ref, pos_ref, fw_ref, y_hbm, o_ref, ybuf, sem, *,
                    final_norm):
    i = pl.program_id(0)
    last = pl.num_programs(0) - 1

    def fetch(table, parity):
        dst = ybuf.at[parity]

        def gather(group, carry):
            _start_tile_runs(table, lambda t, r, e: t[0, r, e], group, y_hbm, dst, sem.at[parity], to_slots=False)
            return carry

        lax.fori_loop(0, NE // RUN_UNROLL, gather, 0)

    par = i % 2

    @pl.when(i == 0)
    def _():
        fetch(seg_ref, 0)

    @pl.when(i < last)
    def _():
        fetch(seg_next_ref, 1 - par)

    cur = ybuf.at[par]
    _tile_copy(y_hbm, 0, cur, 0, 2 * CMB_TM, sem.at[par]).wait()

    his, los = _unpack_halves(cur, 2 * CMB_TM)
    ys = jnp.concatenate([c.astype(BF16) for c in his + los], axis=1)
    gw = gw_ref[...]
    pos = pos_ref[0]
    prow = lax.broadcasted_iota(jnp.int32, (2 * CMB_TM, CMB_TM), 0)
    selector = jnp.where(prow == pos[0:1], gw[0:1], jnp.where(prow == pos[1:2], gw[1:2], 0.0))
    moe = lax.dot_general(selector.astype(BF16), ys, (((0,), (0,)), ((), ())), preferred_element_type=F32)
    x2 = x_ref[...] + gate_ref[0] * moe
    if final_norm:
        ms = jnp.mean(x2 * x2, axis=-1, keepdims=True)
        x2 = x2 * lax.rsqrt(ms + NORM_EPS) * fw_ref[...]
    o_ref[...] = x2


def _combine_call(runs, x1, gate, gw, sorted_pos, fw, yg, seq, final_norm):
    n = x1.shape[0]
    tiles_per_batch = seq // CMB_TM
    nt = n // CMB_TM
    return pl.pallas_call(
        functools.partial(_combine_kernel, final_norm=final_norm),
        grid=(nt,),
        in_specs=[
            pl.BlockSpec((1, 8, LANES), lambda i: (i, 0, 0), memory_space=pltpu.SMEM),
            pl.BlockSpec((1, 8, LANES), lambda i: (jnp.minimum(i + 1, nt - 1), 0, 0), memory_space=pltpu.SMEM),
            pl.BlockSpec((CMB_TM, D), lambda i: (i, 0)),
            pl.BlockSpec((1, 1, D), lambda i: (i // tiles_per_batch, 0, 0)),
            pl.BlockSpec((2, CMB_TM), lambda i: (0, i)),
            pl.BlockSpec((1, 2, CMB_TM), lambda i: (i, 0, 0)),
            pl.BlockSpec((1, D), lambda i: (0, 0)),
            pl.BlockSpec(memory_space=pl.ANY),
        ],
        out_specs=pl.BlockSpec((CMB_TM, D), lambda i: (i, 0)),
        out_shape=jax.ShapeDtypeStruct((n, D), F32),
        scratch_shapes=[pltpu.VMEM((2, 2 * CMB_TM * PK_SUB, LANES), jnp.uint32), pltpu.SemaphoreType.DMA((2,))],
        compiler_params=pltpu.CompilerParams(
            dimension_semantics=("arbitrary",), vmem_limit_bytes=VMEM_LIMIT),
        name="moe_combine",
    )(runs, runs, x1, gate, gw, sorted_pos, fw, yg)


def _decay_tables():
    log_gamma = jnp.log1p(-jnp.exp2(-5.0 - jnp.arange(NH, dtype=F32)))
    pos = jnp.arange(CH, dtype=F32)
    diff = pos[:, None] - pos[None, :]
    mask = jnp.where(diff >= 0, jnp.exp(log_gamma[:, None, None] * jnp.maximum(diff, 0.0)), 0.0)
    q_decay = jnp.exp(log_gamma[:, None] * (pos + 1.0))
    k_decay = jnp.exp(log_gamma[:, None] * (CH - 1.0 - pos))
    chunk_decay = jnp.exp(log_gamma * CH)
    return mask.astype(F32), q_decay.T, k_decay.T, chunk_decay


def kernel(x, c, positions, w_ada, b_ada, norm1_w, norm2_w, w_in, w_out, ret_norm_w, sg_w_s, sg_b_s,
           w_router, router_bias, w1, w3, w2, final_norm_w):
    bsz, seq, d = x.shape
    n = bsz * seq
    assert d == D and seq % INP_TM == 0 and seq % MIX_ROWS == 0 and n % CMB_TM == 0

    assert bsz <= 8
    c_cols = jnp.zeros((D, 8), F32).at[:, :bsz].set(c.T)
    mod = _ada_call(c_cols, w_ada, b_ada, bsz)[:, :bsz]
    mod = mod.reshape(N_LAYERS, bsz, N_MOD, 1, D)

    half = HD // 2
    inv_freq = (ROPE_THETA ** (-jnp.arange(half, dtype=F32) / half)).reshape(1, half)
    cos, sin = _rope_call(positions.reshape(n, 1), inv_freq)
    decay_mask, q_decay_t, k_decay_t, chunk_decay = _decay_tables()

    perm = (jnp.arange(NE) % NEG) * EPG + jnp.arange(NE) // NEG
    wr_t = w_router.T[perm]
    rb_col = router_bias[perm].reshape(NE, 1)

    nb = n * 2 // EXP_BM + NE
    assert nb <= BLK_LANES and n % OUT_TM == 0
    xs = x.reshape(n, D)
    for l in range(N_LAYERS):
        shift1, scale1, gate1, shift2, scale2, gate2 = [mod[l, :, t] for t in range(N_MOD)]
        proj = _inproj_call(xs, shift1, scale1, norm1_w[l].reshape(1, D), w_in[l].astype(BF16), cos, sin, seq)
        mix = _mix_call(proj, chunk_decay, decay_mask, q_decay_t, k_decay_t, ret_norm_w[l].reshape(1, RW),
                        sg_w_s[l], sg_b_s[l].T, bsz, seq)
        x1, sorted_pos, runs, gw, owner, n_active, xg = _outproj_call(
            mix, xs, w_out[l].astype(BF16), gate1, shift2, scale2, norm2_w[l].reshape(1, D), wr_t, rb_col, seq, nb)

        owner = owner.reshape(BLK_LANES)
        ids = jnp.arange(BLK_LANES, dtype=jnp.int32)
        ahead = (owner[None, :] < owner[:, None]) | ((owner[None, :] == owner[:, None]) & (ids[None, :] < ids[:, None]))
        pos = jnp.sum(ahead, axis=1)
        at = pos[None, :] == ids[:, None]
        order = jnp.sum(jnp.where(at, ids[None, :], 0), axis=1)[:nb].astype(jnp.int32)
        block_expert = jnp.minimum(jnp.sum(jnp.where(at, owner[None, :], 0), axis=1)[:nb], NE - 1).astype(jnp.int32)

        yg = _expert_call(order, block_expert, n_active.reshape(1), xg, w1, w3, w2, l)
        xs = _combine_call(runs, x1, gate2, gw, sorted_pos, final_norm_w.reshape(1, D), yg, seq,
                           final_norm=(l == N_LAYERS - 1))
    return xs.reshape(bsz, seq, D)
```

```python
import functools

import jax
import jax.numpy as jnp
from jax import lax
from jax.experimental import pallas as pl
from jax.experimental.pallas import tpu as pltpu

F32 = jnp.float32
BF16 = jnp.bfloat16

D = 2048
N_LAYERS = 2
RW = D // 2
NH = 4
HD = RW // NH
CH = 128
RCH = 128
ROPE_THETA = 10000.0
SGW = D - RW
NG = 8
GD = SGW // NG
PROJ = 4 * RW + 2 * SGW
NE = 32
NEG = 8
EPG = NE // NEG
FF = 512
N_MOD = 6
NORM_EPS = 1e-6

VMEM_LIMIT = 48 * 1024 * 1024
BIG_VMEM_LIMIT = 56 * 1024 * 1024

ADA_TN = 1024
INP_TM = 256
INP_TN = 512
MIX_ROWS = 1024
OUT_TM = 512
RUN_UNROLL = 4
RUN_SHORT = 64
EXP_BM = 512
EXP_SUB = 2
CMB_TM = OUT_TM
BLK_LANES = 256

LANES = 128
PK_SUB = D // 2 // LANES


def _sigmoid(v):
    return 1.0 / (1.0 + jnp.exp(-v))


def _gelu_tanh(v):
    return 0.5 * v * (1.0 + jnp.tanh(0.7978845608028654 * (v + 0.044715 * (v * v * v))))


def _ada_kernel(c_ref, w_ref, b_ref, o_ref, *, bsz):
    c = c_ref[...]
    ca = c * _sigmoid(c)
    w = w_ref[0]
    rows = [jnp.sum(w * ca[:, b:b + 1], axis=0, keepdims=True) for b in range(bsz)]
    rows.append(jnp.zeros((8 - bsz, ADA_TN), F32))
    o_ref[0] = jnp.concatenate(rows, axis=0) + b_ref[0]


def _ada_call(c_cols, w_ada, b_ada, bsz):
    depth, _, ncol = w_ada.shape
    return pl.pallas_call(
        functools.partial(_ada_kernel, bsz=bsz),
        grid=(depth, ncol // ADA_TN),
        in_specs=[
            pl.BlockSpec((D, 8), lambda l, j: (0, 0)),
            pl.BlockSpec((1, D, ADA_TN), lambda l, j: (l, 0, j)),
            pl.BlockSpec((1, 1, ADA_TN), lambda l, j: (l, 0, j)),
        ],
        out_specs=pl.BlockSpec((1, 8, ADA_TN), lambda l, j: (l, 0, j)),
        out_shape=jax.ShapeDtypeStruct((depth, 8, ncol), F32),
        compiler_params=pltpu.CompilerParams(vmem_limit_bytes=VMEM_LIMIT),
        name="ada_mod",
    )(c_cols, w_ada, b_ada.reshape(depth, 1, ncol))


def _inproj_kernel(x_ref, shift_ref, scale_ref, nw_ref, pos_ref, freq_ref, w_hbm, o_ref, w_vmem, stage, sem, *,
                   layer):
    @pl.when(pl.program_id(0) == 0)
    def _():
        def panel(j):
            cols = pl.ds(j * INP_TN, INP_TN)
            return pltpu.make_async_copy(w_hbm.at[layer, :, cols], stage.at[j % 2], sem.at[j % 2])

        npanel = PROJ // INP_TN
        panel(0).start()
        for j in range(npanel):
            if j + 1 < npanel:
                panel(j + 1).start()
            panel(j).wait()
            w_vmem[:, j * INP_TN:(j + 1) * INP_TN] = stage[j % 2].astype(BF16)

    x = x_ref[...]
    ms = jnp.mean(x * x, axis=-1, keepdims=True)
    h = x * lax.rsqrt(ms + NORM_EPS) * nw_ref[...]
    h = (h * (1.0 + scale_ref[0]) + shift_ref[0]).astype(BF16)
    ang = pos_ref[...].astype(F32) * freq_ref[...]
    cos = jnp.cos(ang)
    sin = jnp.sin(ang)
    half = HD // 2
    n_rot = 2 * RW // INP_TN
    order = list(range(n_rot, PROJ // INP_TN)) + list(range(n_rot))

    for j in order:
        c0 = j * INP_TN
        acc = jnp.dot(h, w_vmem[:, c0:c0 + INP_TN], preferred_element_type=F32)
        sec = c0 // RW
        if sec in (0, 1):
            scale = 1.0 if sec == 0 else HD ** -0.5
            for hh in range(INP_TN // HD):
                a = acc[:, hh * HD:hh * HD + half]
                b = acc[:, hh * HD + half:(hh + 1) * HD]
                o_ref[:, c0 + hh * HD:c0 + hh * HD + half] = ((a * cos - b * sin) * scale).astype(BF16)
                o_ref[:, c0 + hh * HD + half:c0 + (hh + 1) * HD] = ((b * cos + a * sin) * scale).astype(BF16)
        elif sec == 2:
            o_ref[:, c0:c0 + INP_TN] = acc.astype(BF16)
        elif sec == 3:
            o_ref[:, c0:c0 + INP_TN] = (acc * _sigmoid(acc)).astype(BF16)
        elif sec == 4:
            o_ref[:, c0:c0 + INP_TN] = _gelu_tanh(acc).astype(BF16)
        else:
            for gg in range(INP_TN // GD):
                t = _gelu_tanh(acc[:, gg * GD:(gg + 1) * GD])
                ms_g = jnp.mean(t * t, axis=-1, keepdims=True)
                o_ref[:, c0 + gg * GD:c0 + (gg + 1) * GD] = (t * lax.rsqrt(ms_g + NORM_EPS)).astype(BF16)


def _inproj_call(x2, shift, scale, nw, w_in, layer, pos_col, inv_freq, seq):
    n = x2.shape[0]
    tiles_per_batch = seq // INP_TM
    bmap = lambda i: (i // tiles_per_batch, 0, 0)
    return pl.pallas_call(
        functools.partial(_inproj_kernel, layer=layer),
        grid=(n // INP_TM,),
        in_specs=[
            pl.BlockSpec((INP_TM, D), lambda i: (i, 0)),
            pl.BlockSpec((1, 1, D), bmap),
            pl.BlockSpec((1, 1, D), bmap),
            pl.BlockSpec((1, D), lambda i: (0, 0)),
            pl.BlockSpec((INP_TM, 1), lambda i: (i, 0)),
            pl.BlockSpec((1, HD // 2), lambda i: (0, 0)),
            pl.BlockSpec(memory_space=pl.ANY),
        ],
        out_specs=pl.BlockSpec((INP_TM, PROJ), lambda i: (i, 0)),
        out_shape=jax.ShapeDtypeStruct((n, PROJ), BF16),
        scratch_shapes=[pltpu.VMEM((D, PROJ), BF16), pltpu.VMEM((2, D, INP_TN), F32),
                        pltpu.SemaphoreType.DMA((2,))],
        compiler_params=pltpu.CompilerParams(
            dimension_semantics=("arbitrary",), vmem_limit_bytes=BIG_VMEM_LIMIT),
        name="in_proj",
    )(x2, shift, scale, nw, pos_col, inv_freq, w_in)


def _mix_kernel(cd_ref, q_ref, k_ref, v_ref, g_ref, u_ref, vs_ref, dm_ref, qd_ref, kd_ref,
                rnw_ref, ws_ref, bs_ref, o_ref, state_ref):
    @pl.when(pl.program_id(1) == 0)
    def _():
        state_ref[...] = jnp.zeros_like(state_ref)

    row = lax.broadcasted_iota(jnp.int32, (CH, CH), 0)
    col = lax.broadcasted_iota(jnp.int32, (CH, CH), 1)
    causal = row >= col

    chunks = [slice(c * CH, (c + 1) * CH) for c in range(MIX_ROWS // CH)]

    for g in range(NG):
        cols = slice(g * GD, (g + 1) * GD)
        wm = jnp.where(causal, ws_ref[g], 0.0).astype(BF16)
        vs_wide = jnp.concatenate([vs_ref[rows, cols] for rows in chunks], axis=1)
        z = jnp.dot(wm, vs_wide, preferred_element_type=F32) + bs_ref[:, g:g + 1]
        for c, rows in enumerate(chunks):
            zc = z[:, c * GD:(c + 1) * GD]
            o_ref[rows, RW + g * GD:RW + (g + 1) * GD] = (u_ref[rows, cols].astype(F32) * zc).astype(BF16)

    for h in range(NH):
        cols = slice(h * HD, (h + 1) * HD)
        st = state_ref[h]
        for rows in [slice(c * RCH, (c + 1) * RCH) for c in range(MIX_ROWS // RCH)]:
            q = q_ref[rows, cols]
            k = k_ref[rows, cols]
            v = v_ref[rows, cols]
            s = lax.dot_general(q, k, (((1,), (1,)), ((), ())), preferred_element_type=F32)
            s = s * dm_ref[h]
            intra = jnp.dot(s.astype(BF16), v, preferred_element_type=F32)
            cross = jnp.dot(q, st.astype(BF16), preferred_element_type=F32) * qd_ref[:, h:h + 1]
            kdec = (k.astype(F32) * kd_ref[:, h:h + 1]).astype(BF16)
            upd = lax.dot_general(kdec, v, (((0,), (0,)), ((), ())), preferred_element_type=F32)
            st = st * cd_ref[h] + upd
            o = intra + cross
            ms = jnp.mean(o * o, axis=-1, keepdims=True)
            o = o * lax.rsqrt(ms + NORM_EPS) * rnw_ref[:, cols]
            o_ref[rows, cols] = (o * g_ref[rows, cols].astype(F32)).astype(BF16)
        state_ref[h] = st


def _mix_call(proj, chunk_decay, decay_mask, q_decay_t, k_decay_t, rnw, ws, bs_t, bsz, seq):
    n = proj.shape[0]
    steps = seq // MIX_ROWS

    def colblk(jb):
        return pl.BlockSpec((MIX_ROWS, RW), lambda b, s, cd: (b * steps + s, jb))

    const2 = lambda b, s, cd: (0, 0)
    const3 = lambda b, s, cd: (0, 0, 0)
    grid_spec = pltpu.PrefetchScalarGridSpec(
        num_scalar_prefetch=1,
        grid=(bsz, steps),
        in_specs=[colblk(0), colblk(1), colblk(2), colblk(3), colblk(4), colblk(5),
                  pl.BlockSpec((NH, RCH, RCH), const3),
                  pl.BlockSpec((RCH, NH), const2),
                  pl.BlockSpec((RCH, NH), const2),
                  pl.BlockSpec((1, RW), const2),
                  pl.BlockSpec((NG, CH, CH), const3),
                  pl.BlockSpec((CH, NG), const2)],
        out_specs=pl.BlockSpec((MIX_ROWS, D), lambda b, s, cd: (b * steps + s, 0)),
        scratch_shapes=[pltpu.VMEM((NH, HD, HD), F32)],
    )
    return pl.pallas_call(
        _mix_kernel,
        grid_spec=grid_spec,
        out_shape=jax.ShapeDtypeStruct((n, D), BF16),
        compiler_params=pltpu.CompilerParams(
            dimension_semantics=("arbitrary", "arbitrary"), vmem_limit_bytes=VMEM_LIMIT),
        name="retention_spatial_mix",
    )(chunk_decay, proj, proj, proj, proj, proj, proj, decay_mask, q_decay_t, k_decay_t, rnw, ws, bs_t)


def _outproj_kernel(mix_ref, x_ref, gate_ref, shift_ref, scale_ref, nw_ref, wr_ref, rb_ref, w_hbm,
                    x1_ref, pos_ref, seg_ref, gw_ref, be_ref, na_ref, xg_hbm,
                    carry_ref, cur_ref, bev_ref, nfree_ref, w_vmem, hbuf, seg_v, seg_s, tail_v, tail_s, zbuf,
                    sem, ssem, zsem):
    i = pl.program_id(0)
    last = pl.num_programs(0) - 1
    nblk = xg_hbm.shape[0] // (EXP_BM * PK_SUB)

    @pl.when(i == 0)
    def _():
        cp = pltpu.make_async_copy(w_hbm, w_vmem, ssem)
        cp.start()
        carry_ref[...] = jnp.zeros_like(carry_ref)
        cur_ref[...] = jnp.zeros_like(cur_ref)
        nfree_ref[...] = jnp.zeros_like(nfree_ref)
        bev_ref[...] = jnp.full(bev_ref.shape, NE, F32)
        cp.wait()

    y = jnp.dot(mix_ref[...], w_vmem[...], preferred_element_type=F32)
    x1 = x_ref[...] + gate_ref[0] * y
    x1_ref[...] = x1
    ms = jnp.mean(x1 * x1, axis=-1, keepdims=True)
    h2 = x1 * lax.rsqrt(ms + NORM_EPS) * nw_ref[...]
    h2 = h2 * (1.0 + scale_ref[0]) + shift_ref[0]

    def scatter_wait():
        _tile_copy(hbuf, 0, xg_hbm, 0, 2 * OUT_TM, sem).wait()

    nt = (((1,), (1,)), ((), ()))
    wr = wr_ref[...]
    w_hi = wr.astype(BF16)
    w_lo = (wr - w_hi.astype(F32)).astype(BF16)
    h_hi = h2.astype(BF16)
    h_lo = (h2 - h_hi.astype(F32)).astype(BF16)
    p_hi = lax.dot_general(jnp.concatenate([w_hi, w_lo], axis=0), h_hi, nt, preferred_element_type=F32)
    p_lo = lax.dot_general(w_hi, h_lo, nt, preferred_element_type=F32)
    logits = p_hi[:NE] + p_hi[NE:] + p_lo
    scores = _sigmoid(logits)
    biased = scores + rb_ref[...]
    a = [biased[m * NEG:(m + 1) * NEG] for m in range(EPG)]
    sc = [scores[m * NEG:(m + 1) * NEG] for m in range(EPG)]
    p, q = jnp.maximum(a[0], a[1]), jnp.minimum(a[0], a[1])
    r, s = jnp.maximum(a[2], a[3]), jnp.minimum(a[2], a[3])
    group_score = jnp.maximum(p, r) + jnp.maximum(jnp.minimum(p, r), jnp.maximum(q, s))
    gidx = lax.broadcasted_iota(jnp.int32, group_score.shape, 0)
    gmax = jnp.max(group_score, axis=0, keepdims=True)
    g_sel = jnp.min(jnp.where(group_score == gmax, gidx, NEG), axis=0, keepdims=True)
    pick = gidx == g_sel
    v = [jnp.sum(jnp.where(pick, a[m], 0.0), axis=0, keepdims=True) for m in range(EPG)]
    u = [jnp.sum(jnp.where(pick, sc[m], 0.0), axis=0, keepdims=True) for m in range(EPG)]

    def first_argmax(vals):
        m = functools.reduce(jnp.maximum, vals)
        idx = jnp.full(m.shape, EPG - 1, jnp.int32)
        for t in range(EPG - 2, -1, -1):
            idx = jnp.where(vals[t] == m, t, idx)
        return idx

    def select(vals, idx):
        out = vals[EPG - 1]
        for t in range(EPG - 2, -1, -1):
            out = jnp.where(idx == t, vals[t], out)
        return out

    i1 = first_argmax(v)
    v2 = [jnp.where(i1 == t, -jnp.inf, v[t]) for t in range(EPG)]
    i2 = first_argmax(v2)
    s1 = select(u, i1)
    s2 = select(u, i2)
    denom = s1 + s2
    e1 = g_sel * EPG + i1
    e2 = g_sel * EPG + i2
    gw_ref[0:1, :] = s1 / denom
    gw_ref[1:2, :] = s2 / denom

    tm = e1.shape[1]
    eidx = lax.broadcasted_iota(jnp.int32, (NE, tm), 0)
    oh1 = eidx == e1
    oh2 = eidx == e2
    oh = jnp.where(oh1 | oh2, 1.0, 0.0)
    tr = lax.broadcasted_iota(jnp.int32, (tm, tm), 0)
    tc = lax.broadcasted_iota(jnp.int32, (tm, tm), 1)
    earlier = jnp.where(tr < tc, 1.0, 0.0).astype(BF16)
    carry = carry_ref[...]
    local = jnp.dot(oh.astype(BF16), earlier, preferred_element_type=F32)
    rank = local + carry
    count = jnp.sum(oh, axis=1, keepdims=True)
    total = carry + count
    carry_ref[...] = total

    inv_bm = 1.0 / EXP_BM
    blocks_before = jnp.floor((carry + (EXP_BM - 1)) * inv_bm)
    need = jnp.floor((total + (EXP_BM - 1)) * inv_bm) - blocks_before
    er = lax.broadcasted_iota(jnp.int32, (NE, NE), 0)
    ec = lax.broadcasted_iota(jnp.int32, (NE, NE), 1)
    lower = jnp.where(ec < er, 1.0, 0.0).astype(BF16)
    count_hi = jnp.floor(count * (1.0 / 16.0))
    lane_e = lax.broadcasted_iota(jnp.int32, (NE, LANES), 1)
    digits = jnp.where(lane_e == 0, need, jnp.where(lane_e == 1, count_hi, count - 16.0 * count_hi))
    sums = jnp.dot(lower, digits.astype(BF16), preferred_element_type=F32)
    nfree = nfree_ref[...]
    base = nfree + sums[:, 0:1]
    start = 16.0 * sums[:, 1:2] + sums[:, 2:3]
    cur = cur_ref[...]
    cur_ref[...] = jnp.where(need > 0, base + need - 1.0, cur)
    nfree_new = nfree + jnp.sum(need, axis=0, keepdims=True)
    nfree_ref[...] = nfree_new
    blk = lax.broadcasted_iota(jnp.int32, (NE, bev_ref.shape[1]), 1).astype(F32)
    hit = jnp.logical_and(blk >= base, blk < base + need)
    owner = jnp.sum(jnp.where(hit, eidx[:, 0:1].astype(F32), 0.0), axis=0, keepdims=True)
    taken = jnp.sum(jnp.where(hit, 1.0, 0.0), axis=0, keepdims=True)
    bev = jnp.where(taken > 0, owner, bev_ref[...])
    bev_ref[...] = bev
    be_ref[...] = bev.astype(jnp.int32)
    na_ref[...] = nfree_new.astype(jnp.int32)

    sorted_all = start + local
    pos1 = jnp.sum(jnp.where(oh1, sorted_all, 0.0), axis=0, keepdims=True).astype(jnp.int32)
    pos2 = jnp.sum(jnp.where(oh2, sorted_all, 0.0), axis=0, keepdims=True).astype(jnp.int32)
    pos_ref[0, 0:1, :] = pos1
    pos_ref[0, 1:2, :] = pos2

    as_row = lambda col: jnp.sum(jnp.where(lane_e == eidx[:, 0:1], col, 0.0), axis=0, keepdims=True)
    j0 = jnp.floor(carry * inv_bm)
    in_block = carry - j0 * EXP_BM
    n_a = jnp.minimum(count, EXP_BM - in_block)
    block_of = lambda j: jnp.where(j < blocks_before, cur, base + (j - blocks_before))
    seg_rows = [start, n_a, block_of(j0) * EXP_BM + in_block, count - n_a, block_of(j0 + 1.0) * EXP_BM]
    seg_v[...] = jnp.concatenate(
        [as_row(col).astype(jnp.int32) for col in seg_rows] + [jnp.zeros((3, LANES), jnp.int32)], axis=0)
    seg_ref[0] = seg_v[...]
    to_smem = pltpu.make_async_copy(seg_v, seg_s, ssem)
    to_smem.start()

    @pl.when(i >= 1)
    def _():
        scatter_wait()

    for part in range(2):
        prow = lax.broadcasted_iota(jnp.int32, (tm, tm), 0) + part * tm
        perm = jnp.where((prow == pos1) | (prow == pos2), 1.0, 0.0).astype(BF16)
        sorted_rows = jnp.dot(perm, h_hi, preferred_element_type=F32)
        _pack_rows(sorted_rows, hbuf.at[pl.ds(part * tm * PK_SUB, tm * PK_SUB)], tm)

    to_smem.wait()

    def scatter(group, carry_):
        _start_tile_runs(seg_s, lambda t, r, e: t[r, e], group, hbuf, xg_hbm, sem, to_slots=True)
        return carry_

    lax.fori_loop(0, NE // RUN_UNROLL, scatter, 0)

    @pl.when(i == last)
    def _():
        scatter_wait()
        lane1 = lax.broadcasted_iota(jnp.int32, (1, LANES), 1)
        cur_row = jnp.where(lane1 == NE, nfree_new, as_row(cur_ref[...]))
        tail_v[0:1, :] = cur_row.astype(jnp.int32)
        tail_v[1:2, :] = as_row(total).astype(jnp.int32)
        cp = pltpu.make_async_copy(tail_v, tail_s, ssem)
        cp.start()
        cp.wait()
        zbuf[...] = jnp.zeros_like(zbuf)

        def fills(wait):
            def per_expert(e, carry_):
                used = tail_s[1, e] & (EXP_BM - 1)
                first = tail_s[0, e] * EXP_BM + used
                npad = jnp.where(used > 0, EXP_BM - used, 0)

                def chunk(size):
                    off = npad & ~(2 * size - 1)

                    @pl.when((npad & size) != 0)
                    def _():
                        cpz = _tile_copy(zbuf, 0, xg_hbm, first + off, size, zsem)
                        cpz.wait() if wait else cpz.start()

                _pad_chunks(chunk)
                return carry_

            lax.fori_loop(0, NE, per_expert, 0)

            def per_block(b, carry_):
                cpz = _tile_copy(zbuf, 0, xg_hbm, b * EXP_BM, EXP_BM, zsem)
                cpz.wait() if wait else cpz.start()
                return carry_

            lax.fori_loop(tail_s[0, NE], nblk, per_block, 0)

        fills(False)
        fills(True)


def _outproj_call(mix, x2, w_bf, gate, shift, scale, nw, wr_t, rb_col, seq, nb):
    n = x2.shape[0]
    tiles_per_batch = seq // OUT_TM
    bmap = lambda i: (i // tiles_per_batch, 0, 0)
    row = lambda i: (i, 0)
    lane = lambda i: (0, i)
    const = lambda i: (0, 0)
    table = lambda i: (i, 0, 0)
    return pl.pallas_call(
        _outproj_kernel,
        grid=(n // OUT_TM,),
        in_specs=[
            pl.BlockSpec((OUT_TM, D), row),
            pl.BlockSpec((OUT_TM, D), row),
            pl.BlockSpec((1, 1, D), bmap),
            pl.BlockSpec((1, 1, D), bmap),
            pl.BlockSpec((1, 1, D), bmap),
            pl.BlockSpec((1, D), const),
            pl.BlockSpec((NE, D), const),
            pl.BlockSpec((NE, 1), const),
            pl.BlockSpec(memory_space=pl.ANY),
        ],
        out_specs=[
            pl.BlockSpec((OUT_TM, D), row),
            pl.BlockSpec((1, 2, OUT_TM), table),
            pl.BlockSpec((1, 8, LANES), table),
            pl.BlockSpec((2, OUT_TM), lane),
            pl.BlockSpec((1, BLK_LANES), const),
            pl.BlockSpec((1, 1), const),
            pl.BlockSpec(memory_space=pl.ANY),
        ],
        out_shape=[
            jax.ShapeDtypeStruct((n, D), F32),
            jax.ShapeDtypeStruct((n // OUT_TM, 2, OUT_TM), jnp.int32),
            jax.ShapeDtypeStruct((n // OUT_TM, 8, LANES), jnp.int32),
            jax.ShapeDtypeStruct((2, n), F32),
            jax.ShapeDtypeStruct((1, BLK_LANES), jnp.int32),
            jax.ShapeDtypeStruct((1, 1), jnp.int32),
            jax.ShapeDtypeStruct((nb * EXP_BM * PK_SUB, LANES), jnp.uint32),
        ],
        scratch_shapes=[
            pltpu.VMEM((NE, 1), F32),
            pltpu.VMEM((NE, 1), F32),
            pltpu.VMEM((1, BLK_LANES), F32),
            pltpu.VMEM((1, 1), F32),
            pltpu.VMEM((D, D), BF16),
            pltpu.VMEM((2 * OUT_TM * PK_SUB, LANES), jnp.uint32),
            pltpu.VMEM((8, LANES), jnp.int32),
            pltpu.SMEM((8, LANES), jnp.int32),
            pltpu.VMEM((2, LANES), jnp.int32),
            pltpu.SMEM((2, LANES), jnp.int32),
            pltpu.VMEM((EXP_BM * PK_SUB, LANES), jnp.uint32),
            pltpu.SemaphoreType.DMA(()),
            pltpu.SemaphoreType.DMA(()),
            pltpu.SemaphoreType.DMA(()),
        ],
        compiler_params=pltpu.CompilerParams(
            dimension_semantics=("arbitrary",), vmem_limit_bytes=BIG_VMEM_LIMIT),
        name="out_proj_router",
    )(mix, x2, gate, shift, scale, nw, wr_t, rb_col, w_bf)


def _pack_rows(v, o_ref, rows):
    hi = lax.bitcast_convert_type(v[:, :D // 2].astype(BF16).astype(F32), jnp.uint32)
    lo = lax.bitcast_convert_type(v[:, D // 2:].astype(BF16).astype(F32), jnp.uint32)
    packed = hi | (lo >> 16)
    for j in range(PK_SUB):
        o_ref[pl.ds(j, rows, stride=PK_SUB), :] = packed[:, j * LANES:(j + 1) * LANES]


def _unpack_halves(x_ref, rows):
    his, los = [], []
    for j in range(PK_SUB):
        p = x_ref[pl.ds(j, rows, stride=PK_SUB), :]
        his.append(lax.bitcast_convert_type(p & jnp.uint32(0xFFFF0000), F32))
        los.append(lax.bitcast_convert_type(p << 16, F32))
    return his, los


def _tile_rows(row, nrows):
    start = row * PK_SUB
    if not isinstance(row, int):
        start = pl.multiple_of(start, PK_SUB)
    return pl.ds(start, nrows * PK_SUB)


def _tile_copy(src, src_row, dst, dst_row, nrows, sem):
    return pltpu.make_async_copy(src.at[_tile_rows(src_row, nrows), :], dst.at[_tile_rows(dst_row, nrows), :], sem)


def _if_rare(cond, body):
    def once(_, carry):
        body()
        return carry

    lax.fori_loop(0, cond.astype(jnp.int32), once, 0)


RUN_SIZES = [OUT_TM >> s for s in range(OUT_TM.bit_length())]
RUN_SIZES_SHORT = [s for s in RUN_SIZES if s < RUN_SHORT]
RUN_SIZES_LONG = [s for s in RUN_SIZES if s >= RUN_SHORT]


def _start_run(src, src_row, dst, dst_row, n, sem, sizes):
    for size in sizes:
        off = n & ~(2 * size - 1)

        @pl.when((n & size) != 0)
        def _(size=size, off=off):
            _tile_copy(src, src_row + off, dst, dst_row + off, size, sem).start()


def _start_tile_runs(table, read, group, src, dst, sem, to_slots):
    runs = []
    for u in range(RUN_UNROLL):
        e = group * RUN_UNROLL + u
        start, n_first, slot_first, n_rest, slot_rest = [read(table, r, e) for r in range(5)]
        first = (start, slot_first) if to_slots else (slot_first, start)
        rest = (start + n_first, slot_rest) if to_slots else (slot_rest, start + n_first)
        runs.append((first, n_first, rest, n_rest))
    for first, n_first, _, _ in runs:
        _start_run(src, first[0], dst, first[1], n_first, sem, RUN_SIZES_SHORT)
    rare = [(nf >= RUN_SHORT) | (nr > 0) for _, nf, _, nr in runs]

    def long_and_rest():
        for (first, n_first, rest, n_rest), cond in zip(runs, rare):
            def one(first=first, n_first=n_first, rest=rest, n_rest=n_rest):
                _start_run(src, first[0], dst, first[1], n_first, sem, RUN_SIZES_LONG)
                _start_run(src, rest[0], dst, rest[1], n_rest, sem, RUN_SIZES)

            _if_rare(cond, one)

    _if_rare(functools.reduce(jnp.logical_or, rare), long_and_rest)


def _pad_chunks(fn):
    size = EXP_BM // 2
    while size >= 1:
        fn(size)
        size //= 2


def _expert_kernel(order_ref, be_ref, na_ref, x_ref, w1_ref, w3_ref, w2_ref, o_ref, w1b, w3b, w2b):
    i = pl.program_id(0)
    active = i < na_ref[0]
    changed = jnp.logical_or(i == 0, be_ref[i] != be_ref[jnp.maximum(i - 1, 0)])

    @pl.when(jnp.logical_and(active, changed))
    def _():
        w1b[...] = w1_ref[0, 0].astype(BF16)
        w3b[...] = w3_ref[0, 0].astype(BF16)
        w2b[...] = w2_ref[0, 0].astype(BF16)

    @pl.when(active)
    def _():
        rows = EXP_BM // EXP_SUB
        for s in range(EXP_SUB):
            part = pl.ds(s * rows * PK_SUB, rows * PK_SUB)
            his, los = _unpack_halves(x_ref.at[part], rows)
            xb = jnp.concatenate([c.astype(BF16) for c in his + los], axis=1)
            h1 = jnp.dot(xb, w1b[...], preferred_element_type=F32)
            h3 = jnp.dot(xb, w3b[...], preferred_element_type=F32)
            act = (h1 * _sigmoid(h1) * h3).astype(BF16)
            _pack_rows(jnp.dot(act, w2b[...], preferred_element_type=F32), o_ref.at[part], rows)

    @pl.when(jnp.logical_not(active))
    def _():
        o_ref[...] = jnp.zeros_like(o_ref)


def _expert_call(order, block_expert, n_active, xg, w1, w3, w2, layer):
    nb = order.shape[0]
    live = lambda i, od, be, na: (od[jnp.minimum(i, na[0] - 1)], 0)
    wmap = lambda i, od, be, na: (layer, be[jnp.minimum(i, na[0] - 1)], 0, 0)
    grid_spec = pltpu.PrefetchScalarGridSpec(
        num_scalar_prefetch=3,
        grid=(nb,),
        in_specs=[
            pl.BlockSpec((EXP_BM * PK_SUB, LANES), live),
            pl.BlockSpec((1, 1, D, FF), wmap),
            pl.BlockSpec((1, 1, D, FF), wmap),
            pl.BlockSpec((1, 1, FF, D), wmap),
        ],
        out_specs=pl.BlockSpec((EXP_BM * PK_SUB, LANES), lambda i, od, be, na: (od[i], 0)),
        scratch_shapes=[pltpu.VMEM((D, FF), BF16), pltpu.VMEM((D, FF), BF16), pltpu.VMEM((FF, D), BF16)],
    )
    return pl.pallas_call(
        _expert_kernel,
        grid_spec=grid_spec,
        out_shape=jax.ShapeDtypeStruct((nb * EXP_BM * PK_SUB, LANES), jnp.uint32),
        compiler_params=pltpu.CompilerParams(
            dimension_semantics=("arbitrary",), vmem_limit_bytes=VMEM_LIMIT),
        name="expert_blocks",
    )(order, block_expert, n_active, xg, w1, w3, w2)


def _combine_kernel(seg_ref, seg_next_ref, x_ref, gate_ref, gw_ref, pos_ref, fw_ref, y_hbm, o_ref, ybuf, sem, *,
                    final_norm):
    i = pl.program_id(0)
    last = pl.num_programs(0) - 1

    def fetch(table, parity):
        dst = ybuf.at[parity]

        def gather(group, carry):
            _start_tile_runs(table, lambda t, r, e: t[0, r, e], group, y_hbm, dst, sem.at[parity], to_slots=False)
            return carry

        lax.fori_loop(0, NE // RUN_UNROLL, gather, 0)

    par = i % 2

    @pl.when(i == 0)
    def _():
        fetch(seg_ref, 0)

    @pl.when(i < last)
    def _():
        fetch(seg_next_ref, 1 - par)

    cur = ybuf.at[par]
    _tile_copy(y_hbm, 0, cur, 0, 2 * CMB_TM, sem.at[par]).wait()

    his, los = _unpack_halves(cur, 2 * CMB_TM)
    ys = jnp.concatenate([c.astype(BF16) for c in his + los], axis=1)
    gw = gw_ref[...]
    pos = pos_ref[0]
    prow = lax.broadcasted_iota(jnp.int32, (2 * CMB_TM, CMB_TM), 0)
    selector = jnp.where(prow == pos[0:1], gw[0:1], jnp.where(prow == pos[1:2], gw[1:2], 0.0))
    moe = lax.dot_general(selector.astype(BF16), ys, (((0,), (0,)), ((), ())), preferred_element_type=F32)
    x2 = x_ref[...] + gate_ref[0] * moe
    if final_norm:
        ms = jnp.mean(x2 * x2, axis=-1, keepdims=True)
        x2 = x2 * lax.rsqrt(ms + NORM_EPS) * fw_ref[...]
    o_ref[...] = x2


def _combine_call(runs, x1, gate, gw, sorted_pos, fw, yg, seq, final_norm):
    n = x1.shape[0]
    tiles_per_batch = seq // CMB_TM
    nt = n // CMB_TM
    return pl.pallas_call(
        functools.partial(_combine_kernel, final_norm=final_norm),
        grid=(nt,),
        in_specs=[
            pl.BlockSpec((1, 8, LANES), lambda i: (i, 0, 0), memory_space=pltpu.SMEM),
            pl.BlockSpec((1, 8, LANES), lambda i: (jnp.minimum(i + 1, nt - 1), 0, 0), memory_space=pltpu.SMEM),
            pl.BlockSpec((CMB_TM, D), lambda i: (i, 0)),
            pl.BlockSpec((1, 1, D), lambda i: (i // tiles_per_batch, 0, 0)),
            pl.BlockSpec((2, CMB_TM), lambda i: (0, i)),
            pl.BlockSpec((1, 2, CMB_TM), lambda i: (i, 0, 0)),
            pl.BlockSpec((1, D), lambda i: (0, 0)),
            pl.BlockSpec(memory_space=pl.ANY),
        ],
        out_specs=pl.BlockSpec((CMB_TM, D), lambda i: (i, 0)),
        out_shape=jax.ShapeDtypeStruct((n, D), F32),
        scratch_shapes=[pltpu.VMEM((2, 2 * CMB_TM * PK_SUB, LANES), jnp.uint32), pltpu.SemaphoreType.DMA((2,))],
        compiler_params=pltpu.CompilerParams(
            dimension_semantics=("arbitrary",), vmem_limit_bytes=VMEM_LIMIT),
        name="moe_combine",
    )(runs, runs, x1, gate, gw, sorted_pos, fw, yg)


def _decay_tables():
    log_gamma = jnp.log1p(-jnp.exp2(-5.0 - jnp.arange(NH, dtype=F32)))
    pos = jnp.arange(RCH, dtype=F32)
    diff = pos[:, None] - pos[None, :]
    mask = jnp.where(diff >= 0, jnp.exp(log_gamma[:, None, None] * jnp.maximum(diff, 0.0)), 0.0)
    q_decay = jnp.exp(log_gamma[:, None] * (pos + 1.0))
    k_decay = jnp.exp(log_gamma[:, None] * (RCH - 1.0 - pos))
    chunk_decay = jnp.exp(log_gamma * RCH)
    return mask.astype(F32), q_decay.T, k_decay.T, chunk_decay


def kernel(x, c, positions, w_ada, b_ada, norm1_w, norm2_w, w_in, w_out, ret_norm_w, sg_w_s, sg_b_s,
           w_router, router_bias, w1, w3, w2, final_norm_w):
    bsz, seq, d = x.shape
    n = bsz * seq
    assert d == D and seq % INP_TM == 0 and seq % MIX_ROWS == 0 and n % CMB_TM == 0

    assert bsz <= 8
    c_cols = jnp.zeros((D, 8), F32).at[:, :bsz].set(c.T)
    mod = _ada_call(c_cols, w_ada, b_ada, bsz)[:, :bsz]
    mod = mod.reshape(N_LAYERS, bsz, N_MOD, 1, D)

    half = HD // 2
    inv_freq = (ROPE_THETA ** (-jnp.arange(half, dtype=F32) / half)).reshape(1, half)
    pos_col = positions.reshape(n, 1)
    decay_mask, q_decay_t, k_decay_t, chunk_decay = _decay_tables()

    perm = (jnp.arange(NE) % NEG) * EPG + jnp.arange(NE) // NEG
    wr_t = w_router.T[perm]
    rb_col = router_bias[perm].reshape(NE, 1)

    nb = n * 2 // EXP_BM + NE
    assert nb <= BLK_LANES and n % OUT_TM == 0
    xs = x.reshape(n, D)
    for l in range(N_LAYERS):
        shift1, scale1, gate1, shift2, scale2, gate2 = [mod[l, :, t] for t in range(N_MOD)]
        proj = _inproj_call(xs, shift1, scale1, norm1_w[l].reshape(1, D), w_in, l, pos_col, inv_freq, seq)
        mix = _mix_call(proj, chunk_decay, decay_mask, q_decay_t, k_decay_t, ret_norm_w[l].reshape(1, RW),
                        sg_w_s[l], sg_b_s[l].T, bsz, seq)
        x1, sorted_pos, runs, gw, owner, n_active, xg = _outproj_call(
            mix, xs, w_out[l].astype(BF16), gate1, shift2, scale2, norm2_w[l].reshape(1, D), wr_t, rb_col, seq, nb)

        owner = owner.reshape(BLK_LANES)
        ids = jnp.arange(BLK_LANES, dtype=jnp.int32)
        ahead = (owner[None, :] < owner[:, None]) | ((owner[None, :] == owner[:, None]) & (ids[None, :] < ids[:, None]))
        pos = jnp.sum(ahead, axis=1)
        at = pos[None, :] == ids[:, None]
        order = jnp.sum(jnp.where(at, ids[None, :], 0), axis=1)[:nb].astype(jnp.int32)
        block_expert = jnp.minimum(jnp.sum(jnp.where(at, owner[None, :], 0), axis=1)[:nb], NE - 1).astype(jnp.int32)

        yg = _expert_call(order, block_expert, n_active.reshape(1), xg, w1, w3, w2, l)
        xs = _combine_call(runs, x1, gate2, gw, sorted_pos, final_norm_w.reshape(1, D), yg, seq,
                           final_norm=(l == N_LAYERS - 1))
    return xs.reshape(bsz, seq, D)
```

```python
import functools

import jax
import jax.numpy as jnp
from jax import lax
from jax.experimental import pallas as pl
from jax.experimental.pallas import tpu as pltpu

F32 = jnp.float32
BF16 = jnp.bfloat16

D = 2048
N_LAYERS = 2
RW = D // 2
NH = 4
HD = RW // NH
CH = 128
RCH = 128
ROPE_THETA = 10000.0
SGW = D - RW
NG = 8
GD = SGW // NG
PROJ = 4 * RW + 2 * SGW
NE = 32
NEG = 8
EPG = NE // NEG
FF = 512
N_MOD = 6
NORM_EPS = 1e-6

VMEM_LIMIT = 48 * 1024 * 1024
BIG_VMEM_LIMIT = 56 * 1024 * 1024

ADA_TN = 1024
INP_TM = 256
INP_TN = 512
MIX_ROWS = 1024
OUT_TM = 512
RUN_UNROLL = 4
RUN_SHORT = 64
EXP_BM = 512
EXP_SUB = 2
CMB_TM = OUT_TM
BLK_LANES = 256

LANES = 128
PK_SUB = D // 2 // LANES


def _sigmoid(v):
    return 1.0 / (1.0 + jnp.exp(-v))


def _gelu_tanh(v):
    return 0.5 * v * (1.0 + jnp.tanh(0.7978845608028654 * (v + 0.044715 * (v * v * v))))


def _ada_kernel(c_ref, w_ref, b_ref, o_ref, *, bsz):
    c = c_ref[...]
    ca = c * _sigmoid(c)
    w = w_ref[0]
    rows = [jnp.sum(w * ca[:, b:b + 1], axis=0, keepdims=True) for b in range(bsz)]
    rows.append(jnp.zeros((8 - bsz, ADA_TN), F32))
    o_ref[0] = jnp.concatenate(rows, axis=0) + b_ref[0]


def _ada_call(c_cols, w_ada, b_ada, bsz):
    depth, _, ncol = w_ada.shape
    return pl.pallas_call(
        functools.partial(_ada_kernel, bsz=bsz),
        grid=(depth, ncol // ADA_TN),
        in_specs=[
            pl.BlockSpec((D, 8), lambda l, j: (0, 0)),
            pl.BlockSpec((1, D, ADA_TN), lambda l, j: (l, 0, j)),
            pl.BlockSpec((1, 1, ADA_TN), lambda l, j: (l, 0, j)),
        ],
        out_specs=pl.BlockSpec((1, 8, ADA_TN), lambda l, j: (l, 0, j)),
        out_shape=jax.ShapeDtypeStruct((depth, 8, ncol), F32),
        compiler_params=pltpu.CompilerParams(vmem_limit_bytes=VMEM_LIMIT),
        name="ada_mod",
    )(c_cols, w_ada, b_ada.reshape(depth, 1, ncol))


def _inproj_kernel(x_ref, shift_ref, scale_ref, nw_ref, pos_ref, freq_ref, w_hbm, o_ref, w_vmem, stage, sem, *,
                   layer):
    @pl.when(pl.program_id(0) == 0)
    def _():
        def panel(j):
            cols = pl.ds(j * INP_TN, INP_TN)
            return pltpu.make_async_copy(w_hbm.at[layer, :, cols], stage.at[j % 2], sem.at[j % 2])

        npanel = PROJ // INP_TN
        panel(0).start()
        for j in range(npanel):
            if j + 1 < npanel:
                panel(j + 1).start()
            panel(j).wait()
            w_vmem[:, j * INP_TN:(j + 1) * INP_TN] = stage[j % 2].astype(BF16)

    x = x_ref[...]
    ms = jnp.mean(x * x, axis=-1, keepdims=True)
    h = x * lax.rsqrt(ms + NORM_EPS) * nw_ref[...]
    h = (h * (1.0 + scale_ref[0]) + shift_ref[0]).astype(BF16)
    ang = pos_ref[...].astype(F32) * freq_ref[...]
    cos = jnp.cos(ang)
    sin = jnp.sin(ang)
    half = HD // 2
    n_rot = 2 * RW // INP_TN
    order = list(range(n_rot, PROJ // INP_TN)) + list(range(n_rot))

    for j in order:
        c0 = j * INP_TN
        acc = jnp.dot(h, w_vmem[:, c0:c0 + INP_TN], preferred_element_type=F32)
        sec = c0 // RW
        if sec in (0, 1):
            scale = 1.0 if sec == 0 else HD ** -0.5
            for hh in range(INP_TN // HD):
                a = acc[:, hh * HD:hh * HD + half]
                b = acc[:, hh * HD + half:(hh + 1) * HD]
                o_ref[:, c0 + hh * HD:c0 + hh * HD + half] = ((a * cos - b * sin) * scale).astype(BF16)
                o_ref[:, c0 + hh * HD + half:c0 + (hh + 1) * HD] = ((b * cos + a * sin) * scale).astype(BF16)
        elif sec == 2:
            o_ref[:, c0:c0 + INP_TN] = acc.astype(BF16)
        elif sec == 3:
            o_ref[:, c0:c0 + INP_TN] = (acc * _sigmoid(acc)).astype(BF16)
        elif sec == 4:
            o_ref[:, c0:c0 + INP_TN] = _gelu_tanh(acc).astype(BF16)
        else:
            for gg in range(INP_TN // GD):
                t = _gelu_tanh(acc[:, gg * GD:(gg + 1) * GD])
                ms_g = jnp.mean(t * t, axis=-1, keepdims=True)
                o_ref[:, c0 + gg * GD:c0 + (gg + 1) * GD] = (t * lax.rsqrt(ms_g + NORM_EPS)).astype(BF16)


def _inproj_call(x2, shift, scale, nw, w_in, layer, pos_col, inv_freq, seq):
    n = x2.shape[0]
    tiles_per_batch = seq // INP_TM
    bmap = lambda i: (i // tiles_per_batch, 0, 0)
    return pl.pallas_call(
        functools.partial(_inproj_kernel, layer=layer),
        grid=(n // INP_TM,),
        in_specs=[
            pl.BlockSpec((INP_TM, D), lambda i: (i, 0)),
            pl.BlockSpec((1, 1, D), bmap),
            pl.BlockSpec((1, 1, D), bmap),
            pl.BlockSpec((1, D), lambda i: (0, 0)),
            pl.BlockSpec((INP_TM, 1), lambda i: (i, 0)),
            pl.BlockSpec((1, HD // 2), lambda i: (0, 0)),
            pl.BlockSpec(memory_space=pl.ANY),
        ],
        out_specs=pl.BlockSpec((INP_TM, PROJ), lambda i: (i, 0)),
        out_shape=jax.ShapeDtypeStruct((n, PROJ), BF16),
        scratch_shapes=[pltpu.VMEM((D, PROJ), BF16), pltpu.VMEM((2, D, INP_TN), F32),
                        pltpu.SemaphoreType.DMA((2,))],
        compiler_params=pltpu.CompilerParams(
            dimension_semantics=("arbitrary",), vmem_limit_bytes=BIG_VMEM_LIMIT),
        name="in_proj",
    )(x2, shift, scale, nw, pos_col, inv_freq, w_in)


def _mix_kernel(cd_ref, q_ref, k_ref, v_ref, g_ref, u_ref, vs_ref, dm_ref, qd_ref, kd_ref,
                rnw_ref, ws_ref, bs_ref, o_ref, state_ref):
    @pl.when(pl.program_id(1) == 0)
    def _():
        state_ref[...] = jnp.zeros_like(state_ref)

    row = lax.broadcasted_iota(jnp.int32, (CH, CH), 0)
    col = lax.broadcasted_iota(jnp.int32, (CH, CH), 1)
    causal = row >= col

    chunks = [slice(c * CH, (c + 1) * CH) for c in range(MIX_ROWS // CH)]

    for g in range(NG):
        cols = slice(g * GD, (g + 1) * GD)
        wm = jnp.where(causal, ws_ref[g], 0.0).astype(BF16)
        vs_wide = jnp.concatenate([vs_ref[rows, cols] for rows in chunks], axis=1)
        z = jnp.dot(wm, vs_wide, preferred_element_type=F32) + bs_ref[:, g:g + 1]
        for c, rows in enumerate(chunks):
            zc = z[:, c * GD:(c + 1) * GD]
            o_ref[rows, RW + g * GD:RW + (g + 1) * GD] = (u_ref[rows, cols].astype(F32) * zc).astype(BF16)

    for h in range(NH):
        cols = slice(h * HD, (h + 1) * HD)
        st = state_ref[h]
        for rows in [slice(c * RCH, (c + 1) * RCH) for c in range(MIX_ROWS // RCH)]:
            q = q_ref[rows, cols]
            k = k_ref[rows, cols]
            v = v_ref[rows, cols]
            s = lax.dot_general(q, k, (((1,), (1,)), ((), ())), preferred_element_type=F32)
            s = s * dm_ref[h]
            intra = jnp.dot(s.astype(BF16), v, preferred_element_type=F32)
            cross = jnp.dot(q, st.astype(BF16), preferred_element_type=F32) * qd_ref[:, h:h + 1]
            kdec = (k.astype(F32) * kd_ref[:, h:h + 1]).astype(BF16)
            upd = lax.dot_general(kdec, v, (((0,), (0,)), ((), ())), preferred_element_type=F32)
            st = st * cd_ref[h] + upd
            o = intra + cross
            ms = jnp.mean(o * o, axis=-1, keepdims=True)
            o = o * lax.rsqrt(ms + NORM_EPS) * rnw_ref[:, cols]
            o_ref[rows, cols] = (o * g_ref[rows, cols].astype(F32)).astype(BF16)
        state_ref[h] = st


def _mix_call(proj, chunk_decay, decay_mask, q_decay_t, k_decay_t, rnw, ws, bs_t, bsz, seq):
    n = proj.shape[0]
    steps = seq // MIX_ROWS

    def colblk(jb):
        return pl.BlockSpec((MIX_ROWS, RW), lambda b, s, cd: (b * steps + s, jb))

    const2 = lambda b, s, cd: (0, 0)
    const3 = lambda b, s, cd: (0, 0, 0)
    grid_spec = pltpu.PrefetchScalarGridSpec(
        num_scalar_prefetch=1,
        grid=(bsz, steps),
        in_specs=[colblk(0), colblk(1), colblk(2), colblk(3), colblk(4), colblk(5),
                  pl.BlockSpec((NH, RCH, RCH), const3),
                  pl.BlockSpec((RCH, NH), const2),
                  pl.BlockSpec((RCH, NH), const2),
                  pl.BlockSpec((1, RW), const2),
                  pl.BlockSpec((NG, CH, CH), const3),
                  pl.BlockSpec((CH, NG), const2)],
        out_specs=pl.BlockSpec((MIX_ROWS, D), lambda b, s, cd: (b * steps + s, 0)),
        scratch_shapes=[pltpu.VMEM((NH, HD, HD), F32)],
    )
    return pl.pallas_call(
        _mix_kernel,
        grid_spec=grid_spec,
        out_shape=jax.ShapeDtypeStruct((n, D), BF16),
        compiler_params=pltpu.CompilerParams(
            dimension_semantics=("arbitrary", "arbitrary"), vmem_limit_bytes=VMEM_LIMIT),
        name="retention_spatial_mix",
    )(chunk_decay, proj, proj, proj, proj, proj, proj, decay_mask, q_decay_t, k_decay_t, rnw, ws, bs_t)


def _outproj_kernel(mix_ref, x_ref, gate_ref, shift_ref, scale_ref, nw_ref, wr_ref, rb_ref, w_hbm,
                    x1_ref, pos_ref, seg_ref, gw_ref, be_ref, na_ref, xg_hbm,
                    carry_ref, cur_ref, bev_ref, nfree_ref, w_vmem, hbuf, seg_v, seg_s, tail_v, tail_s, zbuf,
                    sem, ssem, zsem):
    i = pl.program_id(0)
    last = pl.num_programs(0) - 1
    nblk = xg_hbm.shape[0] // (EXP_BM * PK_SUB)

    @pl.when(i == 0)
    def _():
        cp = pltpu.make_async_copy(w_hbm, w_vmem, ssem)
        cp.start()
        carry_ref[...] = jnp.zeros_like(carry_ref)
        cur_ref[...] = jnp.zeros_like(cur_ref)
        nfree_ref[...] = jnp.zeros_like(nfree_ref)
        bev_ref[...] = jnp.full(bev_ref.shape, NE, F32)
        cp.wait()

    y = jnp.dot(mix_ref[...], w_vmem[...], preferred_element_type=F32)
    x1 = x_ref[...] + gate_ref[0] * y
    x1_ref[...] = x1
    ms = jnp.mean(x1 * x1, axis=-1, keepdims=True)
    h2 = x1 * lax.rsqrt(ms + NORM_EPS) * nw_ref[...]
    h2 = h2 * (1.0 + scale_ref[0]) + shift_ref[0]

    def scatter_wait():
        _tile_copy(hbuf, 0, xg_hbm, 0, 2 * OUT_TM, sem).wait()

    nt = (((1,), (1,)), ((), ()))
    wr = wr_ref[...]
    w_hi = wr.astype(BF16)
    w_lo = (wr - w_hi.astype(F32)).astype(BF16)
    h_hi = h2.astype(BF16)
    h_lo = (h2 - h_hi.astype(F32)).astype(BF16)
    p_hi = lax.dot_general(jnp.concatenate([w_hi, w_lo], axis=0), h_hi, nt, preferred_element_type=F32)
    p_lo = lax.dot_general(w_hi, h_lo, nt, preferred_element_type=F32)
    logits = p_hi[:NE] + p_hi[NE:] + p_lo
    scores = _sigmoid(logits)
    biased = scores + rb_ref[...]
    a = [biased[m * NEG:(m + 1) * NEG] for m in range(EPG)]
    sc = [scores[m * NEG:(m + 1) * NEG] for m in range(EPG)]
    p, q = jnp.maximum(a[0], a[1]), jnp.minimum(a[0], a[1])
    r, s = jnp.maximum(a[2], a[3]), jnp.minimum(a[2], a[3])
    group_score = jnp.maximum(p, r) + jnp.maximum(jnp.minimum(p, r), jnp.maximum(q, s))
    gidx = lax.broadcasted_iota(jnp.int32, group_score.shape, 0)
    gmax = jnp.max(group_score, axis=0, keepdims=True)
    g_sel = jnp.min(jnp.where(group_score == gmax, gidx, NEG), axis=0, keepdims=True)
    pick = gidx == g_sel
    v = [jnp.sum(jnp.where(pick, a[m], 0.0), axis=0, keepdims=True) for m in range(EPG)]
    u = [jnp.sum(jnp.where(pick, sc[m], 0.0), axis=0, keepdims=True) for m in range(EPG)]

    def first_argmax(vals):
        m = functools.reduce(jnp.maximum, vals)
        idx = jnp.full(m.shape, EPG - 1, jnp.int32)
        for t in range(EPG - 2, -1, -1):
            idx = jnp.where(vals[t] == m, t, idx)
        return idx

    def select(vals, idx):
        out = vals[EPG - 1]
        for t in range(EPG - 2, -1, -1):
            out = jnp.where(idx == t, vals[t], out)
        return out

    i1 = first_argmax(v)
    v2 = [jnp.where(i1 == t, -jnp.inf, v[t]) for t in range(EPG)]
    i2 = first_argmax(v2)
    s1 = select(u, i1)
    s2 = select(u, i2)
    denom = s1 + s2
    e1 = g_sel * EPG + i1
    e2 = g_sel * EPG + i2
    gw_ref[0:1, :] = s1 / denom
    gw_ref[1:2, :] = s2 / denom

    tm = e1.shape[1]
    eidx = lax.broadcasted_iota(jnp.int32, (NE, tm), 0)
    oh1 = eidx == e1
    oh2 = eidx == e2
    oh = jnp.where(oh1 | oh2, 1.0, 0.0)
    tr = lax.broadcasted_iota(jnp.int32, (tm, tm), 0)
    tc = lax.broadcasted_iota(jnp.int32, (tm, tm), 1)
    earlier = jnp.where(tr < tc, 1.0, 0.0).astype(BF16)
    carry = carry_ref[...]
    local = jnp.dot(oh.astype(BF16), earlier, preferred_element_type=F32)
    rank = local + carry
    count = jnp.sum(oh, axis=1, keepdims=True)
    total = carry + count
    carry_ref[...] = total

    inv_bm = 1.0 / EXP_BM
    blocks_before = jnp.floor((carry + (EXP_BM - 1)) * inv_bm)
    need = jnp.floor((total + (EXP_BM - 1)) * inv_bm) - blocks_before
    er = lax.broadcasted_iota(jnp.int32, (NE, NE), 0)
    ec = lax.broadcasted_iota(jnp.int32, (NE, NE), 1)
    lower = jnp.where(ec < er, 1.0, 0.0).astype(BF16)
    count_hi = jnp.floor(count * (1.0 / 16.0))
    lane_e = lax.broadcasted_iota(jnp.int32, (NE, LANES), 1)
    digits = jnp.where(lane_e == 0, need, jnp.where(lane_e == 1, count_hi, count - 16.0 * count_hi))
    sums = jnp.dot(lower, digits.astype(BF16), preferred_element_type=F32)
    nfree = nfree_ref[...]
    base = nfree + sums[:, 0:1]
    start = 16.0 * sums[:, 1:2] + sums[:, 2:3]
    cur = cur_ref[...]
    cur_ref[...] = jnp.where(need > 0, base + need - 1.0, cur)
    nfree_new = nfree + jnp.sum(need, axis=0, keepdims=True)
    nfree_ref[...] = nfree_new
    blk = lax.broadcasted_iota(jnp.int32, (NE, bev_ref.shape[1]), 1).astype(F32)
    hit = jnp.logical_and(blk >= base, blk < base + need)
    owner = jnp.sum(jnp.where(hit, eidx[:, 0:1].astype(F32), 0.0), axis=0, keepdims=True)
    taken = jnp.sum(jnp.where(hit, 1.0, 0.0), axis=0, keepdims=True)
    bev = jnp.where(taken > 0, owner, bev_ref[...])
    bev_ref[...] = bev
    be_ref[...] = bev.astype(jnp.int32)
    na_ref[...] = nfree_new.astype(jnp.int32)

    sorted_all = start + local
    pos1 = jnp.sum(jnp.where(oh1, sorted_all, 0.0), axis=0, keepdims=True).astype(jnp.int32)
    pos2 = jnp.sum(jnp.where(oh2, sorted_all, 0.0), axis=0, keepdims=True).astype(jnp.int32)
    pos_ref[0, 0:1, :] = pos1
    pos_ref[0, 1:2, :] = pos2

    as_row = lambda col: jnp.sum(jnp.where(lane_e == eidx[:, 0:1], col, 0.0), axis=0, keepdims=True)
    j0 = jnp.floor(carry * inv_bm)
    in_block = carry - j0 * EXP_BM
    n_a = jnp.minimum(count, EXP_BM - in_block)
    block_of = lambda j: jnp.where(j < blocks_before, cur, base + (j - blocks_before))
    seg_rows = [start, n_a, block_of(j0) * EXP_BM + in_block, count - n_a, block_of(j0 + 1.0) * EXP_BM]
    seg_v[...] = jnp.concatenate(
        [as_row(col).astype(jnp.int32) for col in seg_rows] + [jnp.zeros((3, LANES), jnp.int32)], axis=0)
    seg_ref[0] = seg_v[...]
    to_smem = pltpu.make_async_copy(seg_v, seg_s, ssem)
    to_smem.start()

    @pl.when(i >= 1)
    def _():
        scatter_wait()

    for part in range(2):
        prow = lax.broadcasted_iota(jnp.int32, (tm, tm), 0) + part * tm
        perm = jnp.where((prow == pos1) | (prow == pos2), 1.0, 0.0).astype(BF16)
        sorted_rows = jnp.dot(perm, h_hi, preferred_element_type=F32)
        _pack_rows(sorted_rows, hbuf.at[pl.ds(part * tm * PK_SUB, tm * PK_SUB)], tm)

    to_smem.wait()

    def scatter(group, carry_):
        _start_tile_runs(seg_s, lambda t, r, e: t[r, e], group, hbuf, xg_hbm, sem, to_slots=True)
        return carry_

    lax.fori_loop(0, NE // RUN_UNROLL, scatter, 0)

    @pl.when(i == last)
    def _():
        scatter_wait()
        lane1 = lax.broadcasted_iota(jnp.int32, (1, LANES), 1)
        cur_row = jnp.where(lane1 == NE, nfree_new, as_row(cur_ref[...]))
        tail_v[0:1, :] = cur_row.astype(jnp.int32)
        tail_v[1:2, :] = as_row(total).astype(jnp.int32)
        cp = pltpu.make_async_copy(tail_v, tail_s, ssem)
        cp.start()
        cp.wait()
        zbuf[...] = jnp.zeros_like(zbuf)

        def fills(wait):
            def per_expert(e, carry_):
                used = tail_s[1, e] & (EXP_BM - 1)
                first = tail_s[0, e] * EXP_BM + used
                npad = jnp.where(used > 0, EXP_BM - used, 0)

                def chunk(size):
                    off = npad & ~(2 * size - 1)

                    @pl.when((npad & size) != 0)
                    def _():
                        cpz = _tile_copy(zbuf, 0, xg_hbm, first + off, size, zsem)
                        cpz.wait() if wait else cpz.start()

                _pad_chunks(chunk)
                return carry_

            lax.fori_loop(0, NE, per_expert, 0)

            def per_block(b, carry_):
                cpz = _tile_copy(zbuf, 0, xg_hbm, b * EXP_BM, EXP_BM, zsem)
                cpz.wait() if wait else cpz.start()
                return carry_

            lax.fori_loop(tail_s[0, NE], nblk, per_block, 0)

        fills(False)
        fills(True)


def _outproj_call(mix, x2, w_bf, gate, shift, scale, nw, wr_t, rb_col, seq, nb):
    n = x2.shape[0]
    tiles_per_batch = seq // OUT_TM
    bmap = lambda i: (i // tiles_per_batch, 0, 0)
    row = lambda i: (i, 0)
    lane = lambda i: (0, i)
    const = lambda i: (0, 0)
    table = lambda i: (i, 0, 0)
    return pl.pallas_call(
        _outproj_kernel,
        grid=(n // OUT_TM,),
        in_specs=[
            pl.BlockSpec((OUT_TM, D), row),
            pl.BlockSpec((OUT_TM, D), row),
            pl.BlockSpec((1, 1, D), bmap),
            pl.BlockSpec((1, 1, D), bmap),
            pl.BlockSpec((1, 1, D), bmap),
            pl.BlockSpec((1, D), const),
            pl.BlockSpec((NE, D), const),
            pl.BlockSpec((NE, 1), const),
            pl.BlockSpec(memory_space=pl.ANY),
        ],
        out_specs=[
            pl.BlockSpec((OUT_TM, D), row),
            pl.BlockSpec((1, 2, OUT_TM), table),
            pl.BlockSpec((1, 8, LANES), table),
            pl.BlockSpec((2, OUT_TM), lane),
            pl.BlockSpec((1, BLK_LANES), const),
            pl.BlockSpec((1, 1), const),
            pl.BlockSpec(memory_space=pl.ANY),
        ],
        out_shape=[
            jax.ShapeDtypeStruct((n, D), F32),
            jax.ShapeDtypeStruct((n // OUT_TM, 2, OUT_TM), jnp.int32),
            jax.ShapeDtypeStruct((n // OUT_TM, 8, LANES), jnp.int32),
            jax.ShapeDtypeStruct((2, n), F32),
            jax.ShapeDtypeStruct((1, BLK_LANES), jnp.int32),
            jax.ShapeDtypeStruct((1, 1), jnp.int32),
            jax.ShapeDtypeStruct((nb * EXP_BM * PK_SUB, LANES), jnp.uint32),
        ],
        scratch_shapes=[
            pltpu.VMEM((NE, 1), F32),
            pltpu.VMEM((NE, 1), F32),
            pltpu.VMEM((1, BLK_LANES), F32),
            pltpu.VMEM((1, 1), F32),
            pltpu.VMEM((D, D), BF16),
            pltpu.VMEM((2 * OUT_TM * PK_SUB, LANES), jnp.uint32),
            pltpu.VMEM((8, LANES), jnp.int32),
            pltpu.SMEM((8, LANES), jnp.int32),
            pltpu.VMEM((2, LANES), jnp.int32),
            pltpu.SMEM((2, LANES), jnp.int32),
            pltpu.VMEM((EXP_BM * PK_SUB, LANES), jnp.uint32),
            pltpu.SemaphoreType.DMA(()),
            pltpu.SemaphoreType.DMA(()),
            pltpu.SemaphoreType.DMA(()),
        ],
        compiler_params=pltpu.CompilerParams(
            dimension_semantics=("arbitrary",), vmem_limit_bytes=BIG_VMEM_LIMIT),
        name="out_proj_router",
    )(mix, x2, gate, shift, scale, nw, wr_t, rb_col, w_bf)


def _pack_rows(v, o_ref, rows):
    hi = lax.bitcast_convert_type(v[:, :D // 2].astype(BF16).astype(F32), jnp.uint32)
    lo = lax.bitcast_convert_type(v[:, D // 2:].astype(BF16).astype(F32), jnp.uint32)
    packed = hi | (lo >> 16)
    for j in range(PK_SUB):
        o_ref[pl.ds(j, rows, stride=PK_SUB), :] = packed[:, j * LANES:(j + 1) * LANES]


def _unpack_halves(x_ref, rows):
    his, los = [], []
    for j in range(PK_SUB):
        p = x_ref[pl.ds(j, rows, stride=PK_SUB), :]
        his.append(lax.bitcast_convert_type(p & jnp.uint32(0xFFFF0000), F32))
        los.append(lax.bitcast_convert_type(p << 16, F32))
    return his, los


def _tile_rows(row, nrows):
    start = row * PK_SUB
    if not isinstance(row, int):
        start = pl.multiple_of(start, PK_SUB)
    return pl.ds(start, nrows * PK_SUB)


def _tile_copy(src, src_row, dst, dst_row, nrows, sem):
    return pltpu.make_async_copy(src.at[_tile_rows(src_row, nrows), :], dst.at[_tile_rows(dst_row, nrows), :], sem)


def _if_rare(cond, body):
    def once(_, carry):
        body()
        return carry

    lax.fori_loop(0, cond.astype(jnp.int32), once, 0)


RUN_SIZES = [OUT_TM >> s for s in range(OUT_TM.bit_length())]
RUN_SIZES_SHORT = [s for s in RUN_SIZES if s < RUN_SHORT]
RUN_SIZES_LONG = [s for s in RUN_SIZES if s >= RUN_SHORT]


def _start_run(src, src_row, dst, dst_row, n, sem, sizes):
    for size in sizes:
        off = n & ~(2 * size - 1)

        @pl.when((n & size) != 0)
        def _(size=size, off=off):
            _tile_copy(src, src_row + off, dst, dst_row + off, size, sem).start()


def _start_tile_runs(table, read, group, src, dst, sem, to_slots):
    runs = []
    for u in range(RUN_UNROLL):
        e = group * RUN_UNROLL + u
        start, n_first, slot_first, n_rest, slot_rest = [read(table, r, e) for r in range(5)]
        first = (start, slot_first) if to_slots else (slot_first, start)
        rest = (start + n_first, slot_rest) if to_slots else (slot_rest, start + n_first)
        runs.append((first, n_first, rest, n_rest))
    for first, n_first, _, _ in runs:
        _start_run(src, first[0], dst, first[1], n_first, sem, RUN_SIZES_SHORT)
    rare = [(nf >= RUN_SHORT) | (nr > 0) for _, nf, _, nr in runs]

    def long_and_rest():
        for (first, n_first, rest, n_rest), cond in zip(runs, rare):
            def one(first=first, n_first=n_first, rest=rest, n_rest=n_rest):
                _start_run(src, first[0], dst, first[1], n_first, sem, RUN_SIZES_LONG)
                _start_run(src, rest[0], dst, rest[1], n_rest, sem, RUN_SIZES)

            _if_rare(cond, one)

    _if_rare(functools.reduce(jnp.logical_or, rare), long_and_rest)


def _pad_chunks(fn):
    size = EXP_BM // 2
    while size >= 1:
        fn(size)
        size //= 2


def _expert_kernel(order_ref, be_ref, next_ref, na_ref, x_ref, w1_hbm, w3_hbm, w2_hbm, o_ref,
                   s1, s3, s2, w1b, w3b, w2b, sem, *, layer):
    i = pl.program_id(0)
    active = i < na_ref[0]
    changed = jnp.logical_or(i == 0, be_ref[i] != be_ref[jnp.maximum(i - 1, 0)])

    def weight_copies(e):
        return [pltpu.make_async_copy(w_hbm.at[layer, e], stage, sem)
                for w_hbm, stage in ((w1_hbm, s1), (w3_hbm, s3), (w2_hbm, s2))]

    @pl.when(jnp.logical_and(active, changed))
    def _():
        @pl.when(i == 0)
        def _():
            for cp in weight_copies(be_ref[0]):
                cp.start()

        for cp in weight_copies(be_ref[i]):
            cp.wait()
        w1b[...] = s1[...].astype(BF16)
        w3b[...] = s3[...].astype(BF16)
        w2b[...] = s2[...].astype(BF16)
        nxt = next_ref[i]

        @pl.when(nxt >= 0)
        def _():
            for cp in weight_copies(nxt):
                cp.start()

    @pl.when(active)
    def _():
        rows = EXP_BM // EXP_SUB
        for s in range(EXP_SUB):
            part = pl.ds(s * rows * PK_SUB, rows * PK_SUB)
            his, los = _unpack_halves(x_ref.at[part], rows)
            xb = jnp.concatenate([c.astype(BF16) for c in his + los], axis=1)
            h1 = jnp.dot(xb, w1b[...], preferred_element_type=F32)
            h3 = jnp.dot(xb, w3b[...], preferred_element_type=F32)
            act = (h1 * _sigmoid(h1) * h3).astype(BF16)
            _pack_rows(jnp.dot(act, w2b[...], preferred_element_type=F32), o_ref.at[part], rows)

    @pl.when(jnp.logical_not(active))
    def _():
        o_ref[...] = jnp.zeros_like(o_ref)


def _expert_call(order, block_expert, next_expert, n_active, xg, w1, w3, w2, layer):
    nb = order.shape[0]
    live = lambda i, od, be, nx, na: (od[jnp.minimum(i, na[0] - 1)], 0)
    grid_spec = pltpu.PrefetchScalarGridSpec(
        num_scalar_prefetch=4,
        grid=(nb,),
        in_specs=[
            pl.BlockSpec((EXP_BM * PK_SUB, LANES), live),
            pl.BlockSpec(memory_space=pl.ANY),
            pl.BlockSpec(memory_space=pl.ANY),
            pl.BlockSpec(memory_space=pl.ANY),
        ],
        out_specs=pl.BlockSpec((EXP_BM * PK_SUB, LANES), lambda i, od, be, nx, na: (od[i], 0)),
        scratch_shapes=[pltpu.VMEM((D, FF), F32), pltpu.VMEM((D, FF), F32), pltpu.VMEM((FF, D), F32),
                        pltpu.VMEM((D, FF), BF16), pltpu.VMEM((D, FF), BF16), pltpu.VMEM((FF, D), BF16),
                        pltpu.SemaphoreType.DMA(())],
    )
    return pl.pallas_call(
        functools.partial(_expert_kernel, layer=layer),
        grid_spec=grid_spec,
        out_shape=jax.ShapeDtypeStruct((nb * EXP_BM * PK_SUB, LANES), jnp.uint32),
        compiler_params=pltpu.CompilerParams(
            dimension_semantics=("arbitrary",), vmem_limit_bytes=VMEM_LIMIT),
        name="expert_blocks",
    )(order, block_expert, next_expert, n_active, xg, w1, w3, w2)


def _combine_kernel(seg_ref, seg_next_ref, x_ref, gate_ref, gw_ref, pos_ref, fw_ref, y_hbm, o_ref, ybuf, sem, *,
                    final_norm):
    i = pl.program_id(0)
    last = pl.num_programs(0) - 1

    def fetch(table, parity):
        dst = ybuf.at[parity]

        def gather(group, carry):
            _start_tile_runs(table, lambda t, r, e: t[0, r, e], group, y_hbm, dst, sem.at[parity], to_slots=False)
            return carry

        lax.fori_loop(0, NE // RUN_UNROLL, gather, 0)

    par = i % 2

    @pl.when(i == 0)
    def _():
        fetch(seg_ref, 0)

    @pl.when(i < last)
    def _():
        fetch(seg_next_ref, 1 - par)

    cur = ybuf.at[par]
    _tile_copy(y_hbm, 0, cur, 0, 2 * CMB_TM, sem.at[par]).wait()

    his, los = _unpack_halves(cur, 2 * CMB_TM)
    ys = jnp.concatenate([c.astype(BF16) for c in his + los], axis=1)
    gw = gw_ref[...]
    pos = pos_ref[0]
    prow = lax.broadcasted_iota(jnp.int32, (2 * CMB_TM, CMB_TM), 0)
    selector = jnp.where(prow == pos[0:1], gw[0:1], jnp.where(prow == pos[1:2], gw[1:2], 0.0))
    moe = lax.dot_general(selector.astype(BF16), ys, (((0,), (0,)), ((), ())), preferred_element_type=F32)
    x2 = x_ref[...] + gate_ref[0] * moe
    if final_norm:
        ms = jnp.mean(x2 * x2, axis=-1, keepdims=True)
        x2 = x2 * lax.rsqrt(ms + NORM_EPS) * fw_ref[...]
    o_ref[...] = x2


def _combine_call(runs, x1, gate, gw, sorted_pos, fw, yg, seq, final_norm):
    n = x1.shape[0]
    tiles_per_batch = seq // CMB_TM
    nt = n // CMB_TM
    return pl.pallas_call(
        functools.partial(_combine_kernel, final_norm=final_norm),
        grid=(nt,),
        in_specs=[
            pl.BlockSpec((1, 8, LANES), lambda i: (i, 0, 0), memory_space=pltpu.SMEM),
            pl.BlockSpec((1, 8, LANES), lambda i: (jnp.minimum(i + 1, nt - 1), 0, 0), memory_space=pltpu.SMEM),
            pl.BlockSpec((CMB_TM, D), lambda i: (i, 0)),
            pl.BlockSpec((1, 1, D), lambda i: (i // tiles_per_batch, 0, 0)),
            pl.BlockSpec((2, CMB_TM), lambda i: (0, i)),
            pl.BlockSpec((1, 2, CMB_TM), lambda i: (i, 0, 0)),
            pl.BlockSpec((1, D), lambda i: (0, 0)),
            pl.BlockSpec(memory_space=pl.ANY),
        ],
        out_specs=pl.BlockSpec((CMB_TM, D), lambda i: (i, 0)),
        out_shape=jax.ShapeDtypeStruct((n, D), F32),
        scratch_shapes=[pltpu.VMEM((2, 2 * CMB_TM * PK_SUB, LANES), jnp.uint32), pltpu.SemaphoreType.DMA((2,))],
        compiler_params=pltpu.CompilerParams(
            dimension_semantics=("arbitrary",), vmem_limit_bytes=VMEM_LIMIT),
        name="moe_combine",
    )(runs, runs, x1, gate, gw, sorted_pos, fw, yg)


def _decay_tables():
    log_gamma = jnp.log1p(-jnp.exp2(-5.0 - jnp.arange(NH, dtype=F32)))
    pos = jnp.arange(RCH, dtype=F32)
    diff = pos[:, None] - pos[None, :]
    mask = jnp.where(diff >= 0, jnp.exp(log_gamma[:, None, None] * jnp.maximum(diff, 0.0)), 0.0)
    q_decay = jnp.exp(log_gamma[:, None] * (pos + 1.0))
    k_decay = jnp.exp(log_gamma[:, None] * (RCH - 1.0 - pos))
    chunk_decay = jnp.exp(log_gamma * RCH)
    return mask.astype(F32), q_decay.T, k_decay.T, chunk_decay


def kernel(x, c, positions, w_ada, b_ada, norm1_w, norm2_w, w_in, w_out, ret_norm_w, sg_w_s, sg_b_s,
           w_router, router_bias, w1, w3, w2, final_norm_w):
    bsz, seq, d = x.shape
    n = bsz * seq
    assert d == D and seq % INP_TM == 0 and seq % MIX_ROWS == 0 and n % CMB_TM == 0

    assert bsz <= 8
    c_cols = jnp.zeros((D, 8), F32).at[:, :bsz].set(c.T)
    mod = _ada_call(c_cols, w_ada, b_ada, bsz)[:, :bsz]
    mod = mod.reshape(N_LAYERS, bsz, N_MOD, 1, D)

    half = HD // 2
    inv_freq = (ROPE_THETA ** (-jnp.arange(half, dtype=F32) / half)).reshape(1, half)
    pos_col = positions.reshape(n, 1)
    decay_mask, q_decay_t, k_decay_t, chunk_decay = _decay_tables()

    perm = (jnp.arange(NE) % NEG) * EPG + jnp.arange(NE) // NEG
    wr_t = w_router.T[perm]
    rb_col = router_bias[perm].reshape(NE, 1)

    nb = n * 2 // EXP_BM + NE
    assert nb <= BLK_LANES and n % OUT_TM == 0
    xs = x.reshape(n, D)
    for l in range(N_LAYERS):
        shift1, scale1, gate1, shift2, scale2, gate2 = [mod[l, :, t] for t in range(N_MOD)]
        proj = _inproj_call(xs, shift1, scale1, norm1_w[l].reshape(1, D), w_in, l, pos_col, inv_freq, seq)
        mix = _mix_call(proj, chunk_decay, decay_mask, q_decay_t, k_decay_t, ret_norm_w[l].reshape(1, RW),
                        sg_w_s[l], sg_b_s[l].T, bsz, seq)
        x1, sorted_pos, runs, gw, owner, n_active, xg = _outproj_call(
            mix, xs, w_out[l].astype(BF16), gate1, shift2, scale2, norm2_w[l].reshape(1, D), wr_t, rb_col, seq, nb)

        owner = owner.reshape(BLK_LANES)
        ids = jnp.arange(BLK_LANES, dtype=jnp.int32)
        ahead = (owner[None, :] < owner[:, None]) | ((owner[None, :] == owner[:, None]) & (ids[None, :] < ids[:, None]))
        pos = jnp.sum(ahead, axis=1)
        at = pos[None, :] == ids[:, None]
        order = jnp.sum(jnp.where(at, ids[None, :], 0), axis=1)[:nb].astype(jnp.int32)
        owner_sorted = jnp.sum(jnp.where(at, owner[None, :], 0), axis=1)[:nb]
        block_expert = jnp.minimum(owner_sorted, NE - 1).astype(jnp.int32)
        later = jnp.min(jnp.where(owner_sorted[None, :] > owner_sorted[:, None], owner_sorted[None, :], NE), axis=1)
        next_expert = jnp.where(later >= NE, -1, later).astype(jnp.int32)

        yg = _expert_call(order, block_expert, next_expert, n_active.reshape(1), xg, w1, w3, w2, l)
        xs = _combine_call(runs, x1, gate2, gw, sorted_pos, final_norm_w.reshape(1, D), yg, seq,
                           final_norm=(l == N_LAYERS - 1))
    return xs.reshape(bsz, seq, D)
```

```python
import functools

import jax
import jax.numpy as jnp
from jax import lax
from jax.experimental import pallas as pl
from jax.experimental.pallas import tpu as pltpu

F32 = jnp.float32
BF16 = jnp.bfloat16

D = 2048
N_LAYERS = 2
RW = D // 2
NH = 4
HD = RW // NH
CH = 128
RCH = 128
ROPE_THETA = 10000.0
SGW = D - RW
NG = 8
GD = SGW // NG
PROJ = 4 * RW + 2 * SGW
NE = 32
NEG = 8
EPG = NE // NEG
FF = 512
N_MOD = 6
NORM_EPS = 1e-6

VMEM_LIMIT = 48 * 1024 * 1024
BIG_VMEM_LIMIT = 56 * 1024 * 1024

ADA_TN = 1024
INP_TM = 256
INP_TN = 512
MIX_ROWS = 1024
OUT_TM = 512
RUN_UNROLL = 4
RUN_SHORT = 64
EXP_BM = 512
EXP_SUB = 2
CMB_TM = OUT_TM
BLK_LANES = 256

LANES = 128
SUBLANES = 8
PK_SUB = D // 2 // LANES
COUNT_DIGIT = 16


def _sigmoid(v):
    return 1.0 / (1.0 + jnp.exp(-v))


def _gelu_tanh(v):
    return 0.5 * v * (1.0 + jnp.tanh(0.7978845608028654 * (v + 0.044715 * (v * v * v))))


def _ada_kernel(c_ref, w_ref, b_ref, o_ref, *, bsz):
    c = c_ref[...]
    ca = c * _sigmoid(c)
    w = w_ref[0]
    rows = [jnp.sum(w * ca[:, b:b + 1], axis=0, keepdims=True) for b in range(bsz)]
    rows.append(jnp.zeros((SUBLANES - bsz, ADA_TN), F32))
    o_ref[0] = jnp.concatenate(rows, axis=0) + b_ref[0]


def _ada_call(c_cols, w_ada, b_ada, bsz):
    depth, _, ncol = w_ada.shape
    return pl.pallas_call(
        functools.partial(_ada_kernel, bsz=bsz),
        grid=(depth, ncol // ADA_TN),
        in_specs=[
            pl.BlockSpec((D, SUBLANES), lambda l, j: (0, 0)),
            pl.BlockSpec((1, D, ADA_TN), lambda l, j: (l, 0, j)),
            pl.BlockSpec((1, 1, ADA_TN), lambda l, j: (l, 0, j)),
        ],
        out_specs=pl.BlockSpec((1, 8, ADA_TN), lambda l, j: (l, 0, j)),
        out_shape=jax.ShapeDtypeStruct((depth, 8, ncol), F32),
        compiler_params=pltpu.CompilerParams(vmem_limit_bytes=VMEM_LIMIT),
        name="ada_mod",
    )(c_cols, w_ada, b_ada.reshape(depth, 1, ncol))


def _inproj_kernel(x_ref, shift_ref, scale_ref, nw_ref, pos_ref, freq_ref, w_hbm, o_ref, w_vmem, stage, sem, *,
                   layer):
    @pl.when(pl.program_id(0) == 0)
    def _():
        def panel(j):
            cols = pl.ds(j * INP_TN, INP_TN)
            return pltpu.make_async_copy(w_hbm.at[layer, :, cols], stage.at[j % 2], sem.at[j % 2])

        npanel = PROJ // INP_TN
        panel(0).start()
        for j in range(npanel):
            if j + 1 < npanel:
                panel(j + 1).start()
            panel(j).wait()
            w_vmem[:, j * INP_TN:(j + 1) * INP_TN] = stage[j % 2].astype(BF16)

    x = x_ref[...]
    ms = jnp.mean(x * x, axis=-1, keepdims=True)
    h = x * lax.rsqrt(ms + NORM_EPS) * nw_ref[...]
    h = (h * (1.0 + scale_ref[0]) + shift_ref[0]).astype(BF16)
    ang = pos_ref[...].astype(F32) * freq_ref[...]
    cos = jnp.cos(ang)
    sin = jnp.sin(ang)
    half = HD // 2
    n_rot = 2 * RW // INP_TN
    order = list(range(n_rot, PROJ // INP_TN)) + list(range(n_rot))

    for j in order:
        c0 = j * INP_TN
        acc = jnp.dot(h, w_vmem[:, c0:c0 + INP_TN], preferred_element_type=F32)
        sec = c0 // RW
        if sec in (0, 1):
            scale = 1.0 if sec == 0 else HD ** -0.5
            for hh in range(INP_TN // HD):
                a = acc[:, hh * HD:hh * HD + half]
                b = acc[:, hh * HD + half:(hh + 1) * HD]
                o_ref[:, c0 + hh * HD:c0 + hh * HD + half] = ((a * cos - b * sin) * scale).astype(BF16)
                o_ref[:, c0 + hh * HD + half:c0 + (hh + 1) * HD] = ((b * cos + a * sin) * scale).astype(BF16)
        elif sec == 2:
            o_ref[:, c0:c0 + INP_TN] = acc.astype(BF16)
        elif sec == 3:
            o_ref[:, c0:c0 + INP_TN] = (acc * _sigmoid(acc)).astype(BF16)
        elif sec == 4:
            o_ref[:, c0:c0 + INP_TN] = _gelu_tanh(acc).astype(BF16)
        else:
            for gg in range(INP_TN // GD):
                t = _gelu_tanh(acc[:, gg * GD:(gg + 1) * GD])
                ms_g = jnp.mean(t * t, axis=-1, keepdims=True)
                o_ref[:, c0 + gg * GD:c0 + (gg + 1) * GD] = (t * lax.rsqrt(ms_g + NORM_EPS)).astype(BF16)


def _inproj_call(x2, shift, scale, nw, w_in, layer, pos_col, inv_freq, seq):
    n = x2.shape[0]
    tiles_per_batch = seq // INP_TM
    bmap = lambda i: (i // tiles_per_batch, 0, 0)
    return pl.pallas_call(
        functools.partial(_inproj_kernel, layer=layer),
        grid=(n // INP_TM,),
        in_specs=[
            pl.BlockSpec((INP_TM, D), lambda i: (i, 0)),
            pl.BlockSpec((1, 1, D), bmap),
            pl.BlockSpec((1, 1, D), bmap),
            pl.BlockSpec((1, D), lambda i: (0, 0)),
            pl.BlockSpec((INP_TM, 1), lambda i: (i, 0)),
            pl.BlockSpec((1, HD // 2), lambda i: (0, 0)),
            pl.BlockSpec(memory_space=pl.ANY),
        ],
        out_specs=pl.BlockSpec((INP_TM, PROJ), lambda i: (i, 0)),
        out_shape=jax.ShapeDtypeStruct((n, PROJ), BF16),
        scratch_shapes=[pltpu.VMEM((D, PROJ), BF16), pltpu.VMEM((2, D, INP_TN), F32),
                        pltpu.SemaphoreType.DMA((2,))],
        compiler_params=pltpu.CompilerParams(
            dimension_semantics=("arbitrary",), vmem_limit_bytes=BIG_VMEM_LIMIT),
        name="in_proj",
    )(x2, shift, scale, nw, pos_col, inv_freq, w_in)


def _mix_kernel(cd_ref, q_ref, k_ref, v_ref, g_ref, u_ref, vs_ref, dm_ref, qd_ref, kd_ref,
                rnw_ref, ws_ref, bs_ref, o_ref, state_ref):
    @pl.when(pl.program_id(1) == 0)
    def _():
        state_ref[...] = jnp.zeros_like(state_ref)

    row = lax.broadcasted_iota(jnp.int32, (CH, CH), 0)
    col = lax.broadcasted_iota(jnp.int32, (CH, CH), 1)
    causal = row >= col

    chunks = [slice(c * CH, (c + 1) * CH) for c in range(MIX_ROWS // CH)]

    for g in range(NG):
        cols = slice(g * GD, (g + 1) * GD)
        wm = jnp.where(causal, ws_ref[g], 0.0).astype(BF16)
        vs_wide = jnp.concatenate([vs_ref[rows, cols] for rows in chunks], axis=1)
        z = jnp.dot(wm, vs_wide, preferred_element_type=F32) + bs_ref[:, g:g + 1]
        for c, rows in enumerate(chunks):
            zc = z[:, c * GD:(c + 1) * GD]
            o_ref[rows, RW + g * GD:RW + (g + 1) * GD] = (u_ref[rows, cols].astype(F32) * zc).astype(BF16)

    for h in range(NH):
        cols = slice(h * HD, (h + 1) * HD)
        st = state_ref[h]
        for rows in [slice(c * RCH, (c + 1) * RCH) for c in range(MIX_ROWS // RCH)]:
            q = q_ref[rows, cols]
            k = k_ref[rows, cols]
            v = v_ref[rows, cols]
            s = lax.dot_general(q, k, (((1,), (1,)), ((), ())), preferred_element_type=F32)
            s = s * dm_ref[h]
            intra = jnp.dot(s.astype(BF16), v, preferred_element_type=F32)
            cross = jnp.dot(q, st.astype(BF16), preferred_element_type=F32) * qd_ref[:, h:h + 1]
            kdec = (k.astype(F32) * kd_ref[:, h:h + 1]).astype(BF16)
            upd = lax.dot_general(kdec, v, (((0,), (0,)), ((), ())), preferred_element_type=F32)
            st = st * cd_ref[h] + upd
            o = intra + cross
            ms = jnp.mean(o * o, axis=-1, keepdims=True)
            o = o * lax.rsqrt(ms + NORM_EPS) * rnw_ref[:, cols]
            o_ref[rows, cols] = (o * g_ref[rows, cols].astype(F32)).astype(BF16)
        state_ref[h] = st


def _mix_call(proj, chunk_decay, decay_mask, q_decay_t, k_decay_t, rnw, ws, bs_t, bsz, seq):
    n = proj.shape[0]
    steps = seq // MIX_ROWS

    def colblk(jb):
        return pl.BlockSpec((MIX_ROWS, RW), lambda b, s, cd: (b * steps + s, jb))

    const2 = lambda b, s, cd: (0, 0)
    const3 = lambda b, s, cd: (0, 0, 0)
    grid_spec = pltpu.PrefetchScalarGridSpec(
        num_scalar_prefetch=1,
        grid=(bsz, steps),
        in_specs=[colblk(0), colblk(1), colblk(2), colblk(3), colblk(4), colblk(5),
                  pl.BlockSpec((NH, RCH, RCH), const3),
                  pl.BlockSpec((RCH, NH), const2),
                  pl.BlockSpec((RCH, NH), const2),
                  pl.BlockSpec((1, RW), const2),
                  pl.BlockSpec((NG, CH, CH), const3),
                  pl.BlockSpec((CH, NG), const2)],
        out_specs=pl.BlockSpec((MIX_ROWS, D), lambda b, s, cd: (b * steps + s, 0)),
        scratch_shapes=[pltpu.VMEM((NH, HD, HD), F32)],
    )
    return pl.pallas_call(
        _mix_kernel,
        grid_spec=grid_spec,
        out_shape=jax.ShapeDtypeStruct((n, D), BF16),
        compiler_params=pltpu.CompilerParams(
            dimension_semantics=("arbitrary", "arbitrary"), vmem_limit_bytes=VMEM_LIMIT),
        name="retention_spatial_mix",
    )(chunk_decay, proj, proj, proj, proj, proj, proj, decay_mask, q_decay_t, k_decay_t, rnw, ws, bs_t)


def _outproj_kernel(mix_ref, x_ref, gate_ref, shift_ref, scale_ref, nw_ref, wr_ref, rb_ref, w_hbm,
                    x1_ref, pos_ref, seg_ref, gw_ref, be_ref, na_ref, xg_hbm,
                    carry_ref, cur_ref, bev_ref, nfree_ref, w_vmem, hbuf, seg_v, seg_s, tail_v, tail_s, zbuf,
                    sem, ssem, zsem):
    i = pl.program_id(0)
    last = pl.num_programs(0) - 1
    nblk = xg_hbm.shape[0] // (EXP_BM * PK_SUB)

    @pl.when(i == 0)
    def _():
        cp = pltpu.make_async_copy(w_hbm, w_vmem, ssem)
        cp.start()
        carry_ref[...] = jnp.zeros_like(carry_ref)
        cur_ref[...] = jnp.zeros_like(cur_ref)
        nfree_ref[...] = jnp.zeros_like(nfree_ref)
        bev_ref[...] = jnp.full(bev_ref.shape, NE, F32)
        cp.wait()

    y = jnp.dot(mix_ref[...], w_vmem[...], preferred_element_type=F32)
    x1 = x_ref[...] + gate_ref[0] * y
    x1_ref[...] = x1
    ms = jnp.mean(x1 * x1, axis=-1, keepdims=True)
    h2 = x1 * lax.rsqrt(ms + NORM_EPS) * nw_ref[...]
    h2 = h2 * (1.0 + scale_ref[0]) + shift_ref[0]

    def scatter_wait():
        _tile_copy(hbuf, 0, xg_hbm, 0, 2 * OUT_TM, sem).wait()

    nt = (((1,), (1,)), ((), ()))
    wr = wr_ref[...]
    w_hi = wr.astype(BF16)
    w_lo = (wr - w_hi.astype(F32)).astype(BF16)
    h_hi = h2.astype(BF16)
    h_lo = (h2 - h_hi.astype(F32)).astype(BF16)
    p_hi = lax.dot_general(jnp.concatenate([w_hi, w_lo], axis=0), h_hi, nt, preferred_element_type=F32)
    p_lo = lax.dot_general(w_hi, h_lo, nt, preferred_element_type=F32)
    logits = p_hi[:NE] + p_hi[NE:] + p_lo
    scores = _sigmoid(logits)
    biased = scores + rb_ref[...]
    a = [biased[m * NEG:(m + 1) * NEG] for m in range(EPG)]
    sc = [scores[m * NEG:(m + 1) * NEG] for m in range(EPG)]
    p, q = jnp.maximum(a[0], a[1]), jnp.minimum(a[0], a[1])
    r, s = jnp.maximum(a[2], a[3]), jnp.minimum(a[2], a[3])
    group_score = jnp.maximum(p, r) + jnp.maximum(jnp.minimum(p, r), jnp.maximum(q, s))
    gidx = lax.broadcasted_iota(jnp.int32, group_score.shape, 0)
    gmax = jnp.max(group_score, axis=0, keepdims=True)
    g_sel = jnp.min(jnp.where(group_score == gmax, gidx, NEG), axis=0, keepdims=True)
    pick = gidx == g_sel
    v = [jnp.sum(jnp.where(pick, a[m], 0.0), axis=0, keepdims=True) for m in range(EPG)]
    u = [jnp.sum(jnp.where(pick, sc[m], 0.0), axis=0, keepdims=True) for m in range(EPG)]

    def first_argmax(vals):
        m = functools.reduce(jnp.maximum, vals)
        idx = jnp.full(m.shape, EPG - 1, jnp.int32)
        for t in range(EPG - 2, -1, -1):
            idx = jnp.where(vals[t] == m, t, idx)
        return idx

    def select(vals, idx):
        out = vals[EPG - 1]
        for t in range(EPG - 2, -1, -1):
            out = jnp.where(idx == t, vals[t], out)
        return out

    i1 = first_argmax(v)
    v2 = [jnp.where(i1 == t, -jnp.inf, v[t]) for t in range(EPG)]
    i2 = first_argmax(v2)
    s1 = select(u, i1)
    s2 = select(u, i2)
    denom = s1 + s2
    e1 = g_sel * EPG + i1
    e2 = g_sel * EPG + i2
    gw_ref[0:1, :] = s1 / denom
    gw_ref[1:2, :] = s2 / denom

    tm = e1.shape[1]
    eidx = lax.broadcasted_iota(jnp.int32, (NE, tm), 0)
    oh1 = eidx == e1
    oh2 = eidx == e2
    oh = jnp.where(oh1 | oh2, 1.0, 0.0)
    tr = lax.broadcasted_iota(jnp.int32, (tm, tm), 0)
    tc = lax.broadcasted_iota(jnp.int32, (tm, tm), 1)
    earlier = jnp.where(tr < tc, 1.0, 0.0).astype(BF16)
    carry = carry_ref[...]
    local = jnp.dot(oh.astype(BF16), earlier, preferred_element_type=F32)
    rank = local + carry
    count = jnp.sum(oh, axis=1, keepdims=True)
    total = carry + count
    carry_ref[...] = total

    inv_bm = 1.0 / EXP_BM
    blocks_before = jnp.floor((carry + (EXP_BM - 1)) * inv_bm)
    need = jnp.floor((total + (EXP_BM - 1)) * inv_bm) - blocks_before
    er = lax.broadcasted_iota(jnp.int32, (NE, NE), 0)
    ec = lax.broadcasted_iota(jnp.int32, (NE, NE), 1)
    lower = jnp.where(ec < er, 1.0, 0.0).astype(BF16)
    count_hi = jnp.floor(count * (1.0 / COUNT_DIGIT))
    lane_e = lax.broadcasted_iota(jnp.int32, (NE, LANES), 1)
    digits = jnp.where(lane_e == 0, need, jnp.where(lane_e == 1, count_hi, count - COUNT_DIGIT * count_hi))
    sums = jnp.dot(lower, digits.astype(BF16), preferred_element_type=F32)
    nfree = nfree_ref[...]
    base = nfree + sums[:, 0:1]
    start = COUNT_DIGIT * sums[:, 1:2] + sums[:, 2:3]
    cur = cur_ref[...]
    cur_ref[...] = jnp.where(need > 0, base + need - 1.0, cur)
    nfree_new = nfree + jnp.sum(need, axis=0, keepdims=True)
    nfree_ref[...] = nfree_new
    blk = lax.broadcasted_iota(jnp.int32, (NE, bev_ref.shape[1]), 1).astype(F32)
    hit = jnp.logical_and(blk >= base, blk < base + need)
    owner = jnp.sum(jnp.where(hit, eidx[:, 0:1].astype(F32), 0.0), axis=0, keepdims=True)
    taken = jnp.sum(jnp.where(hit, 1.0, 0.0), axis=0, keepdims=True)
    bev = jnp.where(taken > 0, owner, bev_ref[...])
    bev_ref[...] = bev
    be_ref[...] = bev.astype(jnp.int32)
    na_ref[...] = nfree_new.astype(jnp.int32)

    sorted_all = start + local
    pos1 = jnp.sum(jnp.where(oh1, sorted_all, 0.0), axis=0, keepdims=True).astype(jnp.int32)
    pos2 = jnp.sum(jnp.where(oh2, sorted_all, 0.0), axis=0, keepdims=True).astype(jnp.int32)
    pos_ref[0, 0:1, :] = pos1
    pos_ref[0, 1:2, :] = pos2

    as_row = lambda col: jnp.sum(jnp.where(lane_e == eidx[:, 0:1], col, 0.0), axis=0, keepdims=True)
    j0 = jnp.floor(carry * inv_bm)
    in_block = carry - j0 * EXP_BM
    n_a = jnp.minimum(count, EXP_BM - in_block)
    block_of = lambda j: jnp.where(j < blocks_before, cur, base + (j - blocks_before))
    seg_rows = [start, n_a, block_of(j0) * EXP_BM + in_block, count - n_a, block_of(j0 + 1.0) * EXP_BM]
    seg_v[...] = jnp.concatenate(
        [as_row(col).astype(jnp.int32) for col in seg_rows]
        + [jnp.zeros((SUBLANES - len(seg_rows), LANES), jnp.int32)], axis=0)
    seg_ref[0] = seg_v[...]
    to_smem = pltpu.make_async_copy(seg_v, seg_s, ssem)
    to_smem.start()

    @pl.when(i >= 1)
    def _():
        scatter_wait()

    for part in range(2):
        prow = lax.broadcasted_iota(jnp.int32, (tm, tm), 0) + part * tm
        perm = jnp.where((prow == pos1) | (prow == pos2), 1.0, 0.0).astype(BF16)
        sorted_rows = jnp.dot(perm, h_hi, preferred_element_type=F32)
        _pack_rows(sorted_rows, hbuf.at[pl.ds(part * tm * PK_SUB, tm * PK_SUB)], tm)

    to_smem.wait()

    def scatter(group, carry_):
        _start_tile_runs(seg_s, lambda t, r, e: t[r, e], group, hbuf, xg_hbm, sem, to_slots=True)
        return carry_

    lax.fori_loop(0, NE // RUN_UNROLL, scatter, 0)

    @pl.when(i == last)
    def _():
        scatter_wait()
        lane1 = lax.broadcasted_iota(jnp.int32, (1, LANES), 1)
        cur_row = jnp.where(lane1 == NE, nfree_new, as_row(cur_ref[...]))
        tail_v[0:1, :] = cur_row.astype(jnp.int32)
        tail_v[1:2, :] = as_row(total).astype(jnp.int32)
        cp = pltpu.make_async_copy(tail_v, tail_s, ssem)
        cp.start()
        cp.wait()
        zbuf[...] = jnp.zeros_like(zbuf)

        def fills(wait):
            def per_expert(e, carry_):
                used = tail_s[1, e] & (EXP_BM - 1)
                first = tail_s[0, e] * EXP_BM + used
                npad = jnp.where(used > 0, EXP_BM - used, 0)

                def chunk(size):
                    off = npad & ~(2 * size - 1)

                    @pl.when((npad & size) != 0)
                    def _():
                        cpz = _tile_copy(zbuf, 0, xg_hbm, first + off, size, zsem)
                        cpz.wait() if wait else cpz.start()

                _pad_chunks(chunk)
                return carry_

            lax.fori_loop(0, NE, per_expert, 0)

            def per_block(b, carry_):
                cpz = _tile_copy(zbuf, 0, xg_hbm, b * EXP_BM, EXP_BM, zsem)
                cpz.wait() if wait else cpz.start()
                return carry_

            lax.fori_loop(tail_s[0, NE], nblk, per_block, 0)

        fills(False)
        fills(True)


def _outproj_call(mix, x2, w_bf, gate, shift, scale, nw, wr_t, rb_col, seq, nb):
    n = x2.shape[0]
    tiles_per_batch = seq // OUT_TM
    bmap = lambda i: (i // tiles_per_batch, 0, 0)
    row = lambda i: (i, 0)
    lane = lambda i: (0, i)
    const = lambda i: (0, 0)
    table = lambda i: (i, 0, 0)
    return pl.pallas_call(
        _outproj_kernel,
        grid=(n // OUT_TM,),
        in_specs=[
            pl.BlockSpec((OUT_TM, D), row),
            pl.BlockSpec((OUT_TM, D), row),
            pl.BlockSpec((1, 1, D), bmap),
            pl.BlockSpec((1, 1, D), bmap),
            pl.BlockSpec((1, 1, D), bmap),
            pl.BlockSpec((1, D), const),
            pl.BlockSpec((NE, D), const),
            pl.BlockSpec((NE, 1), const),
            pl.BlockSpec(memory_space=pl.ANY),
        ],
        out_specs=[
            pl.BlockSpec((OUT_TM, D), row),
            pl.BlockSpec((1, 2, OUT_TM), table),
            pl.BlockSpec((1, SUBLANES, LANES), table),
            pl.BlockSpec((2, OUT_TM), lane),
            pl.BlockSpec((1, BLK_LANES), const),
            pl.BlockSpec((1, 1), const),
            pl.BlockSpec(memory_space=pl.ANY),
        ],
        out_shape=[
            jax.ShapeDtypeStruct((n, D), F32),
            jax.ShapeDtypeStruct((n // OUT_TM, 2, OUT_TM), jnp.int32),
            jax.ShapeDtypeStruct((n // OUT_TM, 8, LANES), jnp.int32),
            jax.ShapeDtypeStruct((2, n), F32),
            jax.ShapeDtypeStruct((1, BLK_LANES), jnp.int32),
            jax.ShapeDtypeStruct((1, 1), jnp.int32),
            jax.ShapeDtypeStruct((nb * EXP_BM * PK_SUB, LANES), jnp.uint32),
        ],
        scratch_shapes=[
            pltpu.VMEM((NE, 1), F32),
            pltpu.VMEM((NE, 1), F32),
            pltpu.VMEM((1, BLK_LANES), F32),
            pltpu.VMEM((1, 1), F32),
            pltpu.VMEM((D, D), BF16),
            pltpu.VMEM((2 * OUT_TM * PK_SUB, LANES), jnp.uint32),
            pltpu.VMEM((SUBLANES, LANES), jnp.int32),
            pltpu.SMEM((SUBLANES, LANES), jnp.int32),
            pltpu.VMEM((2, LANES), jnp.int32),
            pltpu.SMEM((2, LANES), jnp.int32),
            pltpu.VMEM((EXP_BM * PK_SUB, LANES), jnp.uint32),
            pltpu.SemaphoreType.DMA(()),
            pltpu.SemaphoreType.DMA(()),
            pltpu.SemaphoreType.DMA(()),
        ],
        compiler_params=pltpu.CompilerParams(
            dimension_semantics=("arbitrary",), vmem_limit_bytes=BIG_VMEM_LIMIT),
        name="out_proj_router",
    )(mix, x2, gate, shift, scale, nw, wr_t, rb_col, w_bf)


def _pack_rows(v, o_ref, rows):
    hi = lax.bitcast_convert_type(v[:, :D // 2].astype(BF16).astype(F32), jnp.uint32)
    lo = lax.bitcast_convert_type(v[:, D // 2:].astype(BF16).astype(F32), jnp.uint32)
    packed = hi | (lo >> 16)
    for j in range(PK_SUB):
        o_ref[pl.ds(j, rows, stride=PK_SUB), :] = packed[:, j * LANES:(j + 1) * LANES]


def _unpack_halves(x_ref, rows):
    his, los = [], []
    for j in range(PK_SUB):
        p = x_ref[pl.ds(j, rows, stride=PK_SUB), :]
        his.append(lax.bitcast_convert_type(p & jnp.uint32(0xFFFF0000), F32))
        los.append(lax.bitcast_convert_type(p << 16, F32))
    return his, los


def _tile_rows(row, nrows):
    start = row * PK_SUB
    if not isinstance(row, int):
        start = pl.multiple_of(start, PK_SUB)
    return pl.ds(start, nrows * PK_SUB)


def _tile_copy(src, src_row, dst, dst_row, nrows, sem):
    return pltpu.make_async_copy(src.at[_tile_rows(src_row, nrows), :], dst.at[_tile_rows(dst_row, nrows), :], sem)


def _if_rare(cond, body):
    def once(_, carry):
        body()
        return carry

    lax.fori_loop(0, cond.astype(jnp.int32), once, 0)


RUN_SIZES = [OUT_TM >> s for s in range(OUT_TM.bit_length())]
RUN_SIZES_SHORT = [s for s in RUN_SIZES if s < RUN_SHORT]
RUN_SIZES_LONG = [s for s in RUN_SIZES if s >= RUN_SHORT]


def _start_run(src, src_row, dst, dst_row, n, sem, sizes):
    for size in sizes:
        off = n & ~(2 * size - 1)

        @pl.when((n & size) != 0)
        def _(size=size, off=off):
            _tile_copy(src, src_row + off, dst, dst_row + off, size, sem).start()


def _start_tile_runs(table, read, group, src, dst, sem, to_slots):
    runs = []
    for u in range(RUN_UNROLL):
        e = group * RUN_UNROLL + u
        start, n_first, slot_first, n_rest, slot_rest = [read(table, r, e) for r in range(5)]
        first = (start, slot_first) if to_slots else (slot_first, start)
        rest = (start + n_first, slot_rest) if to_slots else (slot_rest, start + n_first)
        runs.append((first, n_first, rest, n_rest))
    for first, n_first, _, _ in runs:
        _start_run(src, first[0], dst, first[1], n_first, sem, RUN_SIZES_SHORT)
    rare = [(nf >= RUN_SHORT) | (nr > 0) for _, nf, _, nr in runs]

    def long_and_rest():
        for (first, n_first, rest, n_rest), cond in zip(runs, rare):
            def one(first=first, n_first=n_first, rest=rest, n_rest=n_rest):
                _start_run(src, first[0], dst, first[1], n_first, sem, RUN_SIZES_LONG)
                _start_run(src, rest[0], dst, rest[1], n_rest, sem, RUN_SIZES)

            _if_rare(cond, one)

    _if_rare(functools.reduce(jnp.logical_or, rare), long_and_rest)


def _pad_chunks(fn):
    size = EXP_BM // 2
    while size >= 1:
        fn(size)
        size //= 2


def _expert_kernel(order_ref, be_ref, next_ref, na_ref, x_ref, w1_hbm, w3_hbm, w2_hbm, o_ref,
                   s1, s3, s2, w1b, w3b, w2b, sem, *, layer):
    i = pl.program_id(0)
    active = i < na_ref[0]
    changed = jnp.logical_or(i == 0, be_ref[i] != be_ref[jnp.maximum(i - 1, 0)])

    def weight_copies(e):
        return [pltpu.make_async_copy(w_hbm.at[layer, e], stage, sem)
                for w_hbm, stage in ((w1_hbm, s1), (w3_hbm, s3), (w2_hbm, s2))]

    @pl.when(jnp.logical_and(active, changed))
    def _():
        @pl.when(i == 0)
        def _():
            for cp in weight_copies(be_ref[0]):
                cp.start()

        for cp in weight_copies(be_ref[i]):
            cp.wait()
        w1b[...] = s1[...].astype(BF16)
        w3b[...] = s3[...].astype(BF16)
        w2b[...] = s2[...].astype(BF16)
        nxt = next_ref[i]

        @pl.when(nxt >= 0)
        def _():
            for cp in weight_copies(nxt):
                cp.start()

    @pl.when(active)
    def _():
        rows = EXP_BM // EXP_SUB
        for s in range(EXP_SUB):
            part = pl.ds(s * rows * PK_SUB, rows * PK_SUB)
            his, los = _unpack_halves(x_ref.at[part], rows)
            xb = jnp.concatenate([c.astype(BF16) for c in his + los], axis=1)
            h1 = jnp.dot(xb, w1b[...], preferred_element_type=F32)
            h3 = jnp.dot(xb, w3b[...], preferred_element_type=F32)
            act = (h1 * _sigmoid(h1) * h3).astype(BF16)
            _pack_rows(jnp.dot(act, w2b[...], preferred_element_type=F32), o_ref.at[part], rows)

    @pl.when(jnp.logical_not(active))
    def _():
        o_ref[...] = jnp.zeros_like(o_ref)


def _expert_call(order, block_expert, next_expert, n_active, xg, w1, w3, w2, layer):
    nb = order.shape[0]
    live = lambda i, od, be, nx, na: (od[jnp.minimum(i, na[0] - 1)], 0)
    grid_spec = pltpu.PrefetchScalarGridSpec(
        num_scalar_prefetch=4,
        grid=(nb,),
        in_specs=[
            pl.BlockSpec((EXP_BM * PK_SUB, LANES), live),
            pl.BlockSpec(memory_space=pl.ANY),
            pl.BlockSpec(memory_space=pl.ANY),
            pl.BlockSpec(memory_space=pl.ANY),
        ],
        out_specs=pl.BlockSpec((EXP_BM * PK_SUB, LANES), lambda i, od, be, nx, na: (od[i], 0)),
        scratch_shapes=[pltpu.VMEM((D, FF), F32), pltpu.VMEM((D, FF), F32), pltpu.VMEM((FF, D), F32),
                        pltpu.VMEM((D, FF), BF16), pltpu.VMEM((D, FF), BF16), pltpu.VMEM((FF, D), BF16),
                        pltpu.SemaphoreType.DMA(())],
    )
    return pl.pallas_call(
        functools.partial(_expert_kernel, layer=layer),
        grid_spec=grid_spec,
        out_shape=jax.ShapeDtypeStruct((nb * EXP_BM * PK_SUB, LANES), jnp.uint32),
        compiler_params=pltpu.CompilerParams(
            dimension_semantics=("arbitrary",), vmem_limit_bytes=VMEM_LIMIT),
        name="expert_blocks",
    )(order, block_expert, next_expert, n_active, xg, w1, w3, w2)


def _combine_kernel(seg_ref, seg_next_ref, x_ref, gate_ref, gw_ref, pos_ref, fw_ref, y_hbm, o_ref, ybuf, sem, *,
                    final_norm):
    i = pl.program_id(0)
    last = pl.num_programs(0) - 1

    def fetch(table, parity):
        dst = ybuf.at[parity]

        def gather(group, carry):
            _start_tile_runs(table, lambda t, r, e: t[0, r, e], group, y_hbm, dst, sem.at[parity], to_slots=False)
            return carry

        lax.fori_loop(0, NE // RUN_UNROLL, gather, 0)

    par = i % 2

    @pl.when(i == 0)
    def _():
        fetch(seg_ref, 0)

    @pl.when(i < last)
    def _():
        fetch(seg_next_ref, 1 - par)

    cur = ybuf.at[par]
    _tile_copy(y_hbm, 0, cur, 0, 2 * CMB_TM, sem.at[par]).wait()

    his, los = _unpack_halves(cur, 2 * CMB_TM)
    ys = jnp.concatenate([c.astype(BF16) for c in his + los], axis=1)
    gw = gw_ref[...]
    pos = pos_ref[0]
    prow = lax.broadcasted_iota(jnp.int32, (2 * CMB_TM, CMB_TM), 0)
    selector = jnp.where(prow == pos[0:1], gw[0:1], jnp.where(prow == pos[1:2], gw[1:2], 0.0))
    moe = lax.dot_general(selector.astype(BF16), ys, (((0,), (0,)), ((), ())), preferred_element_type=F32)
    x2 = x_ref[...] + gate_ref[0] * moe
    if final_norm:
        ms = jnp.mean(x2 * x2, axis=-1, keepdims=True)
        x2 = x2 * lax.rsqrt(ms + NORM_EPS) * fw_ref[...]
    o_ref[...] = x2


def _combine_call(runs, x1, gate, gw, sorted_pos, fw, yg, seq, final_norm):
    n = x1.shape[0]
    tiles_per_batch = seq // CMB_TM
    nt = n // CMB_TM
    return pl.pallas_call(
        functools.partial(_combine_kernel, final_norm=final_norm),
        grid=(nt,),
        in_specs=[
            pl.BlockSpec((1, SUBLANES, LANES), lambda i: (i, 0, 0), memory_space=pltpu.SMEM),
            pl.BlockSpec((1, SUBLANES, LANES), lambda i: (jnp.minimum(i + 1, nt - 1), 0, 0), memory_space=pltpu.SMEM),
            pl.BlockSpec((CMB_TM, D), lambda i: (i, 0)),
            pl.BlockSpec((1, 1, D), lambda i: (i // tiles_per_batch, 0, 0)),
            pl.BlockSpec((2, CMB_TM), lambda i: (0, i)),
            pl.BlockSpec((1, 2, CMB_TM), lambda i: (i, 0, 0)),
            pl.BlockSpec((1, D), lambda i: (0, 0)),
            pl.BlockSpec(memory_space=pl.ANY),
        ],
        out_specs=pl.BlockSpec((CMB_TM, D), lambda i: (i, 0)),
        out_shape=jax.ShapeDtypeStruct((n, D), F32),
        scratch_shapes=[pltpu.VMEM((2, 2 * CMB_TM * PK_SUB, LANES), jnp.uint32), pltpu.SemaphoreType.DMA((2,))],
        compiler_params=pltpu.CompilerParams(
            dimension_semantics=("arbitrary",), vmem_limit_bytes=VMEM_LIMIT),
        name="moe_combine",
    )(runs, runs, x1, gate, gw, sorted_pos, fw, yg)


def _decay_tables():
    log_gamma = jnp.log1p(-jnp.exp2(-5.0 - jnp.arange(NH, dtype=F32)))
    pos = jnp.arange(RCH, dtype=F32)
    diff = pos[:, None] - pos[None, :]
    mask = jnp.where(diff >= 0, jnp.exp(log_gamma[:, None, None] * jnp.maximum(diff, 0.0)), 0.0)
    q_decay = jnp.exp(log_gamma[:, None] * (pos + 1.0))
    k_decay = jnp.exp(log_gamma[:, None] * (RCH - 1.0 - pos))
    chunk_decay = jnp.exp(log_gamma * RCH)
    return mask.astype(F32), q_decay.T, k_decay.T, chunk_decay


def kernel(x, c, positions, w_ada, b_ada, norm1_w, norm2_w, w_in, w_out, ret_norm_w, sg_w_s, sg_b_s,
           w_router, router_bias, w1, w3, w2, final_norm_w):
    bsz, seq, d = x.shape
    n = bsz * seq
    assert d == D and seq % INP_TM == 0 and seq % MIX_ROWS == 0 and n % CMB_TM == 0

    assert bsz <= SUBLANES
    c_cols = jnp.zeros((D, SUBLANES), F32).at[:, :bsz].set(c.T)
    mod = _ada_call(c_cols, w_ada, b_ada, bsz)[:, :bsz]
    mod = mod.reshape(N_LAYERS, bsz, N_MOD, 1, D)

    half = HD // 2
    inv_freq = (ROPE_THETA ** (-jnp.arange(half, dtype=F32) / half)).reshape(1, half)
    pos_col = positions.reshape(n, 1)
    decay_mask, q_decay_t, k_decay_t, chunk_decay = _decay_tables()

    perm = (jnp.arange(NE) % NEG) * EPG + jnp.arange(NE) // NEG
    wr_t = w_router.T[perm]
    rb_col = router_bias[perm].reshape(NE, 1)

    nb = n * 2 // EXP_BM + NE
    assert nb <= BLK_LANES and n % OUT_TM == 0
    xs = x.reshape(n, D)
    for l in range(N_LAYERS):
        shift1, scale1, gate1, shift2, scale2, gate2 = [mod[l, :, t] for t in range(N_MOD)]
        proj = _inproj_call(xs, shift1, scale1, norm1_w[l].reshape(1, D), w_in, l, pos_col, inv_freq, seq)
        mix = _mix_call(proj, chunk_decay, decay_mask, q_decay_t, k_decay_t, ret_norm_w[l].reshape(1, RW),
                        sg_w_s[l], sg_b_s[l].T, bsz, seq)
        x1, sorted_pos, runs, gw, owner, n_active, xg = _outproj_call(
            mix, xs, w_out[l].astype(BF16), gate1, shift2, scale2, norm2_w[l].reshape(1, D), wr_t, rb_col, seq, nb)

        owner = owner.reshape(BLK_LANES)
        ids = jnp.arange(BLK_LANES, dtype=jnp.int32)
        ahead = (owner[None, :] < owner[:, None]) | ((owner[None, :] == owner[:, None]) & (ids[None, :] < ids[:, None]))
        pos = jnp.sum(ahead, axis=1)
        at = pos[None, :] == ids[:, None]
        order = jnp.sum(jnp.where(at, ids[None, :], 0), axis=1)[:nb].astype(jnp.int32)
        owner_sorted = jnp.sum(jnp.where(at, owner[None, :], 0), axis=1)[:nb]
        block_expert = jnp.minimum(owner_sorted, NE - 1).astype(jnp.int32)
        later = jnp.min(jnp.where(owner_sorted[None, :] > owner_sorted[:, None], owner_sorted[None, :], NE), axis=1)
        next_expert = jnp.where(later >= NE, -1, later).astype(jnp.int32)

        yg = _expert_call(order, block_expert, next_expert, n_active.reshape(1), xg, w1, w3, w2, l)
        xs = _combine_call(runs, x1, gate2, gw, sorted_pos, final_norm_w.reshape(1, D), yg, seq,
                           final_norm=(l == N_LAYERS - 1))
    return xs.reshape(bsz, seq, D)
```

```python
import functools

import jax
import jax.numpy as jnp
from jax import lax
from jax.experimental import pallas as pl
from jax.experimental.pallas import tpu as pltpu

F32 = jnp.float32
BF16 = jnp.bfloat16

D = 2048
N_LAYERS = 2
RW = D // 2
NH = 4
HD = RW // NH
CH = 128
RCH = 128
ROPE_THETA = 10000.0
SGW = D - RW
NG = 8
GD = SGW // NG
PROJ = 4 * RW + 2 * SGW
NE = 32
NEG = 8
EPG = NE // NEG
FF = 512
N_MOD = 6
NORM_EPS = 1e-6

VMEM_LIMIT = 48 * 1024 * 1024
BIG_VMEM_LIMIT = 56 * 1024 * 1024

ADA_TN = 1024
INP_TM = 256
INP_TN = 512
MIX_ROWS = 1024
OUT_TM = 512
RUN_UNROLL = 4
RUN_SHORT = 64
EXP_BM = 512
EXP_SUB = 2
CMB_TM = OUT_TM
BLK_LANES = 256

LANES = 128
SUBLANES = 8
PK_SUB = D // 2 // LANES
COUNT_DIGIT = 16


def _sigmoid(v):
    return 1.0 / (1.0 + jnp.exp(-v))


def _gelu_tanh(v):
    return 0.5 * v * (1.0 + jnp.tanh(0.7978845608028654 * (v + 0.044715 * (v * v * v))))


def _ada_kernel(c_ref, w_ref, b_ref, o_ref, *, bsz):
    c = c_ref[...]
    ca = c * _sigmoid(c)
    w = w_ref[0]
    rows = [jnp.sum(w * ca[:, b:b + 1], axis=0, keepdims=True) for b in range(bsz)]
    rows.append(jnp.zeros((SUBLANES - bsz, ADA_TN), F32))
    o_ref[0] = jnp.concatenate(rows, axis=0) + b_ref[0]


def _ada_call(c_cols, w_ada, b_ada, bsz):
    depth, _, ncol = w_ada.shape
    return pl.pallas_call(
        functools.partial(_ada_kernel, bsz=bsz),
        grid=(depth, ncol // ADA_TN),
        in_specs=[
            pl.BlockSpec((D, SUBLANES), lambda l, j: (0, 0)),
            pl.BlockSpec((1, D, ADA_TN), lambda l, j: (l, 0, j)),
            pl.BlockSpec((1, 1, ADA_TN), lambda l, j: (l, 0, j)),
        ],
        out_specs=pl.BlockSpec((1, 8, ADA_TN), lambda l, j: (l, 0, j)),
        out_shape=jax.ShapeDtypeStruct((depth, 8, ncol), F32),
        compiler_params=pltpu.CompilerParams(vmem_limit_bytes=VMEM_LIMIT),
        name="ada_mod",
    )(c_cols, w_ada, b_ada.reshape(depth, 1, ncol))


def _inproj_kernel(x_ref, shift_ref, scale_ref, nw_ref, pos_ref, freq_ref, w_hbm, o_ref, w_vmem, stage, sem, *,
                   layer):
    @pl.when(pl.program_id(0) == 0)
    def _():
        def panel(j):
            cols = pl.ds(j * INP_TN, INP_TN)
            return pltpu.make_async_copy(w_hbm.at[layer, :, cols], stage.at[j % 2], sem.at[j % 2])

        npanel = PROJ // INP_TN
        panel(0).start()
        for j in range(npanel):
            if j + 1 < npanel:
                panel(j + 1).start()
            panel(j).wait()
            w_vmem[:, j * INP_TN:(j + 1) * INP_TN] = stage[j % 2].astype(BF16)

    x = x_ref[...]
    ms = jnp.mean(x * x, axis=-1, keepdims=True)
    h = x * lax.rsqrt(ms + NORM_EPS) * nw_ref[...]
    h = (h * (1.0 + scale_ref[0]) + shift_ref[0]).astype(BF16)
    ang = pos_ref[...].astype(F32) * freq_ref[...]
    cos = jnp.cos(ang)
    sin = jnp.sin(ang)
    half = HD // 2
    n_rot = 2 * RW // INP_TN
    order = list(range(n_rot, PROJ // INP_TN)) + list(range(n_rot))

    for j in order:
        c0 = j * INP_TN
        acc = jnp.dot(h, w_vmem[:, c0:c0 + INP_TN], preferred_element_type=F32)
        sec = c0 // RW
        s0 = c0 % RW
        if sec in (0, 1):
            scale = 1.0 if sec == 0 else HD ** -0.5
            for hh in range(INP_TN // HD):
                a = acc[:, hh * HD:hh * HD + half]
                b = acc[:, hh * HD + half:(hh + 1) * HD]
                o_ref[sec, :, s0 + hh * HD:s0 + hh * HD + half] = ((a * cos - b * sin) * scale).astype(BF16)
                o_ref[sec, :, s0 + hh * HD + half:s0 + (hh + 1) * HD] = ((b * cos + a * sin) * scale).astype(BF16)
        elif sec == 2:
            o_ref[sec, :, s0:s0 + INP_TN] = acc.astype(BF16)
        elif sec == 3:
            o_ref[sec, :, s0:s0 + INP_TN] = (acc * _sigmoid(acc)).astype(BF16)
        elif sec == 4:
            o_ref[sec, :, s0:s0 + INP_TN] = _gelu_tanh(acc).astype(BF16)
        else:
            for gg in range(INP_TN // GD):
                t = _gelu_tanh(acc[:, gg * GD:(gg + 1) * GD])
                ms_g = jnp.mean(t * t, axis=-1, keepdims=True)
                o_ref[sec, :, s0 + gg * GD:s0 + (gg + 1) * GD] = (t * lax.rsqrt(ms_g + NORM_EPS)).astype(BF16)


def _inproj_call(x2, shift, scale, nw, w_in, layer, pos_col, inv_freq, seq):
    n = x2.shape[0]
    tiles_per_batch = seq // INP_TM
    bmap = lambda i: (i // tiles_per_batch, 0, 0)
    return pl.pallas_call(
        functools.partial(_inproj_kernel, layer=layer),
        grid=(n // INP_TM,),
        in_specs=[
            pl.BlockSpec((INP_TM, D), lambda i: (i, 0)),
            pl.BlockSpec((1, 1, D), bmap),
            pl.BlockSpec((1, 1, D), bmap),
            pl.BlockSpec((1, D), lambda i: (0, 0)),
            pl.BlockSpec((INP_TM, 1), lambda i: (i, 0)),
            pl.BlockSpec((1, HD // 2), lambda i: (0, 0)),
            pl.BlockSpec(memory_space=pl.ANY),
        ],
        out_specs=pl.BlockSpec((PROJ // RW, INP_TM, RW), lambda i: (0, i, 0)),
        out_shape=jax.ShapeDtypeStruct((PROJ // RW, n, RW), BF16),
        scratch_shapes=[pltpu.VMEM((D, PROJ), BF16), pltpu.VMEM((2, D, INP_TN), F32),
                        pltpu.SemaphoreType.DMA((2,))],
        compiler_params=pltpu.CompilerParams(
            dimension_semantics=("arbitrary",), vmem_limit_bytes=BIG_VMEM_LIMIT),
        name="in_proj",
    )(x2, shift, scale, nw, pos_col, inv_freq, w_in)


def _mix_kernel(cd_ref, q_ref, k_ref, v_ref, g_ref, u_ref, vs_ref, dm_ref, qd_ref, kd_ref,
                rnw_ref, ws_ref, bs_ref, o_ref, state_ref):
    @pl.when(pl.program_id(1) == 0)
    def _():
        state_ref[...] = jnp.zeros_like(state_ref)

    row = lax.broadcasted_iota(jnp.int32, (CH, CH), 0)
    col = lax.broadcasted_iota(jnp.int32, (CH, CH), 1)
    causal = row >= col

    chunks = [slice(c * CH, (c + 1) * CH) for c in range(MIX_ROWS // CH)]

    for g in range(NG):
        cols = slice(g * GD, (g + 1) * GD)
        wm = jnp.where(causal, ws_ref[g], 0.0).astype(BF16)
        vs_wide = jnp.concatenate([vs_ref[rows, cols] for rows in chunks], axis=1)
        z = jnp.dot(wm, vs_wide, preferred_element_type=F32) + bs_ref[:, g:g + 1]
        for c, rows in enumerate(chunks):
            zc = z[:, c * GD:(c + 1) * GD]
            o_ref[rows, RW + g * GD:RW + (g + 1) * GD] = (u_ref[rows, cols].astype(F32) * zc).astype(BF16)

    for h in range(NH):
        cols = slice(h * HD, (h + 1) * HD)
        st = state_ref[h]
        for rows in [slice(c * RCH, (c + 1) * RCH) for c in range(MIX_ROWS // RCH)]:
            q = q_ref[rows, cols]
            k = k_ref[rows, cols]
            v = v_ref[rows, cols]
            s = lax.dot_general(q, k, (((1,), (1,)), ((), ())), preferred_element_type=F32)
            s = s * dm_ref[h]
            intra = jnp.dot(s.astype(BF16), v, preferred_element_type=F32)
            cross = jnp.dot(q, st.astype(BF16), preferred_element_type=F32) * qd_ref[:, h:h + 1]
            kdec = (k.astype(F32) * kd_ref[:, h:h + 1]).astype(BF16)
            upd = lax.dot_general(kdec, v, (((0,), (0,)), ((), ())), preferred_element_type=F32)
            st = st * cd_ref[h] + upd
            o = intra + cross
            ms = jnp.mean(o * o, axis=-1, keepdims=True)
            o = o * lax.rsqrt(ms + NORM_EPS) * rnw_ref[:, cols]
            o_ref[rows, cols] = (o * g_ref[rows, cols].astype(F32)).astype(BF16)
        state_ref[h] = st


def _mix_call(proj, chunk_decay, decay_mask, q_decay_t, k_decay_t, rnw, ws, bs_t, bsz, seq):
    n = proj.shape[1]
    steps = seq // MIX_ROWS

    def colblk(jb):
        return pl.BlockSpec((None, MIX_ROWS, RW), lambda b, s, cd: (jb, b * steps + s, 0))

    const2 = lambda b, s, cd: (0, 0)
    const3 = lambda b, s, cd: (0, 0, 0)
    grid_spec = pltpu.PrefetchScalarGridSpec(
        num_scalar_prefetch=1,
        grid=(bsz, steps),
        in_specs=[colblk(0), colblk(1), colblk(2), colblk(3), colblk(4), colblk(5),
                  pl.BlockSpec((NH, RCH, RCH), const3),
                  pl.BlockSpec((RCH, NH), const2),
                  pl.BlockSpec((RCH, NH), const2),
                  pl.BlockSpec((1, RW), const2),
                  pl.BlockSpec((NG, CH, CH), const3),
                  pl.BlockSpec((CH, NG), const2)],
        out_specs=pl.BlockSpec((MIX_ROWS, D), lambda b, s, cd: (b * steps + s, 0)),
        scratch_shapes=[pltpu.VMEM((NH, HD, HD), F32)],
    )
    return pl.pallas_call(
        _mix_kernel,
        grid_spec=grid_spec,
        out_shape=jax.ShapeDtypeStruct((n, D), BF16),
        compiler_params=pltpu.CompilerParams(
            dimension_semantics=("arbitrary", "arbitrary"), vmem_limit_bytes=VMEM_LIMIT),
        name="retention_spatial_mix",
    )(chunk_decay, proj, proj, proj, proj, proj, proj, decay_mask, q_decay_t, k_decay_t, rnw, ws, bs_t)


def _outproj_kernel(mix_ref, x_ref, gate_ref, shift_ref, scale_ref, nw_ref, wr_ref, rb_ref, w_hbm,
                    x1_ref, pos_ref, seg_ref, gw_ref, be_ref, na_ref, xg_hbm,
                    carry_ref, cur_ref, bev_ref, nfree_ref, w_vmem, hbuf, seg_v, seg_s, tail_v, tail_s, zbuf,
                    sem, ssem, zsem):
    i = pl.program_id(0)
    last = pl.num_programs(0) - 1
    nblk = xg_hbm.shape[0] // (EXP_BM * PK_SUB)

    @pl.when(i == 0)
    def _():
        cp = pltpu.make_async_copy(w_hbm, w_vmem, ssem)
        cp.start()
        carry_ref[...] = jnp.zeros_like(carry_ref)
        cur_ref[...] = jnp.zeros_like(cur_ref)
        nfree_ref[...] = jnp.zeros_like(nfree_ref)
        bev_ref[...] = jnp.full(bev_ref.shape, NE, F32)
        cp.wait()

    y = jnp.dot(mix_ref[...], w_vmem[...], preferred_element_type=F32)
    x1 = x_ref[...] + gate_ref[0] * y
    x1_ref[...] = x1
    ms = jnp.mean(x1 * x1, axis=-1, keepdims=True)
    h2 = x1 * lax.rsqrt(ms + NORM_EPS) * nw_ref[...]
    h2 = h2 * (1.0 + scale_ref[0]) + shift_ref[0]

    def scatter_wait():
        _tile_copy(hbuf, 0, xg_hbm, 0, 2 * OUT_TM, sem).wait()

    nt = (((1,), (1,)), ((), ()))
    wr = wr_ref[...]
    w_hi = wr.astype(BF16)
    w_lo = (wr - w_hi.astype(F32)).astype(BF16)
    h_hi = h2.astype(BF16)
    h_lo = (h2 - h_hi.astype(F32)).astype(BF16)
    p_hi = lax.dot_general(jnp.concatenate([w_hi, w_lo], axis=0), h_hi, nt, preferred_element_type=F32)
    p_lo = lax.dot_general(w_hi, h_lo, nt, preferred_element_type=F32)
    logits = p_hi[:NE] + p_hi[NE:] + p_lo
    scores = _sigmoid(logits)
    biased = scores + rb_ref[...]
    a = [biased[m * NEG:(m + 1) * NEG] for m in range(EPG)]
    sc = [scores[m * NEG:(m + 1) * NEG] for m in range(EPG)]
    p, q = jnp.maximum(a[0], a[1]), jnp.minimum(a[0], a[1])
    r, s = jnp.maximum(a[2], a[3]), jnp.minimum(a[2], a[3])
    group_score = jnp.maximum(p, r) + jnp.maximum(jnp.minimum(p, r), jnp.maximum(q, s))
    gidx = lax.broadcasted_iota(jnp.int32, group_score.shape, 0)
    gmax = jnp.max(group_score, axis=0, keepdims=True)
    g_sel = jnp.min(jnp.where(group_score == gmax, gidx, NEG), axis=0, keepdims=True)
    pick = gidx == g_sel
    v = [jnp.sum(jnp.where(pick, a[m], 0.0), axis=0, keepdims=True) for m in range(EPG)]
    u = [jnp.sum(jnp.where(pick, sc[m], 0.0), axis=0, keepdims=True) for m in range(EPG)]

    def first_argmax(vals):
        m = functools.reduce(jnp.maximum, vals)
        idx = jnp.full(m.shape, EPG - 1, jnp.int32)
        for t in range(EPG - 2, -1, -1):
            idx = jnp.where(vals[t] == m, t, idx)
        return idx

    def select(vals, idx):
        out = vals[EPG - 1]
        for t in range(EPG - 2, -1, -1):
            out = jnp.where(idx == t, vals[t], out)
        return out

    i1 = first_argmax(v)
    v2 = [jnp.where(i1 == t, -jnp.inf, v[t]) for t in range(EPG)]
    i2 = first_argmax(v2)
    s1 = select(u, i1)
    s2 = select(u, i2)
    denom = s1 + s2
    e1 = g_sel * EPG + i1
    e2 = g_sel * EPG + i2
    gw_ref[0:1, :] = s1 / denom
    gw_ref[1:2, :] = s2 / denom

    tm = e1.shape[1]
    eidx = lax.broadcasted_iota(jnp.int32, (NE, tm), 0)
    oh1 = eidx == e1
    oh2 = eidx == e2
    oh = jnp.where(oh1 | oh2, 1.0, 0.0)
    tr = lax.broadcasted_iota(jnp.int32, (tm, tm), 0)
    tc = lax.broadcasted_iota(jnp.int32, (tm, tm), 1)
    earlier = jnp.where(tr < tc, 1.0, 0.0).astype(BF16)
    carry = carry_ref[...]
    local = jnp.dot(oh.astype(BF16), earlier, preferred_element_type=F32)
    rank = local + carry
    count = jnp.sum(oh, axis=1, keepdims=True)
    total = carry + count
    carry_ref[...] = total

    inv_bm = 1.0 / EXP_BM
    blocks_before = jnp.floor((carry + (EXP_BM - 1)) * inv_bm)
    need = jnp.floor((total + (EXP_BM - 1)) * inv_bm) - blocks_before
    er = lax.broadcasted_iota(jnp.int32, (NE, NE), 0)
    ec = lax.broadcasted_iota(jnp.int32, (NE, NE), 1)
    lower = jnp.where(ec < er, 1.0, 0.0).astype(BF16)
    count_hi = jnp.floor(count * (1.0 / COUNT_DIGIT))
    lane_e = lax.broadcasted_iota(jnp.int32, (NE, LANES), 1)
    digits = jnp.where(lane_e == 0, need, jnp.where(lane_e == 1, count_hi, count - COUNT_DIGIT * count_hi))
    sums = jnp.dot(lower, digits.astype(BF16), preferred_element_type=F32)
    nfree = nfree_ref[...]
    base = nfree + sums[:, 0:1]
    start = COUNT_DIGIT * sums[:, 1:2] + sums[:, 2:3]
    cur = cur_ref[...]
    cur_ref[...] = jnp.where(need > 0, base + need - 1.0, cur)
    nfree_new = nfree + jnp.sum(need, axis=0, keepdims=True)
    nfree_ref[...] = nfree_new
    blk = lax.broadcasted_iota(jnp.int32, (NE, bev_ref.shape[1]), 1).astype(F32)
    hit = jnp.logical_and(blk >= base, blk < base + need)
    owner = jnp.sum(jnp.where(hit, eidx[:, 0:1].astype(F32), 0.0), axis=0, keepdims=True)
    taken = jnp.sum(jnp.where(hit, 1.0, 0.0), axis=0, keepdims=True)
    bev = jnp.where(taken > 0, owner, bev_ref[...])
    bev_ref[...] = bev
    be_ref[...] = bev.astype(jnp.int32)
    na_ref[...] = nfree_new.astype(jnp.int32)

    sorted_all = start + local
    pos1 = jnp.sum(jnp.where(oh1, sorted_all, 0.0), axis=0, keepdims=True).astype(jnp.int32)
    pos2 = jnp.sum(jnp.where(oh2, sorted_all, 0.0), axis=0, keepdims=True).astype(jnp.int32)
    pos_ref[0, 0:1, :] = pos1
    pos_ref[0, 1:2, :] = pos2

    as_row = lambda col: jnp.sum(jnp.where(lane_e == eidx[:, 0:1], col, 0.0), axis=0, keepdims=True)
    j0 = jnp.floor(carry * inv_bm)
    in_block = carry - j0 * EXP_BM
    n_a = jnp.minimum(count, EXP_BM - in_block)
    block_of = lambda j: jnp.where(j < blocks_before, cur, base + (j - blocks_before))
    seg_rows = [start, n_a, block_of(j0) * EXP_BM + in_block, count - n_a, block_of(j0 + 1.0) * EXP_BM]
    seg_v[...] = jnp.concatenate(
        [as_row(col).astype(jnp.int32) for col in seg_rows]
        + [jnp.zeros((SUBLANES - len(seg_rows), LANES), jnp.int32)], axis=0)
    seg_ref[0] = seg_v[...]
    to_smem = pltpu.make_async_copy(seg_v, seg_s, ssem)
    to_smem.start()

    @pl.when(i >= 1)
    def _():
        scatter_wait()

    for part in range(2):
        prow = lax.broadcasted_iota(jnp.int32, (tm, tm), 0) + part * tm
        perm = jnp.where((prow == pos1) | (prow == pos2), 1.0, 0.0).astype(BF16)
        sorted_rows = jnp.dot(perm, h_hi, preferred_element_type=F32)
        _pack_rows(sorted_rows, hbuf.at[pl.ds(part * tm * PK_SUB, tm * PK_SUB)], tm)

    to_smem.wait()

    def scatter(group, carry_):
        _start_tile_runs(seg_s, lambda t, r, e: t[r, e], group, hbuf, xg_hbm, sem, to_slots=True)
        return carry_

    lax.fori_loop(0, NE // RUN_UNROLL, scatter, 0)

    @pl.when(i == last)
    def _():
        scatter_wait()
        lane1 = lax.broadcasted_iota(jnp.int32, (1, LANES), 1)
        cur_row = jnp.where(lane1 == NE, nfree_new, as_row(cur_ref[...]))
        tail_v[0:1, :] = cur_row.astype(jnp.int32)
        tail_v[1:2, :] = as_row(total).astype(jnp.int32)
        cp = pltpu.make_async_copy(tail_v, tail_s, ssem)
        cp.start()
        cp.wait()
        zbuf[...] = jnp.zeros_like(zbuf)

        def fills(wait):
            def per_expert(e, carry_):
                used = tail_s[1, e] & (EXP_BM - 1)
                first = tail_s[0, e] * EXP_BM + used
                npad = jnp.where(used > 0, EXP_BM - used, 0)

                def chunk(size):
                    off = npad & ~(2 * size - 1)

                    @pl.when((npad & size) != 0)
                    def _():
                        cpz = _tile_copy(zbuf, 0, xg_hbm, first + off, size, zsem)
                        cpz.wait() if wait else cpz.start()

                _pad_chunks(chunk)
                return carry_

            lax.fori_loop(0, NE, per_expert, 0)

            def per_block(b, carry_):
                cpz = _tile_copy(zbuf, 0, xg_hbm, b * EXP_BM, EXP_BM, zsem)
                cpz.wait() if wait else cpz.start()
                return carry_

            lax.fori_loop(tail_s[0, NE], nblk, per_block, 0)

        fills(False)
        fills(True)


def _outproj_call(mix, x2, w_bf, gate, shift, scale, nw, wr_t, rb_col, seq, nb):
    n = x2.shape[0]
    tiles_per_batch = seq // OUT_TM
    bmap = lambda i: (i // tiles_per_batch, 0, 0)
    row = lambda i: (i, 0)
    lane = lambda i: (0, i)
    const = lambda i: (0, 0)
    table = lambda i: (i, 0, 0)
    return pl.pallas_call(
        _outproj_kernel,
        grid=(n // OUT_TM,),
        in_specs=[
            pl.BlockSpec((OUT_TM, D), row),
            pl.BlockSpec((OUT_TM, D), row),
            pl.BlockSpec((1, 1, D), bmap),
            pl.BlockSpec((1, 1, D), bmap),
            pl.BlockSpec((1, 1, D), bmap),
            pl.BlockSpec((1, D), const),
            pl.BlockSpec((NE, D), const),
            pl.BlockSpec((NE, 1), const),
            pl.BlockSpec(memory_space=pl.ANY),
        ],
        out_specs=[
            pl.BlockSpec((OUT_TM, D), row),
            pl.BlockSpec((1, 2, OUT_TM), table),
            pl.BlockSpec((1, SUBLANES, LANES), table),
            pl.BlockSpec((2, OUT_TM), lane),
            pl.BlockSpec((1, BLK_LANES), const),
            pl.BlockSpec((1, 1), const),
            pl.BlockSpec(memory_space=pl.ANY),
        ],
        out_shape=[
            jax.ShapeDtypeStruct((n, D), F32),
            jax.ShapeDtypeStruct((n // OUT_TM, 2, OUT_TM), jnp.int32),
            jax.ShapeDtypeStruct((n // OUT_TM, 8, LANES), jnp.int32),
            jax.ShapeDtypeStruct((2, n), F32),
            jax.ShapeDtypeStruct((1, BLK_LANES), jnp.int32),
            jax.ShapeDtypeStruct((1, 1), jnp.int32),
            jax.ShapeDtypeStruct((nb * EXP_BM * PK_SUB, LANES), jnp.uint32),
        ],
        scratch_shapes=[
            pltpu.VMEM((NE, 1), F32),
            pltpu.VMEM((NE, 1), F32),
            pltpu.VMEM((1, BLK_LANES), F32),
            pltpu.VMEM((1, 1), F32),
            pltpu.VMEM((D, D), BF16),
            pltpu.VMEM((2 * OUT_TM * PK_SUB, LANES), jnp.uint32),
            pltpu.VMEM((SUBLANES, LANES), jnp.int32),
            pltpu.SMEM((SUBLANES, LANES), jnp.int32),
            pltpu.VMEM((2, LANES), jnp.int32),
            pltpu.SMEM((2, LANES), jnp.int32),
            pltpu.VMEM((EXP_BM * PK_SUB, LANES), jnp.uint32),
            pltpu.SemaphoreType.DMA(()),
            pltpu.SemaphoreType.DMA(()),
            pltpu.SemaphoreType.DMA(()),
        ],
        compiler_params=pltpu.CompilerParams(
            dimension_semantics=("arbitrary",), vmem_limit_bytes=BIG_VMEM_LIMIT),
        name="out_proj_router",
    )(mix, x2, gate, shift, scale, nw, wr_t, rb_col, w_bf)


def _pack_rows(v, o_ref, rows):
    hi = lax.bitcast_convert_type(v[:, :D // 2].astype(BF16).astype(F32), jnp.uint32)
    lo = lax.bitcast_convert_type(v[:, D // 2:].astype(BF16).astype(F32), jnp.uint32)
    packed = hi | (lo >> 16)
    for j in range(PK_SUB):
        o_ref[pl.ds(j, rows, stride=PK_SUB), :] = packed[:, j * LANES:(j + 1) * LANES]


def _unpack_halves(x_ref, rows):
    his, los = [], []
    for j in range(PK_SUB):
        p = x_ref[pl.ds(j, rows, stride=PK_SUB), :]
        his.append(lax.bitcast_convert_type(p & jnp.uint32(0xFFFF0000), F32))
        los.append(lax.bitcast_convert_type(p << 16, F32))
    return his, los


def _tile_rows(row, nrows):
    start = row * PK_SUB
    if not isinstance(row, int):
        start = pl.multiple_of(start, PK_SUB)
    return pl.ds(start, nrows * PK_SUB)


def _tile_copy(src, src_row, dst, dst_row, nrows, sem):
    return pltpu.make_async_copy(src.at[_tile_rows(src_row, nrows), :], dst.at[_tile_rows(dst_row, nrows), :], sem)


def _if_rare(cond, body):
    def once(_, carry):
        body()
        return carry

    lax.fori_loop(0, cond.astype(jnp.int32), once, 0)


RUN_SIZES = [OUT_TM >> s for s in range(OUT_TM.bit_length())]
RUN_SIZES_SHORT = [s for s in RUN_SIZES if s < RUN_SHORT]
RUN_SIZES_LONG = [s for s in RUN_SIZES if s >= RUN_SHORT]


def _start_run(src, src_row, dst, dst_row, n, sem, sizes):
    for size in sizes:
        off = n & ~(2 * size - 1)

        @pl.when((n & size) != 0)
        def _(size=size, off=off):
            _tile_copy(src, src_row + off, dst, dst_row + off, size, sem).start()


def _start_tile_runs(table, read, group, src, dst, sem, to_slots):
    runs = []
    for u in range(RUN_UNROLL):
        e = group * RUN_UNROLL + u
        start, n_first, slot_first, n_rest, slot_rest = [read(table, r, e) for r in range(5)]
        first = (start, slot_first) if to_slots else (slot_first, start)
        rest = (start + n_first, slot_rest) if to_slots else (slot_rest, start + n_first)
        runs.append((first, n_first, rest, n_rest))
    for first, n_first, _, _ in runs:
        _start_run(src, first[0], dst, first[1], n_first, sem, RUN_SIZES_SHORT)
    rare = [(nf >= RUN_SHORT) | (nr > 0) for _, nf, _, nr in runs]

    def long_and_rest():
        for (first, n_first, rest, n_rest), cond in zip(runs, rare):
            def one(first=first, n_first=n_first, rest=rest, n_rest=n_rest):
                _start_run(src, first[0], dst, first[1], n_first, sem, RUN_SIZES_LONG)
                _start_run(src, rest[0], dst, rest[1], n_rest, sem, RUN_SIZES)

            _if_rare(cond, one)

    _if_rare(functools.reduce(jnp.logical_or, rare), long_and_rest)


def _pad_chunks(fn):
    size = EXP_BM // 2
    while size >= 1:
        fn(size)
        size //= 2


def _expert_kernel(order_ref, be_ref, next_ref, na_ref, x_ref, w1_hbm, w3_hbm, w2_hbm, o_ref,
                   s1, s3, s2, w1b, w3b, w2b, sem, *, layer):
    i = pl.program_id(0)
    active = i < na_ref[0]
    changed = jnp.logical_or(i == 0, be_ref[i] != be_ref[jnp.maximum(i - 1, 0)])

    def weight_copies(e):
        return [pltpu.make_async_copy(w_hbm.at[layer, e], stage, sem)
                for w_hbm, stage in ((w1_hbm, s1), (w3_hbm, s3), (w2_hbm, s2))]

    @pl.when(jnp.logical_and(active, changed))
    def _():
        @pl.when(i == 0)
        def _():
            for cp in weight_copies(be_ref[0]):
                cp.start()

        for cp in weight_copies(be_ref[i]):
            cp.wait()
        w1b[...] = s1[...].astype(BF16)
        w3b[...] = s3[...].astype(BF16)
        w2b[...] = s2[...].astype(BF16)
        nxt = next_ref[i]

        @pl.when(nxt >= 0)
        def _():
            for cp in weight_copies(nxt):
                cp.start()

    @pl.when(active)
    def _():
        rows = EXP_BM // EXP_SUB
        for s in range(EXP_SUB):
            part = pl.ds(s * rows * PK_SUB, rows * PK_SUB)
            his, los = _unpack_halves(x_ref.at[part], rows)
            xb = jnp.concatenate([c.astype(BF16) for c in his + los], axis=1)
            h1 = jnp.dot(xb, w1b[...], preferred_element_type=F32)
            h3 = jnp.dot(xb, w3b[...], preferred_element_type=F32)
            act = (h1 * _sigmoid(h1) * h3).astype(BF16)
            _pack_rows(jnp.dot(act, w2b[...], preferred_element_type=F32), o_ref.at[part], rows)

    @pl.when(jnp.logical_not(active))
    def _():
        o_ref[...] = jnp.zeros_like(o_ref)


def _expert_call(order, block_expert, next_expert, n_active, xg, w1, w3, w2, layer):
    nb = order.shape[0]
    live = lambda i, od, be, nx, na: (od[jnp.minimum(i, na[0] - 1)], 0)
    grid_spec = pltpu.PrefetchScalarGridSpec(
        num_scalar_prefetch=4,
        grid=(nb,),
        in_specs=[
            pl.BlockSpec((EXP_BM * PK_SUB, LANES), live),
            pl.BlockSpec(memory_space=pl.ANY),
            pl.BlockSpec(memory_space=pl.ANY),
            pl.BlockSpec(memory_space=pl.ANY),
        ],
        out_specs=pl.BlockSpec((EXP_BM * PK_SUB, LANES), lambda i, od, be, nx, na: (od[i], 0)),
        scratch_shapes=[pltpu.VMEM((D, FF), F32), pltpu.VMEM((D, FF), F32), pltpu.VMEM((FF, D), F32),
                        pltpu.VMEM((D, FF), BF16), pltpu.VMEM((D, FF), BF16), pltpu.VMEM((FF, D), BF16),
                        pltpu.SemaphoreType.DMA(())],
    )
    return pl.pallas_call(
        functools.partial(_expert_kernel, layer=layer),
        grid_spec=grid_spec,
        out_shape=jax.ShapeDtypeStruct((nb * EXP_BM * PK_SUB, LANES), jnp.uint32),
        compiler_params=pltpu.CompilerParams(
            dimension_semantics=("arbitrary",), vmem_limit_bytes=VMEM_LIMIT),
        name="expert_blocks",
    )(order, block_expert, next_expert, n_active, xg, w1, w3, w2)


def _combine_kernel(seg_ref, seg_next_ref, x_ref, gate_ref, gw_ref, pos_ref, fw_ref, y_hbm, o_ref, ybuf, sem, *,
                    final_norm):
    i = pl.program_id(0)
    last = pl.num_programs(0) - 1

    def fetch(table, parity):
        dst = ybuf.at[parity]

        def gather(group, carry):
            _start_tile_runs(table, lambda t, r, e: t[0, r, e], group, y_hbm, dst, sem.at[parity], to_slots=False)
            return carry

        lax.fori_loop(0, NE // RUN_UNROLL, gather, 0)

    par = i % 2

    @pl.when(i == 0)
    def _():
        fetch(seg_ref, 0)

    @pl.when(i < last)
    def _():
        fetch(seg_next_ref, 1 - par)

    cur = ybuf.at[par]
    _tile_copy(y_hbm, 0, cur, 0, 2 * CMB_TM, sem.at[par]).wait()

    his, los = _unpack_halves(cur, 2 * CMB_TM)
    ys = jnp.concatenate([c.astype(BF16) for c in his + los], axis=1)
    gw = gw_ref[...]
    pos = pos_ref[0]
    prow = lax.broadcasted_iota(jnp.int32, (2 * CMB_TM, CMB_TM), 0)
    selector = jnp.where(prow == pos[0:1], gw[0:1], jnp.where(prow == pos[1:2], gw[1:2], 0.0))
    moe = lax.dot_general(selector.astype(BF16), ys, (((0,), (0,)), ((), ())), preferred_element_type=F32)
    x2 = x_ref[...] + gate_ref[0] * moe
    if final_norm:
        ms = jnp.mean(x2 * x2, axis=-1, keepdims=True)
        x2 = x2 * lax.rsqrt(ms + NORM_EPS) * fw_ref[...]
    o_ref[...] = x2


def _combine_call(runs, x1, gate, gw, sorted_pos, fw, yg, seq, final_norm):
    n = x1.shape[0]
    tiles_per_batch = seq // CMB_TM
    nt = n // CMB_TM
    return pl.pallas_call(
        functools.partial(_combine_kernel, final_norm=final_norm),
        grid=(nt,),
        in_specs=[
            pl.BlockSpec((1, SUBLANES, LANES), lambda i: (i, 0, 0), memory_space=pltpu.SMEM),
            pl.BlockSpec((1, SUBLANES, LANES), lambda i: (jnp.minimum(i + 1, nt - 1), 0, 0), memory_space=pltpu.SMEM),
            pl.BlockSpec((CMB_TM, D), lambda i: (i, 0)),
            pl.BlockSpec((1, 1, D), lambda i: (i // tiles_per_batch, 0, 0)),
            pl.BlockSpec((2, CMB_TM), lambda i: (0, i)),
            pl.BlockSpec((1, 2, CMB_TM), lambda i: (i, 0, 0)),
            pl.BlockSpec((1, D), lambda i: (0, 0)),
            pl.BlockSpec(memory_space=pl.ANY),
        ],
        out_specs=pl.BlockSpec((CMB_TM, D), lambda i: (i, 0)),
        out_shape=jax.ShapeDtypeStruct((n, D), F32),
        scratch_shapes=[pltpu.VMEM((2, 2 * CMB_TM * PK_SUB, LANES), jnp.uint32), pltpu.SemaphoreType.DMA((2,))],
        compiler_params=pltpu.CompilerParams(
            dimension_semantics=("arbitrary",), vmem_limit_bytes=VMEM_LIMIT),
        name="moe_combine",
    )(runs, runs, x1, gate, gw, sorted_pos, fw, yg)


def _decay_tables():
    log_gamma = jnp.log1p(-jnp.exp2(-5.0 - jnp.arange(NH, dtype=F32)))
    pos = jnp.arange(RCH, dtype=F32)
    diff = pos[:, None] - pos[None, :]
    mask = jnp.where(diff >= 0, jnp.exp(log_gamma[:, None, None] * jnp.maximum(diff, 0.0)), 0.0)
    q_decay = jnp.exp(log_gamma[:, None] * (pos + 1.0))
    k_decay = jnp.exp(log_gamma[:, None] * (RCH - 1.0 - pos))
    chunk_decay = jnp.exp(log_gamma * RCH)
    return mask.astype(F32), q_decay.T, k_decay.T, chunk_decay


def kernel(x, c, positions, w_ada, b_ada, norm1_w, norm2_w, w_in, w_out, ret_norm_w, sg_w_s, sg_b_s,
           w_router, router_bias, w1, w3, w2, final_norm_w):
    bsz, seq, d = x.shape
    n = bsz * seq
    assert d == D and seq % INP_TM == 0 and seq % MIX_ROWS == 0 and n % CMB_TM == 0

    assert bsz <= SUBLANES
    c_cols = jnp.zeros((D, SUBLANES), F32).at[:, :bsz].set(c.T)
    mod = _ada_call(c_cols, w_ada, b_ada, bsz)[:, :bsz]
    mod = mod.reshape(N_LAYERS, bsz, N_MOD, 1, D)

    half = HD // 2
    inv_freq = (ROPE_THETA ** (-jnp.arange(half, dtype=F32) / half)).reshape(1, half)
    pos_col = positions.reshape(n, 1)
    decay_mask, q_decay_t, k_decay_t, chunk_decay = _decay_tables()

    perm = (jnp.arange(NE) % NEG) * EPG + jnp.arange(NE) // NEG
    wr_t = w_router.T[perm]
    rb_col = router_bias[perm].reshape(NE, 1)

    nb = n * 2 // EXP_BM + NE
    assert nb <= BLK_LANES and n % OUT_TM == 0
    xs = x.reshape(n, D)
    for l in range(N_LAYERS):
        shift1, scale1, gate1, shift2, scale2, gate2 = [mod[l, :, t] for t in range(N_MOD)]
        proj = _inproj_call(xs, shift1, scale1, norm1_w[l].reshape(1, D), w_in, l, pos_col, inv_freq, seq)
        mix = _mix_call(proj, chunk_decay, decay_mask, q_decay_t, k_decay_t, ret_norm_w[l].reshape(1, RW),
                        sg_w_s[l], sg_b_s[l].T, bsz, seq)
        x1, sorted_pos, runs, gw, owner, n_active, xg = _outproj_call(
            mix, xs, w_out[l].astype(BF16), gate1, shift2, scale2, norm2_w[l].reshape(1, D), wr_t, rb_col, seq, nb)

        owner = owner.reshape(BLK_LANES)
        ids = jnp.arange(BLK_LANES, dtype=jnp.int32)
        ahead = (owner[None, :] < owner[:, None]) | ((owner[None, :] == owner[:, None]) & (ids[None, :] < ids[:, None]))
        pos = jnp.sum(ahead, axis=1)
        at = pos[None, :] == ids[:, None]
        order = jnp.sum(jnp.where(at, ids[None, :], 0), axis=1)[:nb].astype(jnp.int32)
        owner_sorted = jnp.sum(jnp.where(at, owner[None, :], 0), axis=1)[:nb]
        block_expert = jnp.minimum(owner_sorted, NE - 1).astype(jnp.int32)
        later = jnp.min(jnp.where(owner_sorted[None, :] > owner_sorted[:, None], owner_sorted[None, :], NE), axis=1)
        next_expert = jnp.where(later >= NE, -1, later).astype(jnp.int32)

        yg = _expert_call(order, block_expert, next_expert, n_active.reshape(1), xg, w1, w3, w2, l)
        xs = _combine_call(runs, x1, gate2, gw, sorted_pos, final_norm_w.reshape(1, D), yg, seq,
                           final_norm=(l == N_LAYERS - 1))
    return xs.reshape(bsz, seq, D)
```
